```python
import jax, jax.numpy as jnp
from jax import lax
import numpy as np

D_MODEL = 1024
BATCH = 2
SEQ = 8192
DEPTH = 1

N_MEM = 256
MEM_HEADS = 4
MEM_HEAD_DIM = D_MODEL // MEM_HEADS
D_MIX = D_MODEL
GLA_WIDTH = D_MIX // 2
GLA_HEADS = 4
GLA_DV = GLA_WIDTH // GLA_HEADS
GLA_DK = GLA_DV // 2
GLA_KEY_WIDTH = GLA_HEADS * GLA_DK
GLA_GATE_RANK = 16
GLA_GATE_NORMALIZER = 16.0
GLA_CHUNK = 64
GLA_NORM_EPS = 1e-5
RWKV_WIDTH = D_MIX - GLA_WIDTH
RWKV_HEAD = 64
RWKV_HEADS = RWKV_WIDTH // RWKV_HEAD
RWKV_DECAY_RANK = 64
RWKV_AAA_RANK = 64
RWKV_GATE_RANK = 128
RWKV_LN_EPS = 64e-5
D_FF = -(-8 * D_MODEL // (3 * 256)) * 256
NORM_EPS = 1e-6

GLA_COLS = (GLA_KEY_WIDTH, GLA_KEY_WIDTH, GLA_WIDTH, GLA_WIDTH, GLA_GATE_RANK)
RWKV_COLS = (RWKV_WIDTH, RWKV_WIDTH, RWKV_WIDTH, RWKV_DECAY_RANK, RWKV_AAA_RANK, RWKV_GATE_RANK)
N_GLA_COLS = sum(GLA_COLS)
N_RWKV_COLS = sum(RWKV_COLS)
D_IN_PROJ = N_GLA_COLS + N_RWKV_COLS

kernel_name = "hymba_gla_rwkv7_memory_block"


def _split(z, cols):
    idx = tuple(int(i) for i in np.cumsum(cols)[:-1])
    return jnp.split(z, idx, axis=-1)


def rms_norm(x, g, eps=NORM_EPS):
    xf = x.astype(jnp.float32)
    y = xf * lax.rsqrt(jnp.mean(xf * xf, axis=-1, keepdims=True) + eps)
    return (y * g.astype(jnp.float32)).astype(x.dtype)


def token_shift(z):
    return jnp.pad(z[:, :-1], ((0, 0), (1, 0), (0, 0)))


def gla_chunked(q, k, v, log_a):
    B, T, H, DK = q.shape
    DV = v.shape[-1]
    C = GLA_CHUNK
    N = T // C
    q, k, log_a = (t.reshape(B, N, C, H, DK) for t in (q, k, log_a))
    v = v.reshape(B, N, C, H, DV)
    b = jnp.cumsum(log_a, axis=2)
    b_last = b[:, :, -1:]
    q_dec = q * jnp.exp(b)
    k_inv = k * jnp.exp(-b)
    k_tail = k * jnp.exp(b_last - b)
    causal = jnp.tril(jnp.ones((C, C), dtype=bool))
    scores = jnp.einsum('bnchk,bnshk->bnhcs', q_dec, k_inv)
    scores = jnp.where(causal, scores, 0.0)
    o_intra = jnp.einsum('bnhcs,bnshv->bnchv', scores, v)
    d_state = jnp.einsum('bnshk,bnshv->bnhkv', k_tail, v)
    chunk_decay = jnp.exp(b_last[:, :, 0])

    def step(S, inp):
        dec, dS = inp
        return S * dec[..., None] + dS, S

    S0 = jnp.zeros((B, H, DK, DV), d_state.dtype)
    _, S_prev = lax.scan(step, S0, (jnp.moveaxis(chunk_decay, 1, 0), jnp.moveaxis(d_state, 1, 0)))
    S_prev = jnp.moveaxis(S_prev, 0, 1)
    o_inter = jnp.einsum('bnchk,bnhkv->bnchv', q_dec, S_prev)
    return (o_intra + o_inter).reshape(B, T, H, DV)


def gla_mixer(z, wa2, ba, norm_g):
    B, T, _ = z.shape
    f32 = jnp.float32
    q, k, v, g, a_lo = _split(z, GLA_COLS)
    log_a = jax.nn.log_sigmoid((a_lo @ wa2 + ba).astype(f32)) / GLA_GATE_NORMALIZER
    q = q.astype(f32).reshape(B, T, GLA_HEADS, GLA_DK) * (GLA_DK ** -0.5)
    k = k.astype(f32).reshape(B, T, GLA_HEADS, GLA_DK)
    v = v.astype(f32).reshape(B, T, GLA_HEADS, GLA_DV)
    log_a = log_a.reshape(B, T, GLA_HEADS, GLA_DK)
    o = gla_chunked(q, k, v, log_a)
    o = o * lax.rsqrt(jnp.mean(o * o, axis=-1, keepdims=True) + GLA_NORM_EPS) * norm_g.astype(f32)
    o = o * jax.nn.silu(g.astype(f32).reshape(B, T, GLA_HEADS, GLA_DV))
    return o.reshape(B, T, GLA_WIDTH).astype(z.dtype)


def rwkv7_scan(r, w, k, v, a, b):
    B, T, H, N = r.shape

    def step(S, inp):
        r_t, w_t, k_t, v_t, a_t, b_t = inp
        sa = jnp.einsum('bhij,bhj->bhi', S, a_t)
        S = S * w_t[:, :, None, :] + sa[..., None] * b_t[:, :, None, :] + v_t[..., None] * k_t[:, :, None, :]
        return S, jnp.einsum('bhij,bhj->bhi', S, r_t)

    xs = tuple(jnp.moveaxis(t.astype(jnp.float32), 1, 0) for t in (r, w, k, v, a, b))
    S0 = jnp.zeros((B, H, N, N), jnp.float32)
    _, y = lax.scan(step, S0, xs)
    return jnp.moveaxis(y, 0, 1)


def rwkv7_mixer(z, mu, w0, w2, a0, a2, g2, k_k, k_a, r_k, ln_g, ln_b):
    B, T, _ = z.shape
    H, N = RWKV_HEADS, RWKV_HEAD
    f32 = jnp.float32
    z = z + (token_shift(z) - z) * mu
    r, k, v, w_lo, a_lo, g_lo = _split(z, RWKV_COLS)
    w_raw = -jax.nn.softplus(-(w0 + jnp.tanh(w_lo) @ w2).astype(f32)) - 0.5
    w = jnp.exp(-jnp.exp(w_raw))
    a = jax.nn.sigmoid((a0 + a_lo @ a2).astype(f32))
    g = jax.nn.sigmoid(g_lo) @ g2
    k = k.astype(f32)
    kk = (k * k_k).reshape(B, T, H, N)
    kk = kk / jnp.maximum(jnp.sqrt(jnp.sum(kk * kk, axis=-1, keepdims=True)), 1e-12)
    k = k * (1.0 + (a - 1.0) * k_a)
    hd = lambda t: t.reshape(B, T, H, N)
    r_h, w_h, k_h, v_h, a_h = hd(r.astype(f32)), hd(w), hd(k), hd(v.astype(f32)), hd(a)
    y = rwkv7_scan(r_h, w_h, k_h, v_h, -kk, kk * a_h)
    mean = jnp.mean(y, axis=-1, keepdims=True)
    var = jnp.mean(jnp.square(y - mean), axis=-1, keepdims=True)
    y = (y - mean) * lax.rsqrt(var + RWKV_LN_EPS) * ln_g.reshape(H, N) + ln_b.reshape(H, N)
    bonus = jnp.sum(r_h * k_h * r_k, axis=-1, keepdims=True) * v_h
    y = (y + bonus).reshape(B, T, RWKV_WIDTH) * g.astype(f32)
    return y.astype(z.dtype)


def memory_cross_attention(h, mem_n, wq, wkv, wo):
    B, T, _ = h.shape
    M = mem_n.shape[1]
    q = (h @ wq).reshape(B, T, MEM_HEADS, MEM_HEAD_DIM)
    k, v = jnp.split(mem_n @ wkv, 2, axis=-1)
    k = k.reshape(B, M, MEM_HEADS, MEM_HEAD_DIM)
    v = v.reshape(B, M, MEM_HEADS, MEM_HEAD_DIM)
    s = jnp.einsum('bthd,bmhd->bhtm', q, k).astype(jnp.float32) * (MEM_HEAD_DIM ** -0.5)
    p = jax.nn.softmax(s, axis=-1).astype(v.dtype)
    o = jnp.einsum('bhtm,bmhd->bthd', p, v).reshape(B, T, D_MODEL)
    return o @ wo


def swiglu_ffn(h, w_gate_up, w_down):
    gate, up = jnp.split(h @ w_gate_up, 2, axis=-1)
    return (jax.nn.silu(gate) * up) @ w_down


def hybrid_layer(x, mem, norm_mix_g, w_in, gla_wa2, gla_ba, gla_norm_g, rwkv_mu, rwkv_w0, rwkv_w2,
                 rwkv_a0, rwkv_a2, rwkv_g2, rwkv_k_k, rwkv_k_a, rwkv_r_k, rwkv_ln_g, rwkv_ln_b, w_out,
                 norm_mem_x_g, norm_mem_g, wq_mem, wkv_mem, wo_mem, norm_ffn_g, w_gate_up, w_down):
    h = rms_norm(x, norm_mix_g)
    z = h @ w_in
    o_gla = gla_mixer(z[..., :N_GLA_COLS], gla_wa2, gla_ba, gla_norm_g)
    o_rwkv = rwkv7_mixer(z[..., N_GLA_COLS:], rwkv_mu, rwkv_w0, rwkv_w2, rwkv_a0, rwkv_a2, rwkv_g2,
                         rwkv_k_k, rwkv_k_a, rwkv_r_k, rwkv_ln_g, rwkv_ln_b)
    x = x + jnp.concatenate([o_gla, o_rwkv], axis=-1) @ w_out
    x = x + memory_cross_attention(rms_norm(x, norm_mem_x_g), rms_norm(mem, norm_mem_g), wq_mem, wkv_mem, wo_mem)
    x = x + swiglu_ffn(rms_norm(x, norm_ffn_g), w_gate_up, w_down)
    return x


def setup_inputs(seed: int = 0) -> dict:
    key = jax.random.key(seed)
    ks = iter(jax.random.split(key, 32))
    f32 = jnp.float32
    L = DEPTH

    def nrm(shape, scale):
        return jax.random.normal(next(ks), shape, f32) * scale

    def gain(shape):
        return 1.0 + nrm(shape, 0.02)

    return {
        "x": nrm((BATCH, SEQ, D_MODEL), 1.0),
        "mem": nrm((BATCH, N_MEM, D_MODEL), 1.0),
        "norm_mix_g": gain((L, D_MODEL)),
        "w_in": nrm((L, D_MODEL, D_IN_PROJ), D_MODEL ** -0.5),
        "gla_wa2": nrm((L, GLA_GATE_RANK, GLA_KEY_WIDTH), GLA_GATE_RANK ** -0.5),
        "gla_ba": nrm((L, GLA_KEY_WIDTH), 0.1),
        "gla_norm_g": gain((L, GLA_DV)),
        "rwkv_mu": jax.random.uniform(next(ks), (L, N_RWKV_COLS), f32),
        "rwkv_w0": jax.random.uniform(next(ks), (L, RWKV_WIDTH), f32, -6.0, -1.0),
        "rwkv_w2": nrm((L, RWKV_DECAY_RANK, RWKV_WIDTH), 0.5 * RWKV_DECAY_RANK ** -0.5),
        "rwkv_a0": nrm((L, RWKV_WIDTH), 0.1),
        "rwkv_a2": nrm((L, RWKV_AAA_RANK, RWKV_WIDTH), RWKV_AAA_RANK ** -0.5),
        "rwkv_g2": nrm((L, RWKV_GATE_RANK, RWKV_WIDTH), RWKV_GATE_RANK ** -0.5),
        "rwkv_k_k": 0.85 + nrm((L, RWKV_WIDTH), 0.02),
        "rwkv_k_a": 1.0 + nrm((L, RWKV_WIDTH), 0.02),
        "rwkv_r_k": nrm((L, RWKV_HEADS, RWKV_HEAD), 0.1),
        "rwkv_ln_g": gain((L, RWKV_WIDTH)),
        "rwkv_ln_b": nrm((L, RWKV_WIDTH), 0.02),
        "w_out": nrm((L, D_MIX, D_MODEL), D_MIX ** -0.5),
        "norm_mem_x_g": gain((L, D_MODEL)),
        "norm_mem_g": gain((L, D_MODEL)),
        "wq_mem": nrm((L, D_MODEL, D_MODEL), D_MODEL ** -0.5),
        "wkv_mem": nrm((L, D_MODEL, 2 * D_MODEL), D_MODEL ** -0.5),
        "wo_mem": nrm((L, D_MODEL, D_MODEL), D_MODEL ** -0.5),
        "norm_ffn_g": gain((L, D_MODEL)),
        "w_gate_up": nrm((L, D_MODEL, 2 * D_FF), D_MODEL ** -0.5),
        "w_down": nrm((L, D_FF, D_MODEL), D_FF ** -0.5),
        "norm_final_g": gain((D_MODEL,)),
    }


def reference(x, mem, norm_mix_g, w_in, gla_wa2, gla_ba, gla_norm_g, rwkv_mu, rwkv_w0, rwkv_w2,
              rwkv_a0, rwkv_a2, rwkv_g2, rwkv_k_k, rwkv_k_a, rwkv_r_k, rwkv_ln_g, rwkv_ln_b, w_out,
              norm_mem_x_g, norm_mem_g, wq_mem, wkv_mem, wo_mem, norm_ffn_g, w_gate_up, w_down,
              norm_final_g):
    for l in range(DEPTH):
        x = hybrid_layer(x, mem, norm_mix_g[l], w_in[l], gla_wa2[l], gla_ba[l], gla_norm_g[l], rwkv_mu[l],
                         rwkv_w0[l], rwkv_w2[l], rwkv_a0[l], rwkv_a2[l], rwkv_g2[l], rwkv_k_k[l], rwkv_k_a[l],
                         rwkv_r_k[l], rwkv_ln_g[l], rwkv_ln_b[l], w_out[l], norm_mem_x_g[l], norm_mem_g[l],
                         wq_mem[l], wkv_mem[l], wo_mem[l], norm_ffn_g[l], w_gate_up[l], w_down[l])
    return rms_norm(x, norm_final_g)
```

```python
import functools
import math

import jax
import jax.numpy as jnp
from jax import lax
from jax.experimental import pallas as pl
from jax.experimental.pallas import tpu as pltpu

F32 = jnp.float32
BF16 = jnp.bfloat16
HIGHEST = lax.Precision.HIGHEST

MEM_HEADS = 4
GLA_HEADS = 4
GLA_DK = 64
GLA_DV = 128
GLA_GATE_RANK = 16
GLA_GATE_NORMALIZER = 16.0
GLA_NORM_EPS = 1e-5
RWKV_HEAD = 64
RWKV_DECAY_RANK = 64
RWKV_AAA_RANK = 64
RWKV_GATE_RANK = 128
RWKV_LN_EPS = 64e-5
NORM_EPS = 1e-6

CHUNK = 64
LANES = 128
VMEM_LIMIT = 56 * 1024 * 1024


def _params(n_grid_dims):
    return pltpu.CompilerParams(dimension_semantics=("arbitrary",) * n_grid_dims,
                                vmem_limit_bytes=VMEM_LIMIT)


def _dot(a, b):
    return jnp.dot(a.astype(BF16), b.astype(BF16), preferred_element_type=F32)


def _dot_nt(a, b):
    return lax.dot_general(a.astype(BF16), b.astype(BF16), (((1,), (1,)), ((), ())),
                           preferred_element_type=F32)


def _dotf(a, b):
    return jnp.dot(a, b, precision=HIGHEST, preferred_element_type=F32)


def _dotf_nt(a, b):
    return lax.dot_general(a, b, (((1,), (1,)), ((), ())), precision=HIGHEST,
                           preferred_element_type=F32)


def _dotf_tn(a, b):
    return lax.dot_general(a, b, (((0,), (0,)), ((), ())), precision=HIGHEST,
                           preferred_element_type=F32)


def _rms_norm(x, g, eps):
    return x * lax.rsqrt(jnp.mean(x * x, axis=-1, keepdims=True) + eps) * g


def _sigmoid(x):
    return 1.0 / (1.0 + jnp.exp(-x))


def _log_sigmoid(x):
    return jnp.minimum(x, 0.0) - jnp.log1p(jnp.exp(-jnp.abs(x)))


def _iota2(shape, dim):
    return lax.broadcasted_iota(jnp.int32, shape, dim)


def _tri_incl(n):
    return (_iota2((n, n), 0) >= _iota2((n, n), 1)).astype(F32)


def _mem_kv_kernel(mem_ref, g_ref, w_ref, k_ref, v_ref):
    d = k_ref.shape[-1]
    m = _rms_norm(mem_ref[...], g_ref[...], NORM_EPS)
    kv = _dot(m, w_ref[...])
    k_ref[...] = kv[:, :d].astype(BF16)
    v_ref[...] = kv[:, d:].astype(BF16)


def _mem_kv(mem2d, g, wkv):
    n, d = mem2d.shape
    return pl.pallas_call(
        _mem_kv_kernel,
        out_shape=(jax.ShapeDtypeStruct((n, d), BF16), jax.ShapeDtypeStruct((n, d), BF16)),
        compiler_params=pltpu.CompilerParams(vmem_limit_bytes=VMEM_LIMIT),
        name="mem_kv",
    )(mem2d, g, wkv)


def _in_proj_kernel(x_ref, g_ref, wg_ref, wr_ref, wa2_ref, ba_ref, mu_ref, zg_ref, zr_ref, prev_ref):
    tm = x_ref.shape[0]
    n_main = zg_ref.shape[-1] - wa2_ref.shape[-1]
    h = _rms_norm(x_ref[...], g_ref[...], NORM_EPS).astype(BF16)

    zg = jnp.dot(h, wg_ref[...], preferred_element_type=F32)
    zg_ref[:, :n_main] = zg[:, :n_main]
    pre = _dot(zg[:, n_main:], wa2_ref[...]) + ba_ref[...]
    zg_ref[:, n_main:] = _log_sigmoid(pre) * (1.0 / GLA_GATE_NORMALIZER)

    zr = jnp.dot(h, wr_ref[...], preferred_element_type=F32)

    @pl.when(pl.program_id(1) == 0)
    def _():
        prev_ref[...] = jnp.zeros_like(prev_ref)

    rolled = pltpu.roll(zr, shift=1, axis=0)
    first = _iota2(zr.shape, 0) == 0
    shifted = jnp.where(first, jnp.broadcast_to(prev_ref[0:1, :], zr.shape), rolled)
    zr_ref[...] = zr + (shifted - zr) * mu_ref[...]
    prev_ref[0:1, :] = zr[tm - 1:tm, :]


def _in_proj(x, g, w_gla, w_rwkv, wa2p, ba, mu, tm):
    b, t, d = x.shape
    ng = w_gla.shape[1] - LANES + wa2p.shape[1]
    nr = w_rwkv.shape[1]
    const = lambda *shape: pl.BlockSpec(shape, lambda i, j: (0,) * len(shape))
    return pl.pallas_call(
        _in_proj_kernel,
        grid=(b, t // tm),
        in_specs=[
            pl.BlockSpec((None, tm, d), lambda i, j: (i, j, 0)),
            const(1, d), const(*w_gla.shape), const(*w_rwkv.shape), const(*wa2p.shape),
            const(1, wa2p.shape[1]), const(1, nr),
        ],
        out_specs=(pl.BlockSpec((None, tm, ng), lambda i, j: (i, j, 0)),
                   pl.BlockSpec((None, tm, nr), lambda i, j: (i, j, 0))),
        out_shape=(jax.ShapeDtypeStruct((b, t, ng), F32), jax.ShapeDtypeStruct((b, t, nr), F32)),
        scratch_shapes=[pltpu.VMEM((8, nr), F32)],
        compiler_params=_params(2),
        name="in_proj",
    )(x, g, w_gla, w_rwkv, wa2p, ba, mu)


def _gla_kernel(z_ref, ng_ref, o_ref, s_ref):
    tt = z_ref.shape[0]
    kw = GLA_HEADS * GLA_DK
    vw = GLA_HEADS * GLA_DV
    c = CHUNK

    @pl.when(pl.program_id(1) == 0)
    def _():
        s_ref[...] = jnp.zeros_like(s_ref)

    tri = _tri_incl(c)
    causal = _iota2((c, c), 0) >= _iota2((c, c), 1)
    norm_g = ng_ref[...]

    def chunk(ci, carry):
        r0 = pl.multiple_of(ci * c, c)
        rows = pl.ds(r0, c)
        q = z_ref[rows, 0:kw] * (GLA_DK ** -0.5)
        k = z_ref[rows, kw:2 * kw]
        v = z_ref[rows, 2 * kw:2 * kw + vw]
        gate = z_ref[rows, 2 * kw + vw:2 * kw + 2 * vw]
        log_a = z_ref[rows, 2 * kw + 2 * vw:3 * kw + 2 * vw]
        bcum = _dotf(tri, log_a)
        blast = bcum[c - 1:c, :]
        q_dec = q * jnp.exp(bcum)
        k_inv = k * jnp.exp(-bcum)
        k_tail = k * jnp.exp(blast - bcum)
        dec = jnp.exp(blast)
        outs = []
        for h in range(GLA_HEADS):
            ks = slice(h * GLA_DK, (h + 1) * GLA_DK)
            vs = slice(h * GLA_DV, (h + 1) * GLA_DV)
            s_prev = s_ref[h]
            scores = jnp.where(causal, _dot_nt(q_dec[:, ks], k_inv[:, ks]), 0.0)
            o = _dot(scores, v[:, vs]) + _dot_nt(q_dec[:, ks], s_prev)
            d_state = lax.dot_general(v[:, vs].astype(BF16), k_tail[:, ks].astype(BF16),
                                      (((0,), (0,)), ((), ())), preferred_element_type=F32)
            s_ref[h] = s_prev * dec[:, ks] + d_state
            o = o * lax.rsqrt(jnp.mean(o * o, axis=-1, keepdims=True) + GLA_NORM_EPS) * norm_g
            gh = gate[:, vs]
            outs.append(o * (gh * _sigmoid(gh)))
        o_ref[rows, :] = jnp.concatenate(outs, axis=-1).astype(o_ref.dtype)
        return carry

    lax.fori_loop(0, tt // c, chunk, 0)


def _gla(zg, norm_g, tt):
    b, t, n = zg.shape
    vw = GLA_HEADS * GLA_DV
    return pl.pallas_call(
        _gla_kernel,
        grid=(b, t // tt),
        in_specs=[pl.BlockSpec((None, tt, n), lambda i, j: (i, j, 0)),
                  pl.BlockSpec((1, GLA_DV), lambda i, j: (0, 0))],
        out_specs=pl.BlockSpec((None, tt, vw), lambda i, j: (i, j, 0)),
        out_shape=jax.ShapeDtypeStruct((b, t, vw), BF16),
        scratch_shapes=[pltpu.VMEM((GLA_HEADS, GLA_DV, GLA_DK), F32)],
        compiler_params=_params(2),
        name="gla",
    )(zg, norm_g)


def _seg_sum(x, seg_ones):
    hi = x.astype(BF16)
    lo = (x - hi.astype(F32)).astype(BF16)
    blk = seg_ones.shape[0]
    parts = []
    for s in range(x.shape[1] // blk):
        sl = slice(s * blk, (s + 1) * blk)
        parts.append(jnp.dot(hi[:, sl], seg_ones, preferred_element_type=F32)
                     + jnp.dot(lo[:, sl], seg_ones, preferred_element_type=F32))
    return jnp.concatenate(parts, axis=-1)


def _tri_inverse(a, eye, blk16, blk32):
    ad = jnp.where(blk16, a, 0.0)
    a2 = _dotf(ad, ad)
    a4 = _dotf(a2, a2)
    a8 = _dotf(a4, a4)
    t = eye + ad
    t = t + _dotf(t, a2)
    t = t + _dotf(t, a4)
    t = t + _dotf(t, a8)
    e1 = jnp.where(jnp.logical_and(blk32, jnp.logical_not(blk16)), a, 0.0)
    t = t + _dotf(_dotf(t, e1), t)
    e2 = jnp.where(blk32, 0.0, a)
    t = t + _dotf(_dotf(t, e2), t)
    return t


def _rwkv_kernel(z_ref, w0_ref, w2_ref, a0_ref, a2_ref, g2_ref, kk_ref, ka_ref, rk_ref, lng_ref, lnb_ref,
                 o_ref, s_ref, st_ref, y_ref):
    tt = z_ref.shape[0]
    w = w0_ref.shape[-1]
    n = RWKV_HEAD
    heads = w // n
    c = CHUNK
    o_wl = 3 * w
    o_al = o_wl + RWKV_DECAY_RANK
    o_gl = o_al + RWKV_AAA_RANK

    @pl.when(pl.program_id(1) == 0)
    def _():
        s_ref[...] = jnp.zeros_like(s_ref)

    seg = 2 * LANES
    seg_ones = (_iota2((seg, seg), 0) // n == _iota2((seg, seg), 1) // n).astype(BF16)

    r = z_ref[:, 0:w]
    k = z_ref[:, w:2 * w]
    v = z_ref[:, 2 * w:3 * w]
    u = w0_ref[...] + _dot(jnp.tanh(z_ref[:, o_wl:o_al]), w2_ref[...])
    lw = -math.exp(-0.5) * _sigmoid(u)
    a = _sigmoid(a0_ref[...] + _dot(z_ref[:, o_al:o_gl], a2_ref[...]))
    kk = k * kk_ref[...]
    kk = kk / jnp.maximum(jnp.sqrt(_seg_sum(kk * kk, seg_ones)), 1e-12)
    k2 = k * (1.0 + (a - 1.0) * ka_ref[...])
    st_ref[0] = r
    st_ref[1] = lw
    st_ref[2] = k2
    st_ref[3] = v
    st_ref[4] = -kk
    st_ref[5] = kk * a

    tri = _tri_incl(c)
    row = _iota2((c, c), 0)
    col = _iota2((c, c), 1)
    lower = row >= col
    strict = row > col
    eye = (row == col).astype(F32)
    blk16 = (row // 16) == (col // 16)
    blk32 = (row // 32) == (col // 32)

    def chunk(ci, carry):
        r0 = pl.multiple_of(ci * c, c)
        rows = pl.ds(r0, c)
        rc = st_ref[0, rows, :]
        lwc = st_ref[1, rows, :]
        kc = st_ref[2, rows, :]
        vc = st_ref[3, rows, :]
        ac = st_ref[4, rows, :]
        bc = st_ref[5, rows, :]
        cum = _dotf(tri, lwc)
        last = cum[c - 1:c, :]
        e_neg = jnp.exp(-cum)
        e_tail = jnp.exp(last - cum)
        r_t = rc * jnp.exp(cum)
        a_t = ac * jnp.exp(cum - lwc)
        b_t = bc * e_neg
        k_t = kc * e_neg
        b_h = bc * e_tail
        k_h = kc * e_tail
        g_c = jnp.exp(last)
        for h in range(heads):
            hs = slice(h * n, (h + 1) * n)
            at_h, rt_h, bt_h, kt_h, v_h = a_t[:, hs], r_t[:, hs], b_t[:, hs], k_t[:, hs], vc[:, hs]
            a_ab = jnp.where(strict, _dotf_nt(at_h, bt_h), 0.0)
            a_ak = jnp.where(strict, _dotf_nt(at_h, kt_h), 0.0)
            q_b = jnp.where(lower, _dotf_nt(rt_h, bt_h), 0.0)
            q_k = jnp.where(lower, _dotf_nt(rt_h, kt_h), 0.0)
            t_inv = _tri_inverse(a_ab, eye, blk16, blk32)
            w_mat = _dotf(t_inv, at_h)
            u0 = _dotf(t_inv, _dotf(a_ak, v_h))
            s0 = s_ref[h]
            u_mat = _dotf_nt(w_mat, s0) + u0
            y = _dotf(q_b, u_mat) + _dotf(q_k, v_h) + _dotf_nt(rt_h, s0)
            s_ref[h] = s0 * g_c[:, hs] + _dotf_tn(u_mat, b_h[:, hs]) + _dotf_tn(v_h, k_h[:, hs])
            y_ref[rows, hs] = y
        return carry

    lax.fori_loop(0, tt // c, chunk, 0)

    y = y_ref[...]
    mean = _seg_sum(y, seg_ones) * (1.0 / n)
    dlt = y - mean
    var = _seg_sum(dlt * dlt, seg_ones) * (1.0 / n)
    yn = dlt * lax.rsqrt(var + RWKV_LN_EPS) * lng_ref[...] + lnb_ref[...]
    bonus = _seg_sum(st_ref[0] * st_ref[2] * rk_ref[...], seg_ones) * st_ref[3]
    gate = _dot(_sigmoid(z_ref[:, o_gl:]), g2_ref[...])
    o_ref[...] = ((yn + bonus) * gate).astype(o_ref.dtype)


def _rwkv(zr, w0, w2, a0, a2, g2, k_k, k_a, r_k, ln_g, ln_b, tt):
    b, t, nz = zr.shape
    w = w0.shape[-1]
    heads = w // RWKV_HEAD
    const = lambda *shape: pl.BlockSpec(shape, lambda i, j: (0,) * len(shape))
    return pl.pallas_call(
        _rwkv_kernel,
        grid=(b, t // tt),
        in_specs=[pl.BlockSpec((None, tt, nz), lambda i, j: (i, j, 0)),
                  const(1, w), const(*w2.shape), const(1, w), const(*a2.shape), const(*g2.shape),
                  const(1, w), const(1, w), const(1, w), const(1, w), const(1, w)],
        out_specs=pl.BlockSpec((None, tt, w), lambda i, j: (i, j, 0)),
        out_shape=jax.ShapeDtypeStruct((b, t, w), BF16),
        scratch_shapes=[pltpu.VMEM((heads, RWKV_HEAD, RWKV_HEAD), F32),
                        pltpu.VMEM((6, tt, w), F32),
                        pltpu.VMEM((tt, w), F32)],
        compiler_params=_params(2),
        name="rwkv",
    )(zr, w0, w2, a0, a2, g2, k_k, k_a, r_k, ln_g, ln_b)


def _post_attn_kernel(x_ref, og_ref, or_ref, wog_ref, wor_ref, gx_ref, wq_ref, k_ref, v_ref, wo_ref, out_ref):
    d = x_ref.shape[-1]
    hd = d // MEM_HEADS
    x1 = (x_ref[...] + jnp.dot(og_ref[...], wog_ref[...], preferred_element_type=F32)
          + jnp.dot(or_ref[...], wor_ref[...], preferred_element_type=F32))
    q = _dot(_rms_norm(x1, gx_ref[...], NORM_EPS), wq_ref[...]).astype(BF16)
    heads = []
    for h in range(MEM_HEADS):
        hs = slice(h * hd, (h + 1) * hd)
        s = lax.dot_general(q[:, hs], k_ref[:, hs], (((1,), (1,)), ((), ())),
                            preferred_element_type=F32) * (hd ** -0.5)
        e = jnp.exp(s - jnp.max(s, axis=-1, keepdims=True))
        p = e / jnp.sum(e, axis=-1, keepdims=True)
        heads.append(_dot(p, v_ref[:, hs]))
    o = jnp.concatenate(heads, axis=-1)
    out_ref[...] = x1 + _dot(o, wo_ref[...])


def _post_attn(x, o_gla, o_rwkv, wo_g, wo_r, gx, wq, kmem, vmem, wo, tm):
    b, t, d = x.shape
    m = kmem.shape[1]
    const = lambda *shape: pl.BlockSpec(shape, lambda i, j: (0,) * len(shape))
    tile = lambda width: pl.BlockSpec((None, tm, width), lambda i, j: (i, j, 0))
    return pl.pallas_call(
        _post_attn_kernel,
        grid=(b, t // tm),
        in_specs=[tile(d), tile(o_gla.shape[-1]), tile(o_rwkv.shape[-1]),
                  const(*wo_g.shape), const(*wo_r.shape), const(1, d), const(*wq.shape),
                  pl.BlockSpec((None, m, d), lambda i, j: (i, 0, 0)),
                  pl.BlockSpec((None, m, d), lambda i, j: (i, 0, 0)),
                  const(*wo.shape)],
        out_specs=tile(d),
        out_shape=jax.ShapeDtypeStruct((b, t, d), F32),
        compiler_params=_params(2),
        name="post_attn",
    )(x, o_gla, o_rwkv, wo_g, wo_r, gx, wq, kmem, vmem, wo)


def _ffn_kernel(x_ref, g_ref, wgu_ref, wd_ref, gf_ref, out_ref, *, ff_chunk):
    x = x_ref[...]
    d_ff = wd_ref.shape[0]
    h = _rms_norm(x, g_ref[...], NORM_EPS).astype(BF16)
    acc = x
    for c0 in range(0, d_ff, ff_chunk):
        gate = jnp.dot(h, wgu_ref[:, c0:c0 + ff_chunk], preferred_element_type=F32)
        up = jnp.dot(h, wgu_ref[:, d_ff + c0:d_ff + c0 + ff_chunk], preferred_element_type=F32)
        act = (gate * _sigmoid(gate) * up).astype(BF16)
        acc = acc + jnp.dot(act, wd_ref[c0:c0 + ff_chunk, :], preferred_element_type=F32)
    out_ref[...] = _rms_norm(acc, gf_ref[...], NORM_EPS)


def _ffn(x, g, wgu, wd, gf, tm, ff_chunk):
    b, t, d = x.shape
    const = lambda *shape: pl.BlockSpec(shape, lambda i, j: (0,) * len(shape))
    tile = pl.BlockSpec((None, tm, d), lambda i, j: (i, j, 0))
    return pl.pallas_call(
        functools.partial(_ffn_kernel, ff_chunk=ff_chunk),
        grid=(b, t // tm),
        in_specs=[tile, const(1, d), const(*wgu.shape), const(*wd.shape), const(1, d)],
        out_specs=tile,
        out_shape=jax.ShapeDtypeStruct((b, t, d), F32),
        compiler_params=_params(2),
        name="ffn",
    )(x, g, wgu, wd, gf)


def _layer(x, mem, norm_mix_g, w_in, gla_wa2, gla_ba, gla_norm_g, rwkv_mu, rwkv_w0, rwkv_w2, rwkv_a0,
           rwkv_a2, rwkv_g2, rwkv_k_k, rwkv_k_a, rwkv_r_k, rwkv_ln_g, rwkv_ln_b, w_out, norm_mem_x_g,
           norm_mem_g, wq_mem, wkv_mem, wo_mem, norm_ffn_g, w_gate_up, w_down, final_g):
    b, t, d = x.shape
    m = mem.shape[1]
    row = lambda p: p.reshape(1, -1).astype(F32)
    kw = GLA_HEADS * GLA_DK
    vw = GLA_HEADS * GLA_DV
    n_gla_main = 2 * kw + 2 * vw
    n_gla = n_gla_main + GLA_GATE_RANK
    pad = LANES - GLA_GATE_RANK
    w_gla = jnp.pad(w_in[:, :n_gla], ((0, 0), (0, pad))).astype(BF16)
    w_rwkv = w_in[:, n_gla:].astype(BF16)
    wa2p = jnp.pad(gla_wa2, ((0, pad), (0, 0))).astype(BF16)

    kmem, vmem = _mem_kv(mem.reshape(b * m, d), row(norm_mem_g), wkv_mem.astype(BF16))
    zg, zr = _in_proj(x, row(norm_mix_g), w_gla, w_rwkv, wa2p, row(gla_ba), row(rwkv_mu), tm=256)
    o_gla = _gla(zg, row(gla_norm_g), tt=512)
    o_rwkv = _rwkv(zr, row(rwkv_w0), rwkv_w2.astype(BF16), row(rwkv_a0), rwkv_a2.astype(BF16),
                   rwkv_g2.astype(BF16), row(rwkv_k_k), row(rwkv_k_a), row(rwkv_r_k), row(rwkv_ln_g),
                   row(rwkv_ln_b), tt=512)
    w_out_b = w_out.astype(BF16)
    x2 = _post_attn(x, o_gla, o_rwkv, w_out_b[:vw], w_out_b[vw:], row(norm_mem_x_g), wq_mem.astype(BF16),
                    kmem.reshape(b, m, d), vmem.reshape(b, m, d), wo_mem.astype(BF16), tm=512)
    return _ffn(x2, row(norm_ffn_g), w_gate_up.astype(BF16), w_down.astype(BF16), row(final_g),
                tm=512, ff_chunk=256)


def kernel(x, mem, norm_mix_g, w_in, gla_wa2, gla_ba, gla_norm_g, rwkv_mu, rwkv_w0, rwkv_w2, rwkv_a0, rwkv_a2, rwkv_g2, rwkv_k_k, rwkv_k_a, rwkv_r_k, rwkv_ln_g, rwkv_ln_b, w_out, norm_mem_x_g, norm_mem_g, wq_mem, wkv_mem, wo_mem, norm_ffn_g, w_gate_up, w_down, norm_final_g):
    assert norm_mix_g.shape[0] == 1, "single-layer block"
    return _layer(x, mem, norm_mix_g[0], w_in[0], gla_wa2[0], gla_ba[0], gla_norm_g[0], rwkv_mu[0],
                  rwkv_w0[0], rwkv_w2[0], rwkv_a0[0], rwkv_a2[0], rwkv_g2[0], rwkv_k_k[0], rwkv_k_a[0],
                  rwkv_r_k[0], rwkv_ln_g[0], rwkv_ln_b[0], w_out[0], norm_mem_x_g[0], norm_mem_g[0],
                  wq_mem[0], wkv_mem[0], wo_mem[0], norm_ffn_g[0], w_gate_up[0], w_down[0], norm_final_g)
```

```python
import functools
import math

import jax
import jax.numpy as jnp
from jax import lax
from jax.experimental import pallas as pl
from jax.experimental.pallas import tpu as pltpu

F32 = jnp.float32
BF16 = jnp.bfloat16
HIGHEST = lax.Precision.HIGHEST

MEM_HEADS = 4
GLA_HEADS = 4
GLA_DK = 64
GLA_DV = 128
GLA_GATE_RANK = 16
GLA_GATE_NORMALIZER = 16.0
GLA_NORM_EPS = 1e-5
RWKV_HEAD = 64
RWKV_DECAY_RANK = 64
RWKV_AAA_RANK = 64
RWKV_GATE_RANK = 128
RWKV_LN_EPS = 64e-5
NORM_EPS = 1e-6

CHUNK = 64
LANES = 128
VMEM_LIMIT = 56 * 1024 * 1024


def _params(n_grid_dims):
    return pltpu.CompilerParams(dimension_semantics=("arbitrary",) * n_grid_dims,
                                vmem_limit_bytes=VMEM_LIMIT)


def _dot(a, b):
    return jnp.dot(a.astype(BF16), b.astype(BF16), preferred_element_type=F32)


def _dot_nt(a, b):
    return lax.dot_general(a.astype(BF16), b.astype(BF16), (((1,), (1,)), ((), ())),
                           preferred_element_type=F32)


def _dotf(a, b):
    return jnp.dot(a, b, precision=HIGHEST, preferred_element_type=F32)


def _rms_norm(x, g, eps):
    return x * lax.rsqrt(jnp.mean(x * x, axis=-1, keepdims=True) + eps) * g


def _sigmoid(x):
    return 1.0 / (1.0 + jnp.exp(-x))


def _log_sigmoid(x):
    return jnp.minimum(x, 0.0) - jnp.log1p(jnp.exp(-jnp.abs(x)))


def _iota2(shape, dim):
    return lax.broadcasted_iota(jnp.int32, shape, dim)


def _tri_incl(n):
    return (_iota2((n, n), 0) >= _iota2((n, n), 1)).astype(F32)


def _mem_kv_kernel(mem_ref, g_ref, w_ref, k_ref, v_ref):
    d = k_ref.shape[-1]
    m = _rms_norm(mem_ref[...], g_ref[...], NORM_EPS)
    kv = _dot(m, w_ref[...])
    k_ref[...] = kv[:, :d].astype(BF16)
    v_ref[...] = kv[:, d:].astype(BF16)


def _mem_kv(mem2d, g, wkv):
    n, d = mem2d.shape
    return pl.pallas_call(
        _mem_kv_kernel,
        out_shape=(jax.ShapeDtypeStruct((n, d), BF16), jax.ShapeDtypeStruct((n, d), BF16)),
        compiler_params=pltpu.CompilerParams(vmem_limit_bytes=VMEM_LIMIT),
        name="mem_kv",
    )(mem2d, g, wkv)


def _in_proj_kernel(x_ref, g_ref, wg_ref, wr_ref, wa2_ref, ba_ref, mu_ref, zg_ref, zr_ref, prev_ref):
    tm = x_ref.shape[0]
    n_main = zg_ref.shape[-1] - wa2_ref.shape[-1]
    h = _rms_norm(x_ref[...], g_ref[...], NORM_EPS).astype(BF16)

    zg = jnp.dot(h, wg_ref[...], preferred_element_type=F32)
    zg_ref[:, :n_main] = zg[:, :n_main]
    pre = _dot(zg[:, n_main:], wa2_ref[...]) + ba_ref[...]
    zg_ref[:, n_main:] = _log_sigmoid(pre) * (1.0 / GLA_GATE_NORMALIZER)

    zr = jnp.dot(h, wr_ref[...], preferred_element_type=F32)

    @pl.when(pl.program_id(1) == 0)
    def _():
        prev_ref[...] = jnp.zeros_like(prev_ref)

    rolled = pltpu.roll(zr, shift=1, axis=0)
    first = _iota2(zr.shape, 0) == 0
    shifted = jnp.where(first, jnp.broadcast_to(prev_ref[0:1, :], zr.shape), rolled)
    zr_ref[...] = zr + (shifted - zr) * mu_ref[...]
    prev_ref[0:1, :] = zr[tm - 1:tm, :]


def _in_proj(x, g, w_gla, w_rwkv, wa2p, ba, mu, tm):
    b, t, d = x.shape
    ng = w_gla.shape[1] - LANES + wa2p.shape[1]
    nr = w_rwkv.shape[1]
    const = lambda *shape: pl.BlockSpec(shape, lambda i, j: (0,) * len(shape))
    return pl.pallas_call(
        _in_proj_kernel,
        grid=(b, t // tm),
        in_specs=[
            pl.BlockSpec((None, tm, d), lambda i, j: (i, j, 0)),
            const(1, d), const(*w_gla.shape), const(*w_rwkv.shape), const(*wa2p.shape),
            const(1, wa2p.shape[1]), const(1, nr),
        ],
        out_specs=(pl.BlockSpec((None, tm, ng), lambda i, j: (i, j, 0)),
                   pl.BlockSpec((None, tm, nr), lambda i, j: (i, j, 0))),
        out_shape=(jax.ShapeDtypeStruct((b, t, ng), F32), jax.ShapeDtypeStruct((b, t, nr), F32)),
        scratch_shapes=[pltpu.VMEM((8, nr), F32)],
        compiler_params=_params(2),
        name="in_proj",
    )(x, g, w_gla, w_rwkv, wa2p, ba, mu)


def _gla_kernel(z_ref, ng_ref, o_ref, s_ref):
    tt = z_ref.shape[0]
    kw = GLA_HEADS * GLA_DK
    vw = GLA_HEADS * GLA_DV
    c = CHUNK

    @pl.when(pl.program_id(1) == 0)
    def _():
        s_ref[...] = jnp.zeros_like(s_ref)

    tri = _tri_incl(c)
    causal = _iota2((c, c), 0) >= _iota2((c, c), 1)
    norm_g = ng_ref[...]

    def chunk(ci, carry):
        r0 = pl.multiple_of(ci * c, c)
        rows = pl.ds(r0, c)
        q = z_ref[rows, 0:kw] * (GLA_DK ** -0.5)
        k = z_ref[rows, kw:2 * kw]
        v = z_ref[rows, 2 * kw:2 * kw + vw]
        gate = z_ref[rows, 2 * kw + vw:2 * kw + 2 * vw]
        log_a = z_ref[rows, 2 * kw + 2 * vw:3 * kw + 2 * vw]
        bcum = _dotf(tri, log_a)
        blast = bcum[c - 1:c, :]
        q_dec = q * jnp.exp(bcum)
        k_inv = k * jnp.exp(-bcum)
        k_tail = k * jnp.exp(blast - bcum)
        dec = jnp.exp(blast)
        outs = []
        for h in range(GLA_HEADS):
            ks = slice(h * GLA_DK, (h + 1) * GLA_DK)
            vs = slice(h * GLA_DV, (h + 1) * GLA_DV)
            s_prev = s_ref[h]
            scores = jnp.where(causal, _dot_nt(q_dec[:, ks], k_inv[:, ks]), 0.0)
            o = _dot(scores, v[:, vs]) + _dot_nt(q_dec[:, ks], s_prev)
            d_state = lax.dot_general(v[:, vs].astype(BF16), k_tail[:, ks].astype(BF16),
                                      (((0,), (0,)), ((), ())), preferred_element_type=F32)
            s_ref[h] = s_prev * dec[:, ks] + d_state
            o = o * lax.rsqrt(jnp.mean(o * o, axis=-1, keepdims=True) + GLA_NORM_EPS) * norm_g
            gh = gate[:, vs]
            outs.append(o * (gh * _sigmoid(gh)))
        o_ref[rows, :] = jnp.concatenate(outs, axis=-1).astype(o_ref.dtype)
        return carry

    lax.fori_loop(0, tt // c, chunk, 0)


def _gla(zg, norm_g, tt):
    b, t, n = zg.shape
    vw = GLA_HEADS * GLA_DV
    return pl.pallas_call(
        _gla_kernel,
        grid=(b, t // tt),
        in_specs=[pl.BlockSpec((None, tt, n), lambda i, j: (i, j, 0)),
                  pl.BlockSpec((1, GLA_DV), lambda i, j: (0, 0))],
        out_specs=pl.BlockSpec((None, tt, vw), lambda i, j: (i, j, 0)),
        out_shape=jax.ShapeDtypeStruct((b, t, vw), BF16),
        scratch_shapes=[pltpu.VMEM((GLA_HEADS, GLA_DV, GLA_DK), F32)],
        compiler_params=_params(2),
        name="gla",
    )(zg, norm_g)


def _split(x):
    hi = x.astype(BF16)
    lo = (x - hi.astype(F32)).astype(BF16)
    return hi, lo


def _seg_sum(x, seg_ones):
    hi, lo = _split(x)
    blk = seg_ones.shape[0]
    parts = []
    for s in range(x.shape[1] // blk):
        sl = slice(s * blk, (s + 1) * blk)
        parts.append(jnp.dot(hi[:, sl], seg_ones, preferred_element_type=F32)
                     + jnp.dot(lo[:, sl], seg_ones, preferred_element_type=F32))
    return jnp.concatenate(parts, axis=-1)


class _Lhs:
    def __init__(self, x):
        self.hi, lo = _split(x)
        self.cat = jnp.concatenate([self.hi, lo], axis=1)


def _block_diag(z, lane_lo):
    zero = jnp.zeros_like(z)
    return jnp.concatenate([jnp.where(lane_lo, z, zero), jnp.where(lane_lo, zero, z)], axis=0)


class _Rhs:
    def __init__(self, zs, lane_lo):
        his, los = [], []
        for z in zs:
            hi, lo = _split(z)
            his.append(_block_diag(hi, lane_lo))
            los.append(_block_diag(lo, lane_lo))
        hi = his[0] if len(his) == 1 else jnp.concatenate(his, axis=1)
        self.lo = los[0] if len(los) == 1 else jnp.concatenate(los, axis=1)
        self.hh = jnp.concatenate([hi, hi], axis=0)


def _mm(l, r):
    return (jnp.dot(l.cat, r.hh, preferred_element_type=F32)
            + jnp.dot(l.hi, r.lo, preferred_element_type=F32))


def _mm_nt(l, r_hi, r_lo):
    dims = (((1,), (1,)), ((), ()))
    return (lax.dot_general(l.cat, jnp.concatenate([r_hi, r_hi], axis=1), dims, preferred_element_type=F32)
            + lax.dot_general(l.hi, r_lo, dims, preferred_element_type=F32))


def _rwkv_kernel(z_ref, w0_ref, w2_ref, a0_ref, a2_ref, g2_ref, kk_ref, ka_ref, rk_ref, lng_ref, lnb_ref,
                 o_ref, s_ref, st_ref, y_ref):
    tt = z_ref.shape[0]
    w = w0_ref.shape[-1]
    n = RWKV_HEAD
    pairs = w // LANES
    c = CHUNK
    o_wl = 3 * w
    o_al = o_wl + RWKV_DECAY_RANK
    o_gl = o_al + RWKV_AAA_RANK

    @pl.when(pl.program_id(1) == 0)
    def _():
        s_ref[...] = jnp.zeros_like(s_ref)

    seg = 2 * LANES
    seg_ones = (_iota2((seg, seg), 0) // n == _iota2((seg, seg), 1) // n).astype(BF16)

    r = z_ref[:, 0:w]
    k = z_ref[:, w:2 * w]
    v = z_ref[:, 2 * w:3 * w]
    u = w0_ref[...] + _dot(jnp.tanh(z_ref[:, o_wl:o_al]), w2_ref[...])
    lw = -math.exp(-0.5) * _sigmoid(u)
    a = _sigmoid(a0_ref[...] + _dot(z_ref[:, o_al:o_gl], a2_ref[...]))
    kk = k * kk_ref[...]
    kk = kk / jnp.maximum(jnp.sqrt(_seg_sum(kk * kk, seg_ones)), 1e-12)
    k2 = k * (1.0 + (a - 1.0) * ka_ref[...])
    st_ref[0] = r
    st_ref[1] = lw
    st_ref[2] = k2
    st_ref[3] = v
    st_ref[4] = -kk
    st_ref[5] = kk * a

    tri3 = (_iota2((c, 3 * c), 0) >= _iota2((c, 3 * c), 1) % c).astype(BF16)
    row = _iota2((c, LANES), 0)
    col = _iota2((c, LANES), 1) % c
    lower = row >= col
    strict = row > col
    eye = (row == col).astype(F32)
    blk16 = (row // 16) == (col // 16)
    blk32 = (row // 32) == (col // 32)
    off16 = jnp.logical_and(blk32, jnp.logical_not(blk16))
    lane_lo = _iota2((c, LANES), 1) < n
    bd_mask = (_iota2((LANES, LANES), 0) // n) == (_iota2((LANES, LANES), 1) // n)
    prange = range(pairs)

    def chunk(ci, carry):
        r0 = pl.multiple_of(ci * c, c)
        rows = pl.ds(r0, c)
        rc = st_ref[0, rows, :]
        lwc = st_ref[1, rows, :]
        kc = st_ref[2, rows, :]
        vc = st_ref[3, rows, :]
        ac = st_ref[4, rows, :]
        bc = st_ref[5, rows, :]
        l1 = lwc.astype(BF16)
        rem = lwc - l1.astype(F32)
        l2 = rem.astype(BF16)
        l3 = (rem - l2.astype(F32)).astype(BF16)
        cum = jnp.dot(tri3, jnp.concatenate([l1, l2, l3], axis=0), preferred_element_type=F32)
        last = cum[c - 1:c, :]
        e_neg = jnp.exp(-cum)
        e_tail = jnp.exp(last - cum)
        r_t = rc * jnp.exp(cum)
        a_t = ac * jnp.exp(cum - lwc)
        b_t = bc * e_neg
        k_t = kc * e_neg
        b_h = bc * e_tail
        k_h = kc * e_tail
        g_c = jnp.exp(last)
        ps = [slice(p * LANES, (p + 1) * LANES) for p in prange]

        a_ab, a_ak, q_b, q_k = [], [], [], []
        for p in prange:
            lhs = _Lhs(jnp.concatenate([a_t[:, ps[p]], r_t[:, ps[p]]], axis=0))
            bth, btl = _split(b_t[:, ps[p]])
            kth, ktl = _split(k_t[:, ps[p]])
            r_hi = jnp.concatenate([_block_diag(bth, lane_lo), _block_diag(kth, lane_lo)], axis=0)
            r_lo = jnp.concatenate([_block_diag(btl, lane_lo), _block_diag(ktl, lane_lo)], axis=0)
            aa = _mm_nt(lhs, r_hi, r_lo)
            a_ab.append(jnp.where(strict, aa[:c, :LANES], 0.0))
            a_ak.append(jnp.where(strict, aa[:c, LANES:], 0.0))
            q_b.append(jnp.where(lower, aa[c:, :LANES], 0.0))
            q_k.append(jnp.where(lower, aa[c:, LANES:], 0.0))

        ad = [jnp.where(blk16, x, 0.0) for x in a_ab]
        ad_l = [_Lhs(x) for x in ad]
        ad_r = [_Rhs([x], lane_lo) for x in ad]
        a2 = [_mm(ad_l[p], ad_r[p]) for p in prange]
        a2_l = [_Lhs(x) for x in a2]
        a2_r = [_Rhs([x], lane_lo) for x in a2]
        a4 = [_mm(a2_l[p], a2_r[p]) for p in prange]
        a4_l = [_Lhs(x) for x in a4]
        a4_r = [_Rhs([x], lane_lo) for x in a4]
        a8 = [_mm(a4_l[p], a4_r[p]) for p in prange]
        a8_r = [_Rhs([x], lane_lo) for x in a8]
        t = [eye + ad[p] for p in prange]
        t = [t[p] + _mm(_Lhs(t[p]), a2_r[p]) for p in prange]
        t = [t[p] + _mm(_Lhs(t[p]), a4_r[p]) for p in prange]
        t = [t[p] + _mm(_Lhs(t[p]), a8_r[p]) for p in prange]
        for msk in (off16, jnp.logical_not(blk32)):
            t_l = [_Lhs(t[p]) for p in prange]
            te = [_mm(t_l[p], _Rhs([jnp.where(msk, a_ab[p], 0.0)], lane_lo)) for p in prange]
            t = [t[p] + _mm(_Lhs(te[p]), _Rhs([t[p]], lane_lo)) for p in prange]

        pv = [_mm(_Lhs(a_ak[p]), _Rhs([vc[:, ps[p]]], lane_lo)) for p in prange]
        wu = [_mm(_Lhs(t[p]), _Rhs([a_t[:, ps[p]], pv[p]], lane_lo)) for p in prange]

        for p in prange:
            s0 = s_ref[p]
            s0h, s0l = _split(s0)
            u_mat = _mm_nt(_Lhs(wu[p][:, :LANES]), s0h, s0l) + wu[p][:, LANES:]
            v_p = vc[:, ps[p]]
            uv_r = _Rhs([u_mat], lane_lo), _Rhs([v_p], lane_lo)
            y = (_mm(_Lhs(q_b[p]), uv_r[0]) + _mm(_Lhs(q_k[p]), uv_r[1])
                 + _mm_nt(_Lhs(r_t[:, ps[p]]), s0h, s0l))
            y_ref[rows, ps[p]] = y
            lt = _Lhs(jnp.concatenate([u_mat, v_p], axis=0).T)
            bkh, bkl = _split(jnp.concatenate([b_h[:, ps[p]], k_h[:, ps[p]]], axis=0))
            ds = (jnp.dot(lt.cat, jnp.concatenate([bkh, bkh], axis=0), preferred_element_type=F32)
                  + jnp.dot(lt.hi, bkl, preferred_element_type=F32))
            s_ref[p] = s0 * g_c[:, ps[p]] + jnp.where(bd_mask, ds, 0.0)
        return carry

    lax.fori_loop(0, tt // c, chunk, 0)

    y = y_ref[...]
    mean = _seg_sum(y, seg_ones) * (1.0 / n)
    dlt = y - mean
    var = _seg_sum(dlt * dlt, seg_ones) * (1.0 / n)
    yn = dlt * lax.rsqrt(var + RWKV_LN_EPS) * lng_ref[...] + lnb_ref[...]
    bonus = _seg_sum(st_ref[0] * st_ref[2] * rk_ref[...], seg_ones) * st_ref[3]
    gate = _dot(_sigmoid(z_ref[:, o_gl:]), g2_ref[...])
    o_ref[...] = ((yn + bonus) * gate).astype(o_ref.dtype)


def _rwkv(zr, w0, w2, a0, a2, g2, k_k, k_a, r_k, ln_g, ln_b, tt):
    b, t, nz = zr.shape
    w = w0.shape[-1]
    const = lambda *shape: pl.BlockSpec(shape, lambda i, j: (0,) * len(shape))
    return pl.pallas_call(
        _rwkv_kernel,
        grid=(b, t // tt),
        in_specs=[pl.BlockSpec((None, tt, nz), lambda i, j: (i, j, 0)),
                  const(1, w), const(*w2.shape), const(1, w), const(*a2.shape), const(*g2.shape),
                  const(1, w), const(1, w), const(1, w), const(1, w), const(1, w)],
        out_specs=pl.BlockSpec((None, tt, w), lambda i, j: (i, j, 0)),
        out_shape=jax.ShapeDtypeStruct((b, t, w), BF16),
        scratch_shapes=[pltpu.VMEM((w // LANES, LANES, LANES), F32),
                        pltpu.VMEM((6, tt, w), F32),
                        pltpu.VMEM((tt, w), F32)],
        compiler_params=_params(2),
        name="rwkv",
    )(zr, w0, w2, a0, a2, g2, k_k, k_a, r_k, ln_g, ln_b)


def _post_attn_kernel(x_ref, og_ref, or_ref, wog_ref, wor_ref, gx_ref, wq_ref, k_ref, v_ref, wo_ref, out_ref):
    d = x_ref.shape[-1]
    hd = d // MEM_HEADS
    x1 = (x_ref[...] + jnp.dot(og_ref[...], wog_ref[...], preferred_element_type=F32)
          + jnp.dot(or_ref[...], wor_ref[...], preferred_element_type=F32))
    q = _dot(_rms_norm(x1, gx_ref[...], NORM_EPS), wq_ref[...]).astype(BF16)
    heads = []
    for h in range(MEM_HEADS):
        hs = slice(h * hd, (h + 1) * hd)
        s = lax.dot_general(q[:, hs], k_ref[:, hs], (((1,), (1,)), ((), ())),
                            preferred_element_type=F32) * (hd ** -0.5)
        e = jnp.exp(s - jnp.max(s, axis=-1, keepdims=True))
        p = e / jnp.sum(e, axis=-1, keepdims=True)
        heads.append(_dot(p, v_ref[:, hs]))
    o = jnp.concatenate(heads, axis=-1)
    out_ref[...] = x1 + _dot(o, wo_ref[...])


def _post_attn(x, o_gla, o_rwkv, wo_g, wo_r, gx, wq, kmem, vmem, wo, tm):
    b, t, d = x.shape
    m = kmem.shape[1]
    const = lambda *shape: pl.BlockSpec(shape, lambda i, j: (0,) * len(shape))
    tile = lambda width: pl.BlockSpec((None, tm, width), lambda i, j: (i, j, 0))
    return pl.pallas_call(
        _post_attn_kernel,
        grid=(b, t // tm),
        in_specs=[tile(d), tile(o_gla.shape[-1]), tile(o_rwkv.shape[-1]),
                  const(*wo_g.shape), const(*wo_r.shape), const(1, d), const(*wq.shape),
                  pl.BlockSpec((None, m, d), lambda i, j: (i, 0, 0)),
                  pl.BlockSpec((None, m, d), lambda i, j: (i, 0, 0)),
                  const(*wo.shape)],
        out_specs=tile(d),
        out_shape=jax.ShapeDtypeStruct((b, t, d), F32),
        compiler_params=_params(2),
        name="post_attn",
    )(x, o_gla, o_rwkv, wo_g, wo_r, gx, wq, kmem, vmem, wo)


def _ffn_kernel(x_ref, g_ref, wgu_ref, wd_ref, gf_ref, out_ref, *, ff_chunk):
    x = x_ref[...]
    d_ff = wd_ref.shape[0]
    h = _rms_norm(x, g_ref[...], NORM_EPS).astype(BF16)
    acc = x
    for c0 in range(0, d_ff, ff_chunk):
        gate = jnp.dot(h, wgu_ref[:, c0:c0 + ff_chunk], preferred_element_type=F32)
        up = jnp.dot(h, wgu_ref[:, d_ff + c0:d_ff + c0 + ff_chunk], preferred_element_type=F32)
        act = (gate * _sigmoid(gate) * up).astype(BF16)
        acc = acc + jnp.dot(act, wd_ref[c0:c0 + ff_chunk, :], preferred_element_type=F32)
    out_ref[...] = _rms_norm(acc, gf_ref[...], NORM_EPS)


def _ffn(x, g, wgu, wd, gf, tm, ff_chunk):
    b, t, d = x.shape
    const = lambda *shape: pl.BlockSpec(shape, lambda i, j: (0,) * len(shape))
    tile = pl.BlockSpec((None, tm, d), lambda i, j: (i, j, 0))
    return pl.pallas_call(
        functools.partial(_ffn_kernel, ff_chunk=ff_chunk),
        grid=(b, t // tm),
        in_specs=[tile, const(1, d), const(*wgu.shape), const(*wd.shape), const(1, d)],
        out_specs=tile,
        out_shape=jax.ShapeDtypeStruct((b, t, d), F32),
        compiler_params=_params(2),
        name="ffn",
    )(x, g, wgu, wd, gf)


def _layer(x, mem, norm_mix_g, w_in, gla_wa2, gla_ba, gla_norm_g, rwkv_mu, rwkv_w0, rwkv_w2, rwkv_a0,
           rwkv_a2, rwkv_g2, rwkv_k_k, rwkv_k_a, rwkv_r_k, rwkv_ln_g, rwkv_ln_b, w_out, norm_mem_x_g,
           norm_mem_g, wq_mem, wkv_mem, wo_mem, norm_ffn_g, w_gate_up, w_down, final_g):
    b, t, d = x.shape
    m = mem.shape[1]
    row = lambda p: p.reshape(1, -1).astype(F32)
    kw = GLA_HEADS * GLA_DK
    vw = GLA_HEADS * GLA_DV
    n_gla_main = 2 * kw + 2 * vw
    n_gla = n_gla_main + GLA_GATE_RANK
    pad = LANES - GLA_GATE_RANK
    w_gla = jnp.pad(w_in[:, :n_gla], ((0, 0), (0, pad))).astype(BF16)
    w_rwkv = w_in[:, n_gla:].astype(BF16)
    wa2p = jnp.pad(gla_wa2, ((0, pad), (0, 0))).astype(BF16)

    kmem, vmem = _mem_kv(mem.reshape(b * m, d), row(norm_mem_g), wkv_mem.astype(BF16))
    zg, zr = _in_proj(x, row(norm_mix_g), w_gla, w_rwkv, wa2p, row(gla_ba), row(rwkv_mu), tm=256)
    o_gla = _gla(zg, row(gla_norm_g), tt=512)
    o_rwkv = _rwkv(zr, row(rwkv_w0), rwkv_w2.astype(BF16), row(rwkv_a0), rwkv_a2.astype(BF16),
                   rwkv_g2.astype(BF16), row(rwkv_k_k), row(rwkv_k_a), row(rwkv_r_k), row(rwkv_ln_g),
                   row(rwkv_ln_b), tt=512)
    w_out_b = w_out.astype(BF16)
    x2 = _post_attn(x, o_gla, o_rwkv, w_out_b[:vw], w_out_b[vw:], row(norm_mem_x_g), wq_mem.astype(BF16),
                    kmem.reshape(b, m, d), vmem.reshape(b, m, d), wo_mem.astype(BF16), tm=512)
    return _ffn(x2, row(norm_ffn_g), w_gate_up.astype(BF16), w_down.astype(BF16), row(final_g),
                tm=512, ff_chunk=256)


def kernel(x, mem, norm_mix_g, w_in, gla_wa2, gla_ba, gla_norm_g, rwkv_mu, rwkv_w0, rwkv_w2, rwkv_a0, rwkv_a2, rwkv_g2, rwkv_k_k, rwkv_k_a, rwkv_r_k, rwkv_ln_g, rwkv_ln_b, w_out, norm_mem_x_g, norm_mem_g, wq_mem, wkv_mem, wo_mem, norm_ffn_g, w_gate_up, w_down, norm_final_g):
    assert norm_mix_g.shape[0] == 1, "single-layer block"
    return _layer(x, mem, norm_mix_g[0], w_in[0], gla_wa2[0], gla_ba[0], gla_norm_g[0], rwkv_mu[0],
                  rwkv_w0[0], rwkv_w2[0], rwkv_a0[0], rwkv_a2[0], rwkv_g2[0], rwkv_k_k[0], rwkv_k_a[0],
                  rwkv_r_k[0], rwkv_ln_g[0], rwkv_ln_b[0], w_out[0], norm_mem_x_g[0], norm_mem_g[0],
                  wq_mem[0], wkv_mem[0], wo_mem[0], norm_ffn_g[0], w_gate_up[0], w_down[0], norm_final_g)
```

```python
import functools
import math

import jax
import jax.numpy as jnp
from jax import lax
from jax.experimental import pallas as pl
from jax.experimental.pallas import tpu as pltpu

F32 = jnp.float32
BF16 = jnp.bfloat16
HIGHEST = lax.Precision.HIGHEST

MEM_HEADS = 4
GLA_HEADS = 4
GLA_DK = 64
GLA_DV = 128
GLA_GATE_RANK = 16
GLA_GATE_NORMALIZER = 16.0
GLA_NORM_EPS = 1e-5
RWKV_HEAD = 64
RWKV_DECAY_RANK = 64
RWKV_AAA_RANK = 64
RWKV_GATE_RANK = 128
RWKV_LN_EPS = 64e-5
NORM_EPS = 1e-6

CHUNK = 64
LANES = 128
VMEM_LIMIT = 56 * 1024 * 1024


def _params(n_grid_dims):
    return pltpu.CompilerParams(dimension_semantics=("arbitrary",) * n_grid_dims,
                                vmem_limit_bytes=VMEM_LIMIT)


def _dot(a, b):
    return jnp.dot(a.astype(BF16), b.astype(BF16), preferred_element_type=F32)


def _dot_nt(a, b):
    return lax.dot_general(a.astype(BF16), b.astype(BF16), (((1,), (1,)), ((), ())),
                           preferred_element_type=F32)


def _dotf(a, b):
    return jnp.dot(a, b, precision=HIGHEST, preferred_element_type=F32)


def _rms_norm(x, g, eps):
    return x * lax.rsqrt(jnp.mean(x * x, axis=-1, keepdims=True) + eps) * g


def _sigmoid(x):
    return 1.0 / (1.0 + jnp.exp(-x))


def _log_sigmoid(x):
    return jnp.minimum(x, 0.0) - jnp.log1p(jnp.exp(-jnp.abs(x)))


def _iota2(shape, dim):
    return lax.broadcasted_iota(jnp.int32, shape, dim)


def _tri_incl(n):
    return (_iota2((n, n), 0) >= _iota2((n, n), 1)).astype(F32)


def _mem_kv_kernel(mem_ref, g_ref, w_ref, k_ref, v_ref):
    d = k_ref.shape[-1]
    m = _rms_norm(mem_ref[...], g_ref[...], NORM_EPS)
    kv = _dot(m, w_ref[...])
    k_ref[...] = kv[:, :d].astype(BF16)
    v_ref[...] = kv[:, d:].astype(BF16)


def _mem_kv(mem2d, g, wkv):
    n, d = mem2d.shape
    return pl.pallas_call(
        _mem_kv_kernel,
        out_shape=(jax.ShapeDtypeStruct((n, d), BF16), jax.ShapeDtypeStruct((n, d), BF16)),
        compiler_params=pltpu.CompilerParams(vmem_limit_bytes=VMEM_LIMIT),
        name="mem_kv",
    )(mem2d, g, wkv)


def _in_proj_kernel(x_ref, g_ref, wg_ref, wr_ref, wa2_ref, ba_ref, mu_ref, zg_ref, zr_ref, prev_ref):
    tm = x_ref.shape[0]
    n_main = zg_ref.shape[-1] - wa2_ref.shape[-1]
    h = _rms_norm(x_ref[...], g_ref[...], NORM_EPS).astype(BF16)

    zg = jnp.dot(h, wg_ref[...], preferred_element_type=F32)
    zg_ref[:, :n_main] = zg[:, :n_main]
    pre = _dot(zg[:, n_main:], wa2_ref[...]) + ba_ref[...]
    zg_ref[:, n_main:] = _log_sigmoid(pre) * (1.0 / GLA_GATE_NORMALIZER)

    zr = jnp.dot(h, wr_ref[...], preferred_element_type=F32)

    @pl.when(pl.program_id(1) == 0)
    def _():
        prev_ref[...] = jnp.zeros_like(prev_ref)

    rolled = pltpu.roll(zr, shift=1, axis=0)
    first = _iota2(zr.shape, 0) == 0
    shifted = jnp.where(first, jnp.broadcast_to(prev_ref[0:1, :], zr.shape), rolled)
    zr_ref[...] = zr + (shifted - zr) * mu_ref[...]
    prev_ref[0:1, :] = zr[tm - 1:tm, :]


def _in_proj(x, g, w_gla, w_rwkv, wa2p, ba, mu, tm):
    b, t, d = x.shape
    ng = w_gla.shape[1] - LANES + wa2p.shape[1]
    nr = w_rwkv.shape[1]
    const = lambda *shape: pl.BlockSpec(shape, lambda i, j: (0,) * len(shape))
    return pl.pallas_call(
        _in_proj_kernel,
        grid=(b, t // tm),
        in_specs=[
            pl.BlockSpec((None, tm, d), lambda i, j: (i, j, 0)),
            const(1, d), const(*w_gla.shape), const(*w_rwkv.shape), const(*wa2p.shape),
            const(1, wa2p.shape[1]), const(1, nr),
        ],
        out_specs=(pl.BlockSpec((None, tm, ng), lambda i, j: (i, j, 0)),
                   pl.BlockSpec((None, tm, nr), lambda i, j: (i, j, 0))),
        out_shape=(jax.ShapeDtypeStruct((b, t, ng), F32), jax.ShapeDtypeStruct((b, t, nr), F32)),
        scratch_shapes=[pltpu.VMEM((8, nr), F32)],
        compiler_params=_params(2),
        name="in_proj",
    )(x, g, w_gla, w_rwkv, wa2p, ba, mu)


def _gla_kernel(z_ref, ng_ref, o_ref, s_ref):
    tt = z_ref.shape[0]
    kw = GLA_HEADS * GLA_DK
    vw = GLA_HEADS * GLA_DV
    c = CHUNK

    @pl.when(pl.program_id(1) == 0)
    def _():
        s_ref[...] = jnp.zeros_like(s_ref)

    tri = _tri_incl(c)
    causal = _iota2((c, c), 0) >= _iota2((c, c), 1)
    norm_g = ng_ref[...]

    def chunk(ci, carry):
        r0 = pl.multiple_of(ci * c, c)
        rows = pl.ds(r0, c)
        q = z_ref[rows, 0:kw] * (GLA_DK ** -0.5)
        k = z_ref[rows, kw:2 * kw]
        v = z_ref[rows, 2 * kw:2 * kw + vw]
        gate = z_ref[rows, 2 * kw + vw:2 * kw + 2 * vw]
        log_a = z_ref[rows, 2 * kw + 2 * vw:3 * kw + 2 * vw]
        bcum = _dotf(tri, log_a)
        blast = bcum[c - 1:c, :]
        q_dec = q * jnp.exp(bcum)
        k_inv = k * jnp.exp(-bcum)
        k_tail = k * jnp.exp(blast - bcum)
        dec = jnp.exp(blast)
        outs = []
        for h in range(GLA_HEADS):
            ks = slice(h * GLA_DK, (h + 1) * GLA_DK)
            vs = slice(h * GLA_DV, (h + 1) * GLA_DV)
            s_prev = s_ref[h]
            scores = jnp.where(causal, _dot_nt(q_dec[:, ks], k_inv[:, ks]), 0.0)
            o = _dot(scores, v[:, vs]) + _dot_nt(q_dec[:, ks], s_prev)
            d_state = lax.dot_general(v[:, vs].astype(BF16), k_tail[:, ks].astype(BF16),
                                      (((0,), (0,)), ((), ())), preferred_element_type=F32)
            s_ref[h] = s_prev * dec[:, ks] + d_state
            o = o * lax.rsqrt(jnp.mean(o * o, axis=-1, keepdims=True) + GLA_NORM_EPS) * norm_g
            gh = gate[:, vs]
            outs.append(o * (gh * _sigmoid(gh)))
        o_ref[rows, :] = jnp.concatenate(outs, axis=-1).astype(o_ref.dtype)
        return carry

    lax.fori_loop(0, tt // c, chunk, 0)


def _gla(zg, norm_g, tt):
    b, t, n = zg.shape
    vw = GLA_HEADS * GLA_DV
    return pl.pallas_call(
        _gla_kernel,
        grid=(b, t // tt),
        in_specs=[pl.BlockSpec((None, tt, n), lambda i, j: (i, j, 0)),
                  pl.BlockSpec((1, GLA_DV), lambda i, j: (0, 0))],
        out_specs=pl.BlockSpec((None, tt, vw), lambda i, j: (i, j, 0)),
        out_shape=jax.ShapeDtypeStruct((b, t, vw), BF16),
        scratch_shapes=[pltpu.VMEM((GLA_HEADS, GLA_DV, GLA_DK), F32)],
        compiler_params=_params(2),
        name="gla",
    )(zg, norm_g)


def _split(x):
    hi = x.astype(BF16)
    lo = (x - hi.astype(F32)).astype(BF16)
    return hi, lo


def _seg_sum(x, seg_ones):
    hi, lo = _split(x)
    blk = seg_ones.shape[0]
    parts = []
    for s in range(x.shape[1] // blk):
        sl = slice(s * blk, (s + 1) * blk)
        parts.append(jnp.dot(hi[:, sl], seg_ones, preferred_element_type=F32)
                     + jnp.dot(lo[:, sl], seg_ones, preferred_element_type=F32))
    return jnp.concatenate(parts, axis=-1)


class _Lhs:
    def __init__(self, x):
        self.hi, lo = _split(x)
        self.cat = jnp.concatenate([self.hi, lo], axis=1)


def _block_diag(z, lane_lo):
    zero = jnp.zeros_like(z)
    return jnp.concatenate([jnp.where(lane_lo, z, zero), jnp.where(lane_lo, zero, z)], axis=0)


class _Rhs:
    def __init__(self, zs, lane_lo):
        his, los = [], []
        for z in zs:
            hi, lo = _split(z)
            his.append(_block_diag(hi, lane_lo))
            los.append(_block_diag(lo, lane_lo))
        hi = his[0] if len(his) == 1 else jnp.concatenate(his, axis=1)
        self.lo = los[0] if len(los) == 1 else jnp.concatenate(los, axis=1)
        self.hh = jnp.concatenate([hi, hi], axis=0)


def _mm(l, r):
    return (jnp.dot(l.cat, r.hh, preferred_element_type=F32)
            + jnp.dot(l.hi, r.lo, preferred_element_type=F32))


def _mm_nt(l, r_hi, r_lo):
    dims = (((1,), (1,)), ((), ()))
    return (lax.dot_general(l.cat, jnp.concatenate([r_hi, r_hi], axis=1), dims, preferred_element_type=F32)
            + lax.dot_general(l.hi, r_lo, dims, preferred_element_type=F32))


def _rwkv_kernel(z_ref, w0_ref, w2_ref, a0_ref, a2_ref, g2_ref, kk_ref, ka_ref, rk_ref, lng_ref, lnb_ref,
                 o_ref, s_ref, st_ref, y_ref):
    nb, tt, _ = z_ref.shape
    w = w0_ref.shape[-1]
    n = RWKV_HEAD
    pairs = w // LANES
    c = CHUNK
    o_wl = 3 * w
    o_al = o_wl + RWKV_DECAY_RANK
    o_gl = o_al + RWKV_AAA_RANK

    @pl.when(pl.program_id(0) == 0)
    def _():
        s_ref[...] = jnp.zeros_like(s_ref)

    seg = 2 * LANES
    seg_ones = (_iota2((seg, seg), 0) // n == _iota2((seg, seg), 1) // n).astype(BF16)

    cols = lambda lo, hi: z_ref[:, :, lo:hi].reshape(nb * tt, hi - lo)
    r = cols(0, w)
    k = cols(w, 2 * w)
    v = cols(2 * w, 3 * w)
    u = w0_ref[...] + _dot(jnp.tanh(cols(o_wl, o_al)), w2_ref[...])
    lw = -math.exp(-0.5) * _sigmoid(u)
    a = _sigmoid(a0_ref[...] + _dot(cols(o_al, o_gl), a2_ref[...]))
    kk = k * kk_ref[...]
    kk = kk / jnp.maximum(jnp.sqrt(_seg_sum(kk * kk, seg_ones)), 1e-12)
    k2 = k * (1.0 + (a - 1.0) * ka_ref[...])
    st_ref[0] = r
    st_ref[1] = lw
    st_ref[2] = k2
    st_ref[3] = v
    st_ref[4] = -kk
    st_ref[5] = kk * a

    tri3 = (_iota2((c, 3 * c), 0) >= _iota2((c, 3 * c), 1) % c).astype(BF16)
    row = _iota2((c, LANES), 0)
    col = _iota2((c, LANES), 1) % c
    lower = row >= col
    strict = row > col
    eye = (row == col).astype(F32)
    blk16 = (row // 16) == (col // 16)
    blk32 = (row // 32) == (col // 32)
    off16 = jnp.logical_and(blk32, jnp.logical_not(blk16))
    lane_lo = _iota2((c, LANES), 1) < n
    bd_mask = (_iota2((LANES, LANES), 0) // n) == (_iota2((LANES, LANES), 1) // n)
    units = [(b, p) for b in range(nb) for p in range(pairs)]
    prange = range(len(units))

    def chunk(ci, carry):
        rows = [pl.ds(pl.multiple_of(b * tt + ci * c, c), c) for b in range(nb)]
        a_t, r_t, b_t, k_t, b_h, k_h, vc, g_c = [], [], [], [], [], [], [], []
        for b in range(nb):
            rc = st_ref[0, rows[b], :]
            lwc = st_ref[1, rows[b], :]
            kc = st_ref[2, rows[b], :]
            ac = st_ref[4, rows[b], :]
            bc = st_ref[5, rows[b], :]
            l1 = lwc.astype(BF16)
            rem = lwc - l1.astype(F32)
            l2 = rem.astype(BF16)
            l3 = (rem - l2.astype(F32)).astype(BF16)
            cum = jnp.dot(tri3, jnp.concatenate([l1, l2, l3], axis=0), preferred_element_type=F32)
            last = cum[c - 1:c, :]
            e_neg = jnp.exp(-cum)
            e_tail = jnp.exp(last - cum)
            r_tb = rc * jnp.exp(cum)
            a_tb = ac * jnp.exp(cum - lwc)
            vb = st_ref[3, rows[b], :]
            for p in range(pairs):
                sl = slice(p * LANES, (p + 1) * LANES)
                a_t.append(a_tb[:, sl])
                r_t.append(r_tb[:, sl])
                b_t.append(bc[:, sl] * e_neg[:, sl])
                k_t.append(kc[:, sl] * e_neg[:, sl])
                b_h.append(bc[:, sl] * e_tail[:, sl])
                k_h.append(kc[:, sl] * e_tail[:, sl])
                vc.append(vb[:, sl])
                g_c.append(jnp.exp(last[:, sl]))

        a_ab, a_ak, q_b, q_k = [], [], [], []
        for p in prange:
            lhs = _Lhs(jnp.concatenate([a_t[p], r_t[p]], axis=0))
            bth, btl = _split(b_t[p])
            kth, ktl = _split(k_t[p])
            r_hi = jnp.concatenate([_block_diag(bth, lane_lo), _block_diag(kth, lane_lo)], axis=0)
            r_lo = jnp.concatenate([_block_diag(btl, lane_lo), _block_diag(ktl, lane_lo)], axis=0)
            aa = _mm_nt(lhs, r_hi, r_lo)
            a_ab.append(jnp.where(strict, aa[:c, :LANES], 0.0))
            a_ak.append(jnp.where(strict, aa[:c, LANES:], 0.0))
            q_b.append(jnp.where(lower, aa[c:, :LANES], 0.0))
            q_k.append(jnp.where(lower, aa[c:, LANES:], 0.0))

        ad = [jnp.where(blk16, x, 0.0) for x in a_ab]
        ad_l = [_Lhs(x) for x in ad]
        ad_r = [_Rhs([x], lane_lo) for x in ad]
        a2 = [_mm(ad_l[p], ad_r[p]) for p in prange]
        a2_l = [_Lhs(x) for x in a2]
        a2_r = [_Rhs([x], lane_lo) for x in a2]
        a4 = [_mm(a2_l[p], a2_r[p]) for p in prange]
        a4_l = [_Lhs(x) for x in a4]
        a4_r = [_Rhs([x], lane_lo) for x in a4]
        a8 = [_mm(a4_l[p], a4_r[p]) for p in prange]
        a8_r = [_Rhs([x], lane_lo) for x in a8]
        t = [eye + ad[p] for p in prange]
        t = [t[p] + _mm(_Lhs(t[p]), a2_r[p]) for p in prange]
        t = [t[p] + _mm(_Lhs(t[p]), a4_r[p]) for p in prange]
        t = [t[p] + _mm(_Lhs(t[p]), a8_r[p]) for p in prange]
        for msk in (off16, jnp.logical_not(blk32)):
            t_l = [_Lhs(t[p]) for p in prange]
            te = [_mm(t_l[p], _Rhs([jnp.where(msk, a_ab[p], 0.0)], lane_lo)) for p in prange]
            t = [t[p] + _mm(_Lhs(te[p]), _Rhs([t[p]], lane_lo)) for p in prange]

        pv = [_mm(_Lhs(a_ak[p]), _Rhs([vc[p]], lane_lo)) for p in prange]
        wu = [_mm(_Lhs(t[p]), _Rhs([a_t[p], pv[p]], lane_lo)) for p in prange]

        for p in prange:
            bi, pi = units[p]
            s0 = s_ref[p]
            s0h, s0l = _split(s0)
            u_mat = _mm_nt(_Lhs(wu[p][:, :LANES]), s0h, s0l) + wu[p][:, LANES:]
            uv_r = _Rhs([u_mat], lane_lo), _Rhs([vc[p]], lane_lo)
            y = (_mm(_Lhs(q_b[p]), uv_r[0]) + _mm(_Lhs(q_k[p]), uv_r[1])
                 + _mm_nt(_Lhs(r_t[p]), s0h, s0l))
            y_ref[rows[bi], pi * LANES:(pi + 1) * LANES] = y
            lt = _Lhs(jnp.concatenate([u_mat, vc[p]], axis=0).T)
            bkh, bkl = _split(jnp.concatenate([b_h[p], k_h[p]], axis=0))
            ds = (jnp.dot(lt.cat, jnp.concatenate([bkh, bkh], axis=0), preferred_element_type=F32)
                  + jnp.dot(lt.hi, bkl, preferred_element_type=F32))
            s_ref[p] = s0 * g_c[p] + jnp.where(bd_mask, ds, 0.0)
        return carry

    lax.fori_loop(0, tt // c, chunk, 0)

    y = y_ref[...]
    mean = _seg_sum(y, seg_ones) * (1.0 / n)
    dlt = y - mean
    var = _seg_sum(dlt * dlt, seg_ones) * (1.0 / n)
    yn = dlt * lax.rsqrt(var + RWKV_LN_EPS) * lng_ref[...] + lnb_ref[...]
    bonus = _seg_sum(st_ref[0] * st_ref[2] * rk_ref[...], seg_ones) * st_ref[3]
    gate = _dot(_sigmoid(cols(o_gl, o_gl + RWKV_GATE_RANK)), g2_ref[...])
    o_ref[...] = ((yn + bonus) * gate).reshape(nb, tt, w).astype(o_ref.dtype)


def _rwkv(zr, w0, w2, a0, a2, g2, k_k, k_a, r_k, ln_g, ln_b, tt):
    b, t, nz = zr.shape
    w = w0.shape[-1]
    const = lambda *shape: pl.BlockSpec(shape, lambda j: (0,) * len(shape))
    return pl.pallas_call(
        _rwkv_kernel,
        grid=(t // tt,),
        in_specs=[pl.BlockSpec((b, tt, nz), lambda j: (0, j, 0)),
                  const(1, w), const(*w2.shape), const(1, w), const(*a2.shape), const(*g2.shape),
                  const(1, w), const(1, w), const(1, w), const(1, w), const(1, w)],
        out_specs=pl.BlockSpec((b, tt, w), lambda j: (0, j, 0)),
        out_shape=jax.ShapeDtypeStruct((b, t, w), BF16),
        scratch_shapes=[pltpu.VMEM((b * w // LANES, LANES, LANES), F32),
                        pltpu.VMEM((6, b * tt, w), F32),
                        pltpu.VMEM((b * tt, w), F32)],
        compiler_params=_params(1),
        name="rwkv",
    )(zr, w0, w2, a0, a2, g2, k_k, k_a, r_k, ln_g, ln_b)


def _post_attn_kernel(x_ref, og_ref, or_ref, wog_ref, wor_ref, gx_ref, wq_ref, k_ref, v_ref, wo_ref, out_ref):
    d = x_ref.shape[-1]
    hd = d // MEM_HEADS
    x1 = (x_ref[...] + jnp.dot(og_ref[...], wog_ref[...], preferred_element_type=F32)
          + jnp.dot(or_ref[...], wor_ref[...], preferred_element_type=F32))
    q = _dot(_rms_norm(x1, gx_ref[...], NORM_EPS), wq_ref[...]).astype(BF16)
    heads = []
    for h in range(MEM_HEADS):
        hs = slice(h * hd, (h + 1) * hd)
        s = lax.dot_general(q[:, hs], k_ref[:, hs], (((1,), (1,)), ((), ())),
                            preferred_element_type=F32) * (hd ** -0.5)
        e = jnp.exp(s - jnp.max(s, axis=-1, keepdims=True))
        p = e / jnp.sum(e, axis=-1, keepdims=True)
        heads.append(_dot(p, v_ref[:, hs]))
    o = jnp.concatenate(heads, axis=-1)
    out_ref[...] = x1 + _dot(o, wo_ref[...])


def _post_attn(x, o_gla, o_rwkv, wo_g, wo_r, gx, wq, kmem, vmem, wo, tm):
    b, t, d = x.shape
    m = kmem.shape[1]
    const = lambda *shape: pl.BlockSpec(shape, lambda i, j: (0,) * len(shape))
    tile = lambda width: pl.BlockSpec((None, tm, width), lambda i, j: (i, j, 0))
    return pl.pallas_call(
        _post_attn_kernel,
        grid=(b, t // tm),
        in_specs=[tile(d), tile(o_gla.shape[-1]), tile(o_rwkv.shape[-1]),
                  const(*wo_g.shape), const(*wo_r.shape), const(1, d), const(*wq.shape),
                  pl.BlockSpec((None, m, d), lambda i, j: (i, 0, 0)),
                  pl.BlockSpec((None, m, d), lambda i, j: (i, 0, 0)),
                  const(*wo.shape)],
        out_specs=tile(d),
        out_shape=jax.ShapeDtypeStruct((b, t, d), F32),
        compiler_params=_params(2),
        name="post_attn",
    )(x, o_gla, o_rwkv, wo_g, wo_r, gx, wq, kmem, vmem, wo)


def _ffn_kernel(x_ref, g_ref, wgu_ref, wd_ref, gf_ref, out_ref, *, ff_chunk):
    x = x_ref[...]
    d_ff = wd_ref.shape[0]
    h = _rms_norm(x, g_ref[...], NORM_EPS).astype(BF16)
    acc = x
    for c0 in range(0, d_ff, ff_chunk):
        gate = jnp.dot(h, wgu_ref[:, c0:c0 + ff_chunk], preferred_element_type=F32)
        up = jnp.dot(h, wgu_ref[:, d_ff + c0:d_ff + c0 + ff_chunk], preferred_element_type=F32)
        act = (gate * _sigmoid(gate) * up).astype(BF16)
        acc = acc + jnp.dot(act, wd_ref[c0:c0 + ff_chunk, :], preferred_element_type=F32)
    out_ref[...] = _rms_norm(acc, gf_ref[...], NORM_EPS)


def _ffn(x, g, wgu, wd, gf, tm, ff_chunk):
    b, t, d = x.shape
    const = lambda *shape: pl.BlockSpec(shape, lambda i, j: (0,) * len(shape))
    tile = pl.BlockSpec((None, tm, d), lambda i, j: (i, j, 0))
    return pl.pallas_call(
        functools.partial(_ffn_kernel, ff_chunk=ff_chunk),
        grid=(b, t // tm),
        in_specs=[tile, const(1, d), const(*wgu.shape), const(*wd.shape), const(1, d)],
        out_specs=tile,
        out_shape=jax.ShapeDtypeStruct((b, t, d), F32),
        compiler_params=_params(2),
        name="ffn",
    )(x, g, wgu, wd, gf)


def _layer(x, mem, norm_mix_g, w_in, gla_wa2, gla_ba, gla_norm_g, rwkv_mu, rwkv_w0, rwkv_w2, rwkv_a0,
           rwkv_a2, rwkv_g2, rwkv_k_k, rwkv_k_a, rwkv_r_k, rwkv_ln_g, rwkv_ln_b, w_out, norm_mem_x_g,
           norm_mem_g, wq_mem, wkv_mem, wo_mem, norm_ffn_g, w_gate_up, w_down, final_g):
    b, t, d = x.shape
    m = mem.shape[1]
    row = lambda p: p.reshape(1, -1).astype(F32)
    kw = GLA_HEADS * GLA_DK
    vw = GLA_HEADS * GLA_DV
    n_gla_main = 2 * kw + 2 * vw
    n_gla = n_gla_main + GLA_GATE_RANK
    pad = LANES - GLA_GATE_RANK
    w_gla = jnp.pad(w_in[:, :n_gla], ((0, 0), (0, pad))).astype(BF16)
    w_rwkv = w_in[:, n_gla:].astype(BF16)
    wa2p = jnp.pad(gla_wa2, ((0, pad), (0, 0))).astype(BF16)

    kmem, vmem = _mem_kv(mem.reshape(b * m, d), row(norm_mem_g), wkv_mem.astype(BF16))
    zg, zr = _in_proj(x, row(norm_mix_g), w_gla, w_rwkv, wa2p, row(gla_ba), row(rwkv_mu), tm=256)
    o_gla = _gla(zg, row(gla_norm_g), tt=512)
    o_rwkv = _rwkv(zr, row(rwkv_w0), rwkv_w2.astype(BF16), row(rwkv_a0), rwkv_a2.astype(BF16),
                   rwkv_g2.astype(BF16), row(rwkv_k_k), row(rwkv_k_a), row(rwkv_r_k), row(rwkv_ln_g),
                   row(rwkv_ln_b), tt=256)
    w_out_b = w_out.astype(BF16)
    x2 = _post_attn(x, o_gla, o_rwkv, w_out_b[:vw], w_out_b[vw:], row(norm_mem_x_g), wq_mem.astype(BF16),
                    kmem.reshape(b, m, d), vmem.reshape(b, m, d), wo_mem.astype(BF16), tm=512)
    return _ffn(x2, row(norm_ffn_g), w_gate_up.astype(BF16), w_down.astype(BF16), row(final_g),
                tm=512, ff_chunk=256)


def kernel(x, mem, norm_mix_g, w_in, gla_wa2, gla_ba, gla_norm_g, rwkv_mu, rwkv_w0, rwkv_w2, rwkv_a0, rwkv_a2, rwkv_g2, rwkv_k_k, rwkv_k_a, rwkv_r_k, rwkv_ln_g, rwkv_ln_b, w_out, norm_mem_x_g, norm_mem_g, wq_mem, wkv_mem, wo_mem, norm_ffn_g, w_gate_up, w_down, norm_final_g):
    assert norm_mix_g.shape[0] == 1, "single-layer block"
    return _layer(x, mem, norm_mix_g[0], w_in[0], gla_wa2[0], gla_ba[0], gla_norm_g[0], rwkv_mu[0],
                  rwkv_w0[0], rwkv_w2[0], rwkv_a0[0], rwkv_a2[0], rwkv_g2[0], rwkv_k_k[0], rwkv_k_a[0],
                  rwkv_r_k[0], rwkv_ln_g[0], rwkv_ln_b[0], w_out[0], norm_mem_x_g[0], norm_mem_g[0],
                  wq_mem[0], wkv_mem[0], wo_mem[0], norm_ffn_g[0], w_gate_up[0], w_down[0], norm_final_g)
```

```python
import functools
import math

import jax
import jax.numpy as jnp
from jax import lax
from jax.experimental import pallas as pl
from jax.experimental.pallas import tpu as pltpu

F32 = jnp.float32
BF16 = jnp.bfloat16
HIGHEST = lax.Precision.HIGHEST

MEM_HEADS = 4
GLA_HEADS = 4
GLA_DK = 64
GLA_DV = 128
GLA_GATE_RANK = 16
GLA_GATE_NORMALIZER = 16.0
GLA_NORM_EPS = 1e-5
RWKV_HEAD = 64
RWKV_DECAY_RANK = 64
RWKV_AAA_RANK = 64
RWKV_GATE_RANK = 128
RWKV_LN_EPS = 64e-5
NORM_EPS = 1e-6

CHUNK = 64
LANES = 128
VMEM_LIMIT = 56 * 1024 * 1024


def _params(n_grid_dims):
    return pltpu.CompilerParams(dimension_semantics=("arbitrary",) * n_grid_dims,
                                vmem_limit_bytes=VMEM_LIMIT)


def _dot(a, b):
    return jnp.dot(a.astype(BF16), b.astype(BF16), preferred_element_type=F32)


def _dot_nt(a, b):
    return lax.dot_general(a.astype(BF16), b.astype(BF16), (((1,), (1,)), ((), ())),
                           preferred_element_type=F32)


def _dotf(a, b):
    return jnp.dot(a, b, precision=HIGHEST, preferred_element_type=F32)


def _rms_norm(x, g, eps):
    return x * lax.rsqrt(jnp.mean(x * x, axis=-1, keepdims=True) + eps) * g


def _sigmoid(x):
    return 1.0 / (1.0 + jnp.exp(-x))


def _log_sigmoid(x):
    return jnp.minimum(x, 0.0) - jnp.log1p(jnp.exp(-jnp.abs(x)))


def _iota2(shape, dim):
    return lax.broadcasted_iota(jnp.int32, shape, dim)


def _tri_incl(n):
    return (_iota2((n, n), 0) >= _iota2((n, n), 1)).astype(F32)


def _mem_kv_kernel(mem_ref, g_ref, w_ref, k_ref, v_ref):
    d = k_ref.shape[-1]
    m = _rms_norm(mem_ref[...], g_ref[...], NORM_EPS)
    kv = _dot(m, w_ref[...])
    k_ref[...] = kv[:, :d].astype(BF16)
    v_ref[...] = kv[:, d:].astype(BF16)


def _mem_kv(mem2d, g, wkv):
    n, d = mem2d.shape
    return pl.pallas_call(
        _mem_kv_kernel,
        out_shape=(jax.ShapeDtypeStruct((n, d), BF16), jax.ShapeDtypeStruct((n, d), BF16)),
        compiler_params=pltpu.CompilerParams(vmem_limit_bytes=VMEM_LIMIT),
        name="mem_kv",
    )(mem2d, g, wkv)


def _in_proj_kernel(x_ref, g_ref, wg_ref, wr_ref, wa2_ref, ba_ref, mu_ref, zg_ref, zr_ref, prev_ref):
    tm = x_ref.shape[0]
    n_main = zg_ref.shape[-1] - wa2_ref.shape[-1]
    h = _rms_norm(x_ref[...], g_ref[...], NORM_EPS).astype(BF16)

    zg = jnp.dot(h, wg_ref[...], preferred_element_type=F32)
    zg_ref[:, :n_main] = zg[:, :n_main]
    pre = _dot(zg[:, n_main:], wa2_ref[...]) + ba_ref[...]
    zg_ref[:, n_main:] = _log_sigmoid(pre) * (1.0 / GLA_GATE_NORMALIZER)

    zr = jnp.dot(h, wr_ref[...], preferred_element_type=F32)

    @pl.when(pl.program_id(1) == 0)
    def _():
        prev_ref[...] = jnp.zeros_like(prev_ref)

    rolled = pltpu.roll(zr, shift=1, axis=0)
    first = _iota2(zr.shape, 0) == 0
    shifted = jnp.where(first, jnp.broadcast_to(prev_ref[0:1, :], zr.shape), rolled)
    zr_ref[...] = zr + (shifted - zr) * mu_ref[...]
    prev_ref[0:1, :] = zr[tm - 1:tm, :]


def _in_proj(x, g, w_gla, w_rwkv, wa2p, ba, mu, tm):
    b, t, d = x.shape
    ng = w_gla.shape[1] - LANES + wa2p.shape[1]
    nr = w_rwkv.shape[1]
    const = lambda *shape: pl.BlockSpec(shape, lambda i, j: (0,) * len(shape))
    return pl.pallas_call(
        _in_proj_kernel,
        grid=(b, t // tm),
        in_specs=[
            pl.BlockSpec((None, tm, d), lambda i, j: (i, j, 0)),
            const(1, d), const(*w_gla.shape), const(*w_rwkv.shape), const(*wa2p.shape),
            const(1, wa2p.shape[1]), const(1, nr),
        ],
        out_specs=(pl.BlockSpec((None, tm, ng), lambda i, j: (i, j, 0)),
                   pl.BlockSpec((None, tm, nr), lambda i, j: (i, j, 0))),
        out_shape=(jax.ShapeDtypeStruct((b, t, ng), F32), jax.ShapeDtypeStruct((b, t, nr), F32)),
        scratch_shapes=[pltpu.VMEM((8, nr), F32)],
        compiler_params=_params(2),
        name="in_proj",
    )(x, g, w_gla, w_rwkv, wa2p, ba, mu)


def _gla_kernel(z_ref, ng_ref, o_ref, s_ref):
    tt = z_ref.shape[0]
    kw = GLA_HEADS * GLA_DK
    vw = GLA_HEADS * GLA_DV
    c = CHUNK

    @pl.when(pl.program_id(1) == 0)
    def _():
        s_ref[...] = jnp.zeros_like(s_ref)

    tri = _tri_incl(c)
    causal = _iota2((c, c), 0) >= _iota2((c, c), 1)
    norm_g = ng_ref[...]

    def chunk(ci, carry):
        r0 = pl.multiple_of(ci * c, c)
        rows = pl.ds(r0, c)
        q = z_ref[rows, 0:kw] * (GLA_DK ** -0.5)
        k = z_ref[rows, kw:2 * kw]
        v = z_ref[rows, 2 * kw:2 * kw + vw]
        gate = z_ref[rows, 2 * kw + vw:2 * kw + 2 * vw]
        log_a = z_ref[rows, 2 * kw + 2 * vw:3 * kw + 2 * vw]
        bcum = _dotf(tri, log_a)
        blast = bcum[c - 1:c, :]
        q_dec = q * jnp.exp(bcum)
        k_inv = k * jnp.exp(-bcum)
        k_tail = k * jnp.exp(blast - bcum)
        dec = jnp.exp(blast)
        outs = []
        for h in range(GLA_HEADS):
            ks = slice(h * GLA_DK, (h + 1) * GLA_DK)
            vs = slice(h * GLA_DV, (h + 1) * GLA_DV)
            s_prev = s_ref[h]
            scores = jnp.where(causal, _dot_nt(q_dec[:, ks], k_inv[:, ks]), 0.0)
            o = _dot(scores, v[:, vs]) + _dot_nt(q_dec[:, ks], s_prev)
            d_state = lax.dot_general(v[:, vs].astype(BF16), k_tail[:, ks].astype(BF16),
                                      (((0,), (0,)), ((), ())), preferred_element_type=F32)
            s_ref[h] = s_prev * dec[:, ks] + d_state
            o = o * lax.rsqrt(jnp.mean(o * o, axis=-1, keepdims=True) + GLA_NORM_EPS) * norm_g
            gh = gate[:, vs]
            outs.append(o * (gh * _sigmoid(gh)))
        o_ref[rows, :] = jnp.concatenate(outs, axis=-1).astype(o_ref.dtype)
        return carry

    lax.fori_loop(0, tt // c, chunk, 0)


def _gla(zg, norm_g, tt):
    b, t, n = zg.shape
    vw = GLA_HEADS * GLA_DV
    return pl.pallas_call(
        _gla_kernel,
        grid=(b, t // tt),
        in_specs=[pl.BlockSpec((None, tt, n), lambda i, j: (i, j, 0)),
                  pl.BlockSpec((1, GLA_DV), lambda i, j: (0, 0))],
        out_specs=pl.BlockSpec((None, tt, vw), lambda i, j: (i, j, 0)),
        out_shape=jax.ShapeDtypeStruct((b, t, vw), BF16),
        scratch_shapes=[pltpu.VMEM((GLA_HEADS, GLA_DV, GLA_DK), F32)],
        compiler_params=_params(2),
        name="gla",
    )(zg, norm_g)


def _split(x):
    hi = x.astype(BF16)
    lo = (x - hi.astype(F32)).astype(BF16)
    return hi, lo


def _seg_sum(x, seg_ones):
    hi, lo = _split(x)
    blk = seg_ones.shape[0]
    parts = []
    for s in range(x.shape[1] // blk):
        sl = slice(s * blk, (s + 1) * blk)
        parts.append(jnp.dot(hi[:, sl], seg_ones, preferred_element_type=F32)
                     + jnp.dot(lo[:, sl], seg_ones, preferred_element_type=F32))
    return jnp.concatenate(parts, axis=-1)


def _block_diag(z, lane_lo):
    zero = jnp.zeros_like(z)
    return jnp.concatenate([jnp.where(lane_lo, z, zero), jnp.where(lane_lo, zero, z)], axis=0)


def _bd(z, lane_lo):
    return _block_diag(z.astype(BF16), lane_lo)


def _mm1(x, r):
    return jnp.dot(x.astype(BF16), r, preferred_element_type=F32)


def _rwkv_kernel(z_ref, w0_ref, w2_ref, a0_ref, a2_ref, g2_ref, kk_ref, ka_ref, rk_ref, lng_ref, lnb_ref,
                 o_ref, s_ref, st_ref, y_ref):
    nb, tt, _ = z_ref.shape
    w = w0_ref.shape[-1]
    n = RWKV_HEAD
    pairs = w // LANES
    c = CHUNK
    o_wl = 3 * w
    o_al = o_wl + RWKV_DECAY_RANK
    o_gl = o_al + RWKV_AAA_RANK

    @pl.when(pl.program_id(0) == 0)
    def _():
        s_ref[...] = jnp.zeros_like(s_ref)

    seg = 2 * LANES
    seg_ones = (_iota2((seg, seg), 0) // n == _iota2((seg, seg), 1) // n).astype(BF16)

    cols = lambda lo, hi: z_ref[:, :, lo:hi].reshape(nb * tt, hi - lo)
    r = cols(0, w)
    k = cols(w, 2 * w)
    v = cols(2 * w, 3 * w)
    u = w0_ref[...] + _dot(jnp.tanh(cols(o_wl, o_al)), w2_ref[...])
    lw = -math.exp(-0.5) * _sigmoid(u)
    a = _sigmoid(a0_ref[...] + _dot(cols(o_al, o_gl), a2_ref[...]))
    kk = k * kk_ref[...]
    kk = kk / jnp.maximum(jnp.sqrt(_seg_sum(kk * kk, seg_ones)), 1e-12)
    k2 = k * (1.0 + (a - 1.0) * ka_ref[...])
    st_ref[0] = r
    st_ref[1] = lw
    st_ref[2] = k2
    st_ref[3] = v
    st_ref[4] = -kk
    st_ref[5] = kk * a

    tri3 = (_iota2((c, 3 * c), 0) >= _iota2((c, 3 * c), 1) % c).astype(BF16)
    row = _iota2((c, LANES), 0)
    col = _iota2((c, LANES), 1) % c
    lower = row >= col
    strict = row > col
    eye = (row == col).astype(F32)
    blk16 = (row // 16) == (col // 16)
    blk32 = (row // 32) == (col // 32)
    off16 = jnp.logical_and(blk32, jnp.logical_not(blk16))
    lane_lo = _iota2((c, LANES), 1) < n
    bd_mask = (_iota2((LANES, LANES), 0) // n) == (_iota2((LANES, LANES), 1) // n)
    units = [(b, p) for b in range(nb) for p in range(pairs)]
    prange = range(len(units))
    nt_dims = (((1,), (1,)), ((), ()))

    def chunk(ci, carry):
        rows = [pl.ds(pl.multiple_of(b * tt + ci * c, c), c) for b in range(nb)]
        a_t, r_t, b_t, k_t, b_h, k_h, vc, g_c = [], [], [], [], [], [], [], []
        for b in range(nb):
            rc = st_ref[0, rows[b], :]
            lwc = st_ref[1, rows[b], :]
            kc = st_ref[2, rows[b], :]
            ac = st_ref[4, rows[b], :]
            bc = st_ref[5, rows[b], :]
            l1 = lwc.astype(BF16)
            rem = lwc - l1.astype(F32)
            l2 = rem.astype(BF16)
            l3 = (rem - l2.astype(F32)).astype(BF16)
            cum = jnp.dot(tri3, jnp.concatenate([l1, l2, l3], axis=0), preferred_element_type=F32)
            last = cum[c - 1:c, :]
            e_neg = jnp.exp(-cum)
            e_tail = jnp.exp(last - cum)
            r_tb = rc * jnp.exp(cum)
            a_tb = ac * jnp.exp(cum - lwc)
            vb = st_ref[3, rows[b], :]
            for p in range(pairs):
                sl = slice(p * LANES, (p + 1) * LANES)
                a_t.append(a_tb[:, sl])
                r_t.append(r_tb[:, sl])
                b_t.append(bc[:, sl] * e_neg[:, sl])
                k_t.append(kc[:, sl] * e_neg[:, sl])
                b_h.append(bc[:, sl] * e_tail[:, sl])
                k_h.append(kc[:, sl] * e_tail[:, sl])
                vc.append(vb[:, sl])
                g_c.append(jnp.exp(last[:, sl]))

        a_ab, a_ak, q_b, q_k = [], [], [], []
        for p in prange:
            lhs = jnp.concatenate([a_t[p], r_t[p]], axis=0).astype(BF16)
            rhs = jnp.concatenate([_bd(b_t[p], lane_lo), _bd(k_t[p], lane_lo)], axis=0)
            aa = lax.dot_general(lhs, rhs, nt_dims, preferred_element_type=F32)
            a_ab.append(jnp.where(strict, aa[:c, :LANES], 0.0))
            a_ak.append(jnp.where(strict, aa[:c, LANES:], 0.0))
            q_b.append(jnp.where(lower, aa[c:, :LANES], 0.0))
            q_k.append(jnp.where(lower, aa[c:, LANES:], 0.0))

        ad = [jnp.where(blk16, x, 0.0) for x in a_ab]
        pw = [_mm1(ad[p], _bd(ad[p], lane_lo)) for p in prange]
        t = [eye + ad[p] for p in prange]
        for _ in range(2):
            both = [_mm1(jnp.concatenate([pw[p], t[p]], axis=0), _bd(pw[p], lane_lo)) for p in prange]
            pw = [x[:c] for x in both]
            t = [t[p] + both[p][c:] for p in prange]
        t = [t[p] + _mm1(t[p], _bd(pw[p], lane_lo)) for p in prange]
        for msk in (off16, jnp.logical_not(blk32)):
            te = [_mm1(t[p], _bd(jnp.where(msk, a_ab[p], 0.0), lane_lo)) for p in prange]
            t = [t[p] + _mm1(te[p], _bd(t[p], lane_lo)) for p in prange]

        v_r = [_bd(vc[p], lane_lo) for p in prange]
        pv = [_mm1(a_ak[p], v_r[p]) for p in prange]
        wu = [_mm1(t[p], jnp.concatenate([_bd(a_t[p], lane_lo), _bd(pv[p], lane_lo)], axis=1))
              for p in prange]

        for p in prange:
            bi, pi = units[p]
            s0 = s_ref[p]
            ws = lax.dot_general(jnp.concatenate([wu[p][:, :LANES], r_t[p]], axis=0).astype(BF16),
                                 s0.astype(BF16), nt_dims, preferred_element_type=F32)
            u_mat = ws[:c] + wu[p][:, LANES:]
            y = ws[c:] + _mm1(jnp.concatenate([q_b[p], q_k[p]], axis=1),
                              jnp.concatenate([_bd(u_mat, lane_lo), v_r[p]], axis=0))
            y_ref[rows[bi], pi * LANES:(pi + 1) * LANES] = y
            ds = _mm1(jnp.concatenate([u_mat, vc[p]], axis=0).T,
                      jnp.concatenate([b_h[p], k_h[p]], axis=0).astype(BF16))
            s_ref[p] = s0 * g_c[p] + jnp.where(bd_mask, ds, 0.0)
        return carry

    lax.fori_loop(0, tt // c, chunk, 0)

    y = y_ref[...]
    mean = _seg_sum(y, seg_ones) * (1.0 / n)
    dlt = y - mean
    var = _seg_sum(dlt * dlt, seg_ones) * (1.0 / n)
    yn = dlt * lax.rsqrt(var + RWKV_LN_EPS) * lng_ref[...] + lnb_ref[...]
    bonus = _seg_sum(st_ref[0] * st_ref[2] * rk_ref[...], seg_ones) * st_ref[3]
    gate = _dot(_sigmoid(cols(o_gl, o_gl + RWKV_GATE_RANK)), g2_ref[...])
    o_ref[...] = ((yn + bonus) * gate).reshape(nb, tt, w).astype(o_ref.dtype)


def _rwkv(zr, w0, w2, a0, a2, g2, k_k, k_a, r_k, ln_g, ln_b, tt):
    b, t, nz = zr.shape
    w = w0.shape[-1]
    const = lambda *shape: pl.BlockSpec(shape, lambda j: (0,) * len(shape))
    return pl.pallas_call(
        _rwkv_kernel,
        grid=(t // tt,),
        in_specs=[pl.BlockSpec((b, tt, nz), lambda j: (0, j, 0)),
                  const(1, w), const(*w2.shape), const(1, w), const(*a2.shape), const(*g2.shape),
                  const(1, w), const(1, w), const(1, w), const(1, w), const(1, w)],
        out_specs=pl.BlockSpec((b, tt, w), lambda j: (0, j, 0)),
        out_shape=jax.ShapeDtypeStruct((b, t, w), BF16),
        scratch_shapes=[pltpu.VMEM((b * w // LANES, LANES, LANES), F32),
                        pltpu.VMEM((6, b * tt, w), F32),
                        pltpu.VMEM((b * tt, w), F32)],
        compiler_params=_params(1),
        name="rwkv",
    )(zr, w0, w2, a0, a2, g2, k_k, k_a, r_k, ln_g, ln_b)


def _post_attn_kernel(x_ref, og_ref, or_ref, wog_ref, wor_ref, gx_ref, wq_ref, k_ref, v_ref, wo_ref, out_ref):
    d = x_ref.shape[-1]
    hd = d // MEM_HEADS
    x1 = (x_ref[...] + jnp.dot(og_ref[...], wog_ref[...], preferred_element_type=F32)
          + jnp.dot(or_ref[...], wor_ref[...], preferred_element_type=F32))
    q = _dot(_rms_norm(x1, gx_ref[...], NORM_EPS), wq_ref[...]).astype(BF16)
    heads = []
    for h in range(MEM_HEADS):
        hs = slice(h * hd, (h + 1) * hd)
        s = lax.dot_general(q[:, hs], k_ref[:, hs], (((1,), (1,)), ((), ())),
                            preferred_element_type=F32) * (hd ** -0.5)
        e = jnp.exp(s - jnp.max(s, axis=-1, keepdims=True))
        p = e / jnp.sum(e, axis=-1, keepdims=True)
        heads.append(_dot(p, v_ref[:, hs]))
    o = jnp.concatenate(heads, axis=-1)
    out_ref[...] = x1 + _dot(o, wo_ref[...])


def _post_attn(x, o_gla, o_rwkv, wo_g, wo_r, gx, wq, kmem, vmem, wo, tm):
    b, t, d = x.shape
    m = kmem.shape[1]
    const = lambda *shape: pl.BlockSpec(shape, lambda i, j: (0,) * len(shape))
    tile = lambda width: pl.BlockSpec((None, tm, width), lambda i, j: (i, j, 0))
    return pl.pallas_call(
        _post_attn_kernel,
        grid=(b, t // tm),
        in_specs=[tile(d), tile(o_gla.shape[-1]), tile(o_rwkv.shape[-1]),
                  const(*wo_g.shape), const(*wo_r.shape), const(1, d), const(*wq.shape),
                  pl.BlockSpec((None, m, d), lambda i, j: (i, 0, 0)),
                  pl.BlockSpec((None, m, d), lambda i, j: (i, 0, 0)),
                  const(*wo.shape)],
        out_specs=tile(d),
        out_shape=jax.ShapeDtypeStruct((b, t, d), F32),
        compiler_params=_params(2),
        name="post_attn",
    )(x, o_gla, o_rwkv, wo_g, wo_r, gx, wq, kmem, vmem, wo)


def _ffn_kernel(x_ref, g_ref, wgu_ref, wd_ref, gf_ref, out_ref, *, ff_chunk):
    x = x_ref[...]
    d_ff = wd_ref.shape[0]
    h = _rms_norm(x, g_ref[...], NORM_EPS).astype(BF16)
    acc = x
    for c0 in range(0, d_ff, ff_chunk):
        gate = jnp.dot(h, wgu_ref[:, c0:c0 + ff_chunk], preferred_element_type=F32)
        up = jnp.dot(h, wgu_ref[:, d_ff + c0:d_ff + c0 + ff_chunk], preferred_element_type=F32)
        act = (gate * _sigmoid(gate) * up).astype(BF16)
        acc = acc + jnp.dot(act, wd_ref[c0:c0 + ff_chunk, :], preferred_element_type=F32)
    out_ref[...] = _rms_norm(acc, gf_ref[...], NORM_EPS)


def _ffn(x, g, wgu, wd, gf, tm, ff_chunk):
    b, t, d = x.shape
    const = lambda *shape: pl.BlockSpec(shape, lambda i, j: (0,) * len(shape))
    tile = pl.BlockSpec((None, tm, d), lambda i, j: (i, j, 0))
    return pl.pallas_call(
        functools.partial(_ffn_kernel, ff_chunk=ff_chunk),
        grid=(b, t // tm),
        in_specs=[tile, const(1, d), const(*wgu.shape), const(*wd.shape), const(1, d)],
        out_specs=tile,
        out_shape=jax.ShapeDtypeStruct((b, t, d), F32),
        compiler_params=_params(2),
        name="ffn",
    )(x, g, wgu, wd, gf)


def _layer(x, mem, norm_mix_g, w_in, gla_wa2, gla_ba, gla_norm_g, rwkv_mu, rwkv_w0, rwkv_w2, rwkv_a0,
           rwkv_a2, rwkv_g2, rwkv_k_k, rwkv_k_a, rwkv_r_k, rwkv_ln_g, rwkv_ln_b, w_out, norm_mem_x_g,
           norm_mem_g, wq_mem, wkv_mem, wo_mem, norm_ffn_g, w_gate_up, w_down, final_g):
    b, t, d = x.shape
    m = mem.shape[1]
    row = lambda p: p.reshape(1, -1).astype(F32)
    kw = GLA_HEADS * GLA_DK
    vw = GLA_HEADS * GLA_DV
    n_gla_main = 2 * kw + 2 * vw
    n_gla = n_gla_main + GLA_GATE_RANK
    pad = LANES - GLA_GATE_RANK
    w_gla = jnp.pad(w_in[:, :n_gla], ((0, 0), (0, pad))).astype(BF16)
    w_rwkv = w_in[:, n_gla:].astype(BF16)
    wa2p = jnp.pad(gla_wa2, ((0, pad), (0, 0))).astype(BF16)

    kmem, vmem = _mem_kv(mem.reshape(b * m, d), row(norm_mem_g), wkv_mem.astype(BF16))
    zg, zr = _in_proj(x, row(norm_mix_g), w_gla, w_rwkv, wa2p, row(gla_ba), row(rwkv_mu), tm=256)
    o_gla = _gla(zg, row(gla_norm_g), tt=512)
    o_rwkv = _rwkv(zr, row(rwkv_w0), rwkv_w2.astype(BF16), row(rwkv_a0), rwkv_a2.astype(BF16),
                   rwkv_g2.astype(BF16), row(rwkv_k_k), row(rwkv_k_a), row(rwkv_r_k), row(rwkv_ln_g),
                   row(rwkv_ln_b), tt=256)
    w_out_b = w_out.astype(BF16)
    x2 = _post_attn(x, o_gla, o_rwkv, w_out_b[:vw], w_out_b[vw:], row(norm_mem_x_g), wq_mem.astype(BF16),
                    kmem.reshape(b, m, d), vmem.reshape(b, m, d), wo_mem.astype(BF16), tm=512)
    return _ffn(x2, row(norm_ffn_g), w_gate_up.astype(BF16), w_down.astype(BF16), row(final_g),
                tm=512, ff_chunk=256)


def kernel(x, mem, norm_mix_g, w_in, gla_wa2, gla_ba, gla_norm_g, rwkv_mu, rwkv_w0, rwkv_w2, rwkv_a0, rwkv_a2, rwkv_g2, rwkv_k_k, rwkv_k_a, rwkv_r_k, rwkv_ln_g, rwkv_ln_b, w_out, norm_mem_x_g, norm_mem_g, wq_mem, wkv_mem, wo_mem, norm_ffn_g, w_gate_up, w_down, norm_final_g):
    assert norm_mix_g.shape[0] == 1, "single-layer block"
    return _layer(x, mem, norm_mix_g[0], w_in[0], gla_wa2[0], gla_ba[0], gla_norm_g[0], rwkv_mu[0],
                  rwkv_w0[0], rwkv_w2[0], rwkv_a0[0], rwkv_a2[0], rwkv_g2[0], rwkv_k_k[0], rwkv_k_a[0],
                  rwkv_r_k[0], rwkv_ln_g[0], rwkv_ln_b[0], w_out[0], norm_mem_x_g[0], norm_mem_g[0],
                  wq_mem[0], wkv_mem[0], wo_mem[0], norm_ffn_g[0], w_gate_up[0], w_down[0], norm_final_g)
```

```python
import functools
import math

import jax
import jax.numpy as jnp
from jax import lax
from jax.experimental import pallas as pl
from jax.experimental.pallas import tpu as pltpu

F32 = jnp.float32
BF16 = jnp.bfloat16
HIGHEST = lax.Precision.HIGHEST

MEM_HEADS = 4
GLA_HEADS = 4
GLA_DK = 64
GLA_DV = 128
GLA_GATE_RANK = 16
GLA_GATE_NORMALIZER = 16.0
GLA_NORM_EPS = 1e-5
RWKV_HEAD = 64
RWKV_DECAY_RANK = 64
RWKV_AAA_RANK = 64
RWKV_GATE_RANK = 128
RWKV_LN_EPS = 64e-5
NORM_EPS = 1e-6

CHUNK = 64
GLA_CHUNKS_PER_ITER = 2
LANES = 128
VMEM_LIMIT = 56 * 1024 * 1024


def _params(n_grid_dims):
    return pltpu.CompilerParams(dimension_semantics=("arbitrary",) * n_grid_dims,
                                vmem_limit_bytes=VMEM_LIMIT)


def _dot(a, b):
    return jnp.dot(a.astype(BF16), b.astype(BF16), preferred_element_type=F32)


def _dot_nt(a, b):
    return lax.dot_general(a.astype(BF16), b.astype(BF16), (((1,), (1,)), ((), ())),
                           preferred_element_type=F32)


def _dotf(a, b):
    return jnp.dot(a, b, precision=HIGHEST, preferred_element_type=F32)


def _rms_norm(x, g, eps):
    return x * lax.rsqrt(jnp.mean(x * x, axis=-1, keepdims=True) + eps) * g


def _sigmoid(x):
    return 1.0 / (1.0 + jnp.exp(-x))


def _log_sigmoid(x):
    return jnp.minimum(x, 0.0) - jnp.log1p(jnp.exp(-jnp.abs(x)))


def _iota2(shape, dim):
    return lax.broadcasted_iota(jnp.int32, shape, dim)


def _tri_incl(n):
    return (_iota2((n, n), 0) >= _iota2((n, n), 1)).astype(F32)


def _mem_kv_kernel(mem_ref, g_ref, w_ref, k_ref, v_ref):
    d = k_ref.shape[-1]
    m = _rms_norm(mem_ref[...], g_ref[...], NORM_EPS)
    kv = _dot(m, w_ref[...])
    k_ref[...] = kv[:, :d].astype(BF16)
    v_ref[...] = kv[:, d:].astype(BF16)


def _mem_kv(mem2d, g, wkv):
    n, d = mem2d.shape
    return pl.pallas_call(
        _mem_kv_kernel,
        out_shape=(jax.ShapeDtypeStruct((n, d), BF16), jax.ShapeDtypeStruct((n, d), BF16)),
        compiler_params=pltpu.CompilerParams(vmem_limit_bytes=VMEM_LIMIT),
        name="mem_kv",
    )(mem2d, g, wkv)


def _in_proj_kernel(x_ref, g_ref, wg_ref, wr_ref, wa2_ref, ba_ref, mu_ref, zg_ref, zr_ref, prev_ref):
    tm = x_ref.shape[0]
    n_main = zg_ref.shape[-1] - wa2_ref.shape[-1]
    h = _rms_norm(x_ref[...], g_ref[...], NORM_EPS).astype(BF16)

    zg = jnp.dot(h, wg_ref[...], preferred_element_type=F32)
    zg_ref[:, :n_main] = zg[:, :n_main]
    pre = _dot(zg[:, n_main:], wa2_ref[...]) + ba_ref[...]
    zg_ref[:, n_main:] = _log_sigmoid(pre) * (1.0 / GLA_GATE_NORMALIZER)

    zr = jnp.dot(h, wr_ref[...], preferred_element_type=F32)

    @pl.when(pl.program_id(1) == 0)
    def _():
        prev_ref[...] = jnp.zeros_like(prev_ref)

    rolled = pltpu.roll(zr, shift=1, axis=0)
    first = _iota2(zr.shape, 0) == 0
    shifted = jnp.where(first, jnp.broadcast_to(prev_ref[0:1, :], zr.shape), rolled)
    zr_ref[...] = zr + (shifted - zr) * mu_ref[...]
    prev_ref[0:1, :] = zr[tm - 1:tm, :]


def _in_proj(x, g, w_gla, w_rwkv, wa2p, ba, mu, tm):
    b, t, d = x.shape
    ng = w_gla.shape[1] - LANES + wa2p.shape[1]
    nr = w_rwkv.shape[1]
    const = lambda *shape: pl.BlockSpec(shape, lambda i, j: (0,) * len(shape))
    return pl.pallas_call(
        _in_proj_kernel,
        grid=(b, t // tm),
        in_specs=[
            pl.BlockSpec((None, tm, d), lambda i, j: (i, j, 0)),
            const(1, d), const(*w_gla.shape), const(*w_rwkv.shape), const(*wa2p.shape),
            const(1, wa2p.shape[1]), const(1, nr),
        ],
        out_specs=(pl.BlockSpec((None, tm, ng), lambda i, j: (i, j, 0)),
                   pl.BlockSpec((None, tm, nr), lambda i, j: (i, j, 0))),
        out_shape=(jax.ShapeDtypeStruct((b, t, ng), F32), jax.ShapeDtypeStruct((b, t, nr), F32)),
        scratch_shapes=[pltpu.VMEM((8, nr), F32)],
        compiler_params=_params(2),
        name="in_proj",
    )(x, g, w_gla, w_rwkv, wa2p, ba, mu)


def _gla_kernel(z_ref, ng_ref, o_ref, s_ref):
    nb, tt, _ = z_ref.shape
    kw = GLA_HEADS * GLA_DK
    vw = GLA_HEADS * GLA_DV
    c = CHUNK
    pairs = GLA_HEADS // 2
    pk = 2 * GLA_DK
    pv = 2 * GLA_DV
    cpi = GLA_CHUNKS_PER_ITER

    @pl.when(pl.program_id(0) == 0)
    def _():
        s_ref[...] = jnp.zeros_like(s_ref)

    tri3 = (_iota2((c, 3 * c), 0) >= _iota2((c, 3 * c), 1) % c).astype(BF16)
    causal = _iota2((c, pk), 0) >= _iota2((c, pk), 1) % c
    lane_lo = _iota2((c, pk), 1) < GLA_DK
    v_lo = _iota2((c, pv), 1) < GLA_DV
    s_mask = (_iota2((pv, pk), 0) // GLA_DV) == (_iota2((pv, pk), 1) // GLA_DK)
    norm_g = ng_ref[...]
    nt_dims = (((1,), (1,)), ((), ()))
    tn_dims = (((0,), (0,)), ((), ()))

    def step(it, carry):
        units = [(b, j, p) for b in range(nb) for j in range(cpi) for p in range(pairs)]
        rows = {(b, j): pl.ds(pl.multiple_of((it * cpi + j) * c, c), c) for b in range(nb) for j in range(cpi)}
        q_dec, k_inv, k_tail, dec, vv = {}, {}, {}, {}, {}
        for (b, j), rs in rows.items():
            q = z_ref[b, rs, 0:kw] * (GLA_DK ** -0.5)
            k = z_ref[b, rs, kw:2 * kw]
            log_a = z_ref[b, rs, 2 * kw + 2 * vw:3 * kw + 2 * vw]
            l1 = log_a.astype(BF16)
            rem = log_a - l1.astype(F32)
            l2 = rem.astype(BF16)
            l3 = (rem - l2.astype(F32)).astype(BF16)
            bcum = jnp.dot(tri3, jnp.concatenate([l1, l2, l3], axis=0), preferred_element_type=F32)
            blast = bcum[c - 1:c, :]
            qd = q * jnp.exp(bcum)
            ki = k * jnp.exp(-bcum)
            kt = k * jnp.exp(blast - bcum)
            dc = jnp.exp(blast)
            for p in range(pairs):
                ks = slice(p * pk, (p + 1) * pk)
                q_dec[b, j, p] = qd[:, ks].astype(BF16)
                k_inv[b, j, p] = ki[:, ks]
                k_tail[b, j, p] = kt[:, ks].astype(BF16)
                dec[b, j, p] = dc[:, ks]
                vv[b, j, p] = z_ref[b, rs, 2 * kw + p * pv:2 * kw + (p + 1) * pv].astype(BF16)

        o_intra, d_state = {}, {}
        for u in units:
            scores = jnp.where(causal, lax.dot_general(q_dec[u], _bd(k_inv[u], lane_lo), nt_dims,
                                                       preferred_element_type=F32), 0.0)
            zero = jnp.zeros_like(vv[u])
            v_bd = jnp.concatenate([jnp.where(v_lo, vv[u], zero), jnp.where(v_lo, zero, vv[u])], axis=0)
            o_intra[u] = _mm1(scores, v_bd)
            d_state[u] = jnp.where(s_mask, lax.dot_general(vv[u], k_tail[u], tn_dims,
                                                           preferred_element_type=F32), 0.0)

        for b in range(nb):
            s_cur = [s_ref[b * pairs + p] for p in range(pairs)]
            for j in range(cpi):
                outs = []
                for p in range(pairs):
                    u = (b, j, p)
                    o = o_intra[u] + lax.dot_general(q_dec[u], s_cur[p].astype(BF16), nt_dims,
                                                     preferred_element_type=F32)
                    s_cur[p] = s_cur[p] * dec[u] + d_state[u]
                    for h in range(2):
                        oh = o[:, h * GLA_DV:(h + 1) * GLA_DV]
                        oh = oh * lax.rsqrt(jnp.mean(oh * oh, axis=-1, keepdims=True) + GLA_NORM_EPS) * norm_g
                        g0 = 2 * kw + vw + (2 * p + h) * GLA_DV
                        gh = z_ref[b, rows[b, j], g0:g0 + GLA_DV]
                        outs.append(oh * (gh * _sigmoid(gh)))
                o_ref[b, rows[b, j], :] = jnp.concatenate(outs, axis=-1).astype(o_ref.dtype)
            for p in range(pairs):
                s_ref[b * pairs + p] = s_cur[p]
        return carry

    lax.fori_loop(0, tt // (c * cpi), step, 0)


def _gla(zg, norm_g, tt):
    b, t, n = zg.shape
    vw = GLA_HEADS * GLA_DV
    return pl.pallas_call(
        _gla_kernel,
        grid=(t // tt,),
        in_specs=[pl.BlockSpec((b, tt, n), lambda j: (0, j, 0)),
                  pl.BlockSpec((1, GLA_DV), lambda j: (0, 0))],
        out_specs=pl.BlockSpec((b, tt, vw), lambda j: (0, j, 0)),
        out_shape=jax.ShapeDtypeStruct((b, t, vw), BF16),
        scratch_shapes=[pltpu.VMEM((b * GLA_HEADS // 2, 2 * GLA_DV, 2 * GLA_DK), F32)],
        compiler_params=_params(1),
        name="gla",
    )(zg, norm_g)


def _split(x):
    hi = x.astype(BF16)
    lo = (x - hi.astype(F32)).astype(BF16)
    return hi, lo


def _seg_sum(x, seg_ones):
    hi, lo = _split(x)
    blk = seg_ones.shape[0]
    parts = []
    for s in range(x.shape[1] // blk):
        sl = slice(s * blk, (s + 1) * blk)
        parts.append(jnp.dot(hi[:, sl], seg_ones, preferred_element_type=F32)
                     + jnp.dot(lo[:, sl], seg_ones, preferred_element_type=F32))
    return jnp.concatenate(parts, axis=-1)


def _block_diag(z, lane_lo):
    zero = jnp.zeros_like(z)
    return jnp.concatenate([jnp.where(lane_lo, z, zero), jnp.where(lane_lo, zero, z)], axis=0)


def _bd(z, lane_lo):
    return _block_diag(z.astype(BF16), lane_lo)


def _mm1(x, r):
    return jnp.dot(x.astype(BF16), r, preferred_element_type=F32)


def _rwkv_kernel(z_ref, w0_ref, w2_ref, a0_ref, a2_ref, g2_ref, kk_ref, ka_ref, rk_ref, lng_ref, lnb_ref,
                 o_ref, s_ref, st_ref, y_ref):
    nb, tt, _ = z_ref.shape
    w = w0_ref.shape[-1]
    n = RWKV_HEAD
    pairs = w // LANES
    c = CHUNK
    o_wl = 3 * w
    o_al = o_wl + RWKV_DECAY_RANK
    o_gl = o_al + RWKV_AAA_RANK

    @pl.when(pl.program_id(0) == 0)
    def _():
        s_ref[...] = jnp.zeros_like(s_ref)

    seg = 2 * LANES
    seg_ones = (_iota2((seg, seg), 0) // n == _iota2((seg, seg), 1) // n).astype(BF16)

    cols = lambda lo, hi: z_ref[:, :, lo:hi].reshape(nb * tt, hi - lo)
    r = cols(0, w)
    k = cols(w, 2 * w)
    v = cols(2 * w, 3 * w)
    u = w0_ref[...] + _dot(jnp.tanh(cols(o_wl, o_al)), w2_ref[...])
    lw = -math.exp(-0.5) * _sigmoid(u)
    a = _sigmoid(a0_ref[...] + _dot(cols(o_al, o_gl), a2_ref[...]))
    kk = k * kk_ref[...]
    kk = kk / jnp.maximum(jnp.sqrt(_seg_sum(kk * kk, seg_ones)), 1e-12)
    k2 = k * (1.0 + (a - 1.0) * ka_ref[...])
    st_ref[0] = r
    st_ref[1] = lw
    st_ref[2] = k2
    st_ref[3] = v
    st_ref[4] = -kk
    st_ref[5] = kk * a

    tri3 = (_iota2((c, 3 * c), 0) >= _iota2((c, 3 * c), 1) % c).astype(BF16)
    row = _iota2((c, LANES), 0)
    col = _iota2((c, LANES), 1) % c
    lower = row >= col
    strict = row > col
    eye = (row == col).astype(F32)
    blk16 = (row // 16) == (col // 16)
    blk32 = (row // 32) == (col // 32)
    off16 = jnp.logical_and(blk32, jnp.logical_not(blk16))
    lane_lo = _iota2((c, LANES), 1) < n
    bd_mask = (_iota2((LANES, LANES), 0) // n) == (_iota2((LANES, LANES), 1) // n)
    units = [(b, p) for b in range(nb) for p in range(pairs)]
    prange = range(len(units))
    nt_dims = (((1,), (1,)), ((), ()))

    def chunk(ci, carry):
        rows = [pl.ds(pl.multiple_of(b * tt + ci * c, c), c) for b in range(nb)]
        a_t, r_t, b_t, k_t, b_h, k_h, vc, g_c = [], [], [], [], [], [], [], []
        for b in range(nb):
            rc = st_ref[0, rows[b], :]
            lwc = st_ref[1, rows[b], :]
            kc = st_ref[2, rows[b], :]
            ac = st_ref[4, rows[b], :]
            bc = st_ref[5, rows[b], :]
            l1 = lwc.astype(BF16)
            rem = lwc - l1.astype(F32)
            l2 = rem.astype(BF16)
            l3 = (rem - l2.astype(F32)).astype(BF16)
            cum = jnp.dot(tri3, jnp.concatenate([l1, l2, l3], axis=0), preferred_element_type=F32)
            last = cum[c - 1:c, :]
            e_neg = jnp.exp(-cum)
            e_tail = jnp.exp(last - cum)
            r_tb = rc * jnp.exp(cum)
            a_tb = ac * jnp.exp(cum - lwc)
            vb = st_ref[3, rows[b], :]
            for p in range(pairs):
                sl = slice(p * LANES, (p + 1) * LANES)
                a_t.append(a_tb[:, sl])
                r_t.append(r_tb[:, sl])
                b_t.append(bc[:, sl] * e_neg[:, sl])
                k_t.append(kc[:, sl] * e_neg[:, sl])
                b_h.append(bc[:, sl] * e_tail[:, sl])
                k_h.append(kc[:, sl] * e_tail[:, sl])
                vc.append(vb[:, sl])
                g_c.append(jnp.exp(last[:, sl]))

        a_ab, a_ak, q_b, q_k = [], [], [], []
        for p in prange:
            lhs = jnp.concatenate([a_t[p], r_t[p]], axis=0).astype(BF16)
            rhs = jnp.concatenate([_bd(b_t[p], lane_lo), _bd(k_t[p], lane_lo)], axis=0)
            aa = lax.dot_general(lhs, rhs, nt_dims, preferred_element_type=F32)
            a_ab.append(jnp.where(strict, aa[:c, :LANES], 0.0))
            a_ak.append(jnp.where(strict, aa[:c, LANES:], 0.0))
            q_b.append(jnp.where(lower, aa[c:, :LANES], 0.0))
            q_k.append(jnp.where(lower, aa[c:, LANES:], 0.0))

        ad = [jnp.where(blk16, x, 0.0) for x in a_ab]
        pw = [_mm1(ad[p], _bd(ad[p], lane_lo)) for p in prange]
        t = [eye + ad[p] for p in prange]
        for _ in range(2):
            both = [_mm1(jnp.concatenate([pw[p], t[p]], axis=0), _bd(pw[p], lane_lo)) for p in prange]
            pw = [x[:c] for x in both]
            t = [t[p] + both[p][c:] for p in prange]
        t = [t[p] + _mm1(t[p], _bd(pw[p], lane_lo)) for p in prange]
        for msk in (off16, jnp.logical_not(blk32)):
            te = [_mm1(t[p], _bd(jnp.where(msk, a_ab[p], 0.0), lane_lo)) for p in prange]
            t = [t[p] + _mm1(te[p], _bd(t[p], lane_lo)) for p in prange]

        v_r = [_bd(vc[p], lane_lo) for p in prange]
        pv = [_mm1(a_ak[p], v_r[p]) for p in prange]
        wu = [_mm1(t[p], jnp.concatenate([_bd(a_t[p], lane_lo), _bd(pv[p], lane_lo)], axis=1))
              for p in prange]

        for p in prange:
            bi, pi = units[p]
            s0 = s_ref[p]
            ws = lax.dot_general(jnp.concatenate([wu[p][:, :LANES], r_t[p]], axis=0).astype(BF16),
                                 s0.astype(BF16), nt_dims, preferred_element_type=F32)
            u_mat = ws[:c] + wu[p][:, LANES:]
            y = ws[c:] + _mm1(jnp.concatenate([q_b[p], q_k[p]], axis=1),
                              jnp.concatenate([_bd(u_mat, lane_lo), v_r[p]], axis=0))
            y_ref[rows[bi], pi * LANES:(pi + 1) * LANES] = y
            ds = _mm1(jnp.concatenate([u_mat, vc[p]], axis=0).T,
                      jnp.concatenate([b_h[p], k_h[p]], axis=0).astype(BF16))
            s_ref[p] = s0 * g_c[p] + jnp.where(bd_mask, ds, 0.0)
        return carry

    lax.fori_loop(0, tt // c, chunk, 0)

    y = y_ref[...]
    mean = _seg_sum(y, seg_ones) * (1.0 / n)
    dlt = y - mean
    var = _seg_sum(dlt * dlt, seg_ones) * (1.0 / n)
    yn = dlt * lax.rsqrt(var + RWKV_LN_EPS) * lng_ref[...] + lnb_ref[...]
    bonus = _seg_sum(st_ref[0] * st_ref[2] * rk_ref[...], seg_ones) * st_ref[3]
    gate = _dot(_sigmoid(cols(o_gl, o_gl + RWKV_GATE_RANK)), g2_ref[...])
    o_ref[...] = ((yn + bonus) * gate).reshape(nb, tt, w).astype(o_ref.dtype)


def _rwkv(zr, w0, w2, a0, a2, g2, k_k, k_a, r_k, ln_g, ln_b, tt):
    b, t, nz = zr.shape
    w = w0.shape[-1]
    const = lambda *shape: pl.BlockSpec(shape, lambda j: (0,) * len(shape))
    return pl.pallas_call(
        _rwkv_kernel,
        grid=(t // tt,),
        in_specs=[pl.BlockSpec((b, tt, nz), lambda j: (0, j, 0)),
                  const(1, w), const(*w2.shape), const(1, w), const(*a2.shape), const(*g2.shape),
                  const(1, w), const(1, w), const(1, w), const(1, w), const(1, w)],
        out_specs=pl.BlockSpec((b, tt, w), lambda j: (0, j, 0)),
        out_shape=jax.ShapeDtypeStruct((b, t, w), BF16),
        scratch_shapes=[pltpu.VMEM((b * w // LANES, LANES, LANES), F32),
                        pltpu.VMEM((6, b * tt, w), F32),
                        pltpu.VMEM((b * tt, w), F32)],
        compiler_params=_params(1),
        name="rwkv",
    )(zr, w0, w2, a0, a2, g2, k_k, k_a, r_k, ln_g, ln_b)


def _post_attn_kernel(x_ref, og_ref, or_ref, wog_ref, wor_ref, gx_ref, wq_ref, k_ref, v_ref, wo_ref, out_ref):
    d = x_ref.shape[-1]
    hd = d // MEM_HEADS
    x1 = (x_ref[...] + jnp.dot(og_ref[...], wog_ref[...], preferred_element_type=F32)
          + jnp.dot(or_ref[...], wor_ref[...], preferred_element_type=F32))
    q = _dot(_rms_norm(x1, gx_ref[...], NORM_EPS), wq_ref[...]).astype(BF16)
    heads = []
    for h in range(MEM_HEADS):
        hs = slice(h * hd, (h + 1) * hd)
        s = lax.dot_general(q[:, hs], k_ref[:, hs], (((1,), (1,)), ((), ())),
                            preferred_element_type=F32) * (hd ** -0.5)
        e = jnp.exp(s - jnp.max(s, axis=-1, keepdims=True))
        p = e / jnp.sum(e, axis=-1, keepdims=True)
        heads.append(_dot(p, v_ref[:, hs]))
    o = jnp.concatenate(heads, axis=-1)
    out_ref[...] = x1 + _dot(o, wo_ref[...])


def _post_attn(x, o_gla, o_rwkv, wo_g, wo_r, gx, wq, kmem, vmem, wo, tm):
    b, t, d = x.shape
    m = kmem.shape[1]
    const = lambda *shape: pl.BlockSpec(shape, lambda i, j: (0,) * len(shape))
    tile = lambda width: pl.BlockSpec((None, tm, width), lambda i, j: (i, j, 0))
    return pl.pallas_call(
        _post_attn_kernel,
        grid=(b, t // tm),
        in_specs=[tile(d), tile(o_gla.shape[-1]), tile(o_rwkv.shape[-1]),
                  const(*wo_g.shape), const(*wo_r.shape), const(1, d), const(*wq.shape),
                  pl.BlockSpec((None, m, d), lambda i, j: (i, 0, 0)),
                  pl.BlockSpec((None, m, d), lambda i, j: (i, 0, 0)),
                  const(*wo.shape)],
        out_specs=tile(d),
        out_shape=jax.ShapeDtypeStruct((b, t, d), F32),
        compiler_params=_params(2),
        name="post_attn",
    )(x, o_gla, o_rwkv, wo_g, wo_r, gx, wq, kmem, vmem, wo)


def _ffn_kernel(x_ref, g_ref, wgu_ref, wd_ref, gf_ref, out_ref, *, ff_chunk):
    x = x_ref[...]
    d_ff = wd_ref.shape[0]
    h = _rms_norm(x, g_ref[...], NORM_EPS).astype(BF16)
    acc = x
    for c0 in range(0, d_ff, ff_chunk):
        gate = jnp.dot(h, wgu_ref[:, c0:c0 + ff_chunk], preferred_element_type=F32)
        up = jnp.dot(h, wgu_ref[:, d_ff + c0:d_ff + c0 + ff_chunk], preferred_element_type=F32)
        act = (gate * _sigmoid(gate) * up).astype(BF16)
        acc = acc + jnp.dot(act, wd_ref[c0:c0 + ff_chunk, :], preferred_element_type=F32)
    out_ref[...] = _rms_norm(acc, gf_ref[...], NORM_EPS)


def _ffn(x, g, wgu, wd, gf, tm, ff_chunk):
    b, t, d = x.shape
    const = lambda *shape: pl.BlockSpec(shape, lambda i, j: (0,) * len(shape))
    tile = pl.BlockSpec((None, tm, d), lambda i, j: (i, j, 0))
    return pl.pallas_call(
        functools.partial(_ffn_kernel, ff_chunk=ff_chunk),
        grid=(b, t // tm),
        in_specs=[tile, const(1, d), const(*wgu.shape), const(*wd.shape), const(1, d)],
        out_specs=tile,
        out_shape=jax.ShapeDtypeStruct((b, t, d), F32),
        compiler_params=_params(2),
        name="ffn",
    )(x, g, wgu, wd, gf)


def _layer(x, mem, norm_mix_g, w_in, gla_wa2, gla_ba, gla_norm_g, rwkv_mu, rwkv_w0, rwkv_w2, rwkv_a0,
           rwkv_a2, rwkv_g2, rwkv_k_k, rwkv_k_a, rwkv_r_k, rwkv_ln_g, rwkv_ln_b, w_out, norm_mem_x_g,
           norm_mem_g, wq_mem, wkv_mem, wo_mem, norm_ffn_g, w_gate_up, w_down, final_g):
    b, t, d = x.shape
    m = mem.shape[1]
    row = lambda p: p.reshape(1, -1).astype(F32)
    kw = GLA_HEADS * GLA_DK
    vw = GLA_HEADS * GLA_DV
    n_gla_main = 2 * kw + 2 * vw
    n_gla = n_gla_main + GLA_GATE_RANK
    pad = LANES - GLA_GATE_RANK
    w_gla = jnp.pad(w_in[:, :n_gla], ((0, 0), (0, pad))).astype(BF16)
    w_rwkv = w_in[:, n_gla:].astype(BF16)
    wa2p = jnp.pad(gla_wa2, ((0, pad), (0, 0))).astype(BF16)

    kmem, vmem = _mem_kv(mem.reshape(b * m, d), row(norm_mem_g), wkv_mem.astype(BF16))
    zg, zr = _in_proj(x, row(norm_mix_g), w_gla, w_rwkv, wa2p, row(gla_ba), row(rwkv_mu), tm=256)
    o_gla = _gla(zg, row(gla_norm_g), tt=512)
    o_rwkv = _rwkv(zr, row(rwkv_w0), rwkv_w2.astype(BF16), row(rwkv_a0), rwkv_a2.astype(BF16),
                   rwkv_g2.astype(BF16), row(rwkv_k_k), row(rwkv_k_a), row(rwkv_r_k), row(rwkv_ln_g),
                   row(rwkv_ln_b), tt=256)
    w_out_b = w_out.astype(BF16)
    x2 = _post_attn(x, o_gla, o_rwkv, w_out_b[:vw], w_out_b[vw:], row(norm_mem_x_g), wq_mem.astype(BF16),
                    kmem.reshape(b, m, d), vmem.reshape(b, m, d), wo_mem.astype(BF16), tm=512)
    return _ffn(x2, row(norm_ffn_g), w_gate_up.astype(BF16), w_down.astype(BF16), row(final_g),
                tm=512, ff_chunk=256)


def kernel(x, mem, norm_mix_g, w_in, gla_wa2, gla_ba, gla_norm_g, rwkv_mu, rwkv_w0, rwkv_w2, rwkv_a0, rwkv_a2, rwkv_g2, rwkv_k_k, rwkv_k_a, rwkv_r_k, rwkv_ln_g, rwkv_ln_b, w_out, norm_mem_x_g, norm_mem_g, wq_mem, wkv_mem, wo_mem, norm_ffn_g, w_gate_up, w_down, norm_final_g):
    assert norm_mix_g.shape[0] == 1, "single-layer block"
    return _layer(x, mem, norm_mix_g[0], w_in[0], gla_wa2[0], gla_ba[0], gla_norm_g[0], rwkv_mu[0],
                  rwkv_w0[0], rwkv_w2[0], rwkv_a0[0], rwkv_a2[0], rwkv_g2[0], rwkv_k_k[0], rwkv_k_a[0],
                  rwkv_r_k[0], rwkv_ln_g[0], rwkv_ln_b[0], w_out[0], norm_mem_x_g[0], norm_mem_g[0],
                  wq_mem[0], wkv_mem[0], wo_mem[0], norm_ffn_g[0], w_gate_up[0], w_down[0], norm_final_g)
```

```python
import functools
import math

import jax
import jax.numpy as jnp
from jax import lax
from jax.experimental import pallas as pl
from jax.experimental.pallas import tpu as pltpu

F32 = jnp.float32
BF16 = jnp.bfloat16

MEM_HEADS = 4
GLA_HEADS = 4
GLA_DK = 64
GLA_DV = 128
GLA_GATE_RANK = 16
GLA_GATE_NORMALIZER = 16.0
GLA_NORM_EPS = 1e-5
RWKV_HEAD = 64
RWKV_DECAY_RANK = 64
RWKV_AAA_RANK = 64
RWKV_GATE_RANK = 128
RWKV_LN_EPS = 64e-5
NORM_EPS = 1e-6

CHUNK = 64
GLA_CHUNKS_PER_ITER = 2
SUBTILES = 2
LANES = 128
VMEM_LIMIT = 56 * 1024 * 1024


def _params(n_grid_dims):
    return pltpu.CompilerParams(dimension_semantics=("arbitrary",) * n_grid_dims,
                                vmem_limit_bytes=VMEM_LIMIT)


def _dot(a, b):
    return jnp.dot(a.astype(BF16), b.astype(BF16), preferred_element_type=F32)


def _rms_norm(x, g, eps):
    return x * lax.rsqrt(jnp.mean(x * x, axis=-1, keepdims=True) + eps) * g


def _sigmoid(x):
    return 1.0 / (1.0 + jnp.exp(-x))


def _log_sigmoid(x):
    return jnp.minimum(x, 0.0) - jnp.log1p(jnp.exp(-jnp.abs(x)))


def _iota2(shape, dim):
    return lax.broadcasted_iota(jnp.int32, shape, dim)


def _cumsum_operator(c):
    return (_iota2((c, 3 * c), 0) >= _iota2((c, 3 * c), 1) % c).astype(BF16)


def _chunk_cumsum(tri3, x):
    p1 = x.astype(BF16)
    rem = x - p1.astype(F32)
    p2 = rem.astype(BF16)
    p3 = (rem - p2.astype(F32)).astype(BF16)
    return jnp.dot(tri3, jnp.concatenate([p1, p2, p3], axis=0), preferred_element_type=F32)


def _mem_kv_kernel(mem_ref, g_ref, w_ref, k_ref, v_ref):
    d = k_ref.shape[-1]
    m = _rms_norm(mem_ref[...], g_ref[...], NORM_EPS)
    kv = _dot(m, w_ref[...])
    k_ref[...] = kv[:, :d].astype(BF16)
    v_ref[...] = kv[:, d:].astype(BF16)


def _mem_kv(mem2d, g, wkv):
    n, d = mem2d.shape
    return pl.pallas_call(
        _mem_kv_kernel,
        out_shape=(jax.ShapeDtypeStruct((n, d), BF16), jax.ShapeDtypeStruct((n, d), BF16)),
        compiler_params=pltpu.CompilerParams(vmem_limit_bytes=VMEM_LIMIT),
        name="mem_kv",
    )(mem2d, g, wkv)


def _in_proj_kernel(x_ref, g_ref, wg_ref, wr_ref, wa2_ref, ba_ref, mu_ref, zg_ref, zr_ref, prev_ref):
    tm = x_ref.shape[0]
    n_main = zg_ref.shape[-1] - wa2_ref.shape[-1]

    @pl.when(pl.program_id(1) == 0)
    def _():
        prev_ref[...] = jnp.zeros_like(prev_ref)

    sub = tm // SUBTILES
    prev = prev_ref[0:1, :]
    for s in range(SUBTILES):
        rows = slice(s * sub, (s + 1) * sub)
        h = _rms_norm(x_ref[rows, :], g_ref[...], NORM_EPS).astype(BF16)

        zg = jnp.dot(h, wg_ref[...], preferred_element_type=F32)
        zg_ref[rows, :n_main] = zg[:, :n_main]
        pre = _dot(zg[:, n_main:], wa2_ref[...]) + ba_ref[...]
        zg_ref[rows, n_main:] = _log_sigmoid(pre) * (1.0 / GLA_GATE_NORMALIZER)

        zr = jnp.dot(h, wr_ref[...], preferred_element_type=F32)
        rolled = pltpu.roll(zr, shift=1, axis=0)
        first = _iota2(zr.shape, 0) == 0
        shifted = jnp.where(first, jnp.broadcast_to(prev, zr.shape), rolled)
        zr_ref[rows, :] = zr + (shifted - zr) * mu_ref[...]
        prev = zr[sub - 1:sub, :]
    prev_ref[0:1, :] = prev


def _in_proj(x, g, w_gla, w_rwkv, wa2p, ba, mu, tm):
    b, t, d = x.shape
    ng = w_gla.shape[1] - LANES + wa2p.shape[1]
    nr = w_rwkv.shape[1]
    const = lambda *shape: pl.BlockSpec(shape, lambda i, j: (0,) * len(shape))
    return pl.pallas_call(
        _in_proj_kernel,
        grid=(b, t // tm),
        in_specs=[
            pl.BlockSpec((None, tm, d), lambda i, j: (i, j, 0)),
            const(1, d), const(*w_gla.shape), const(*w_rwkv.shape), const(*wa2p.shape),
            const(1, wa2p.shape[1]), const(1, nr),
        ],
        out_specs=(pl.BlockSpec((None, tm, ng), lambda i, j: (i, j, 0)),
                   pl.BlockSpec((None, tm, nr), lambda i, j: (i, j, 0))),
        out_shape=(jax.ShapeDtypeStruct((b, t, ng), F32), jax.ShapeDtypeStruct((b, t, nr), F32)),
        scratch_shapes=[pltpu.VMEM((8, nr), F32)],
        compiler_params=_params(2),
        name="in_proj",
    )(x, g, w_gla, w_rwkv, wa2p, ba, mu)


def _gla_kernel(z_ref, ng_ref, o_ref, s_ref):
    nb, tt, _ = z_ref.shape
    kw = GLA_HEADS * GLA_DK
    vw = GLA_HEADS * GLA_DV
    c = CHUNK
    pairs = GLA_HEADS // 2
    pk = 2 * GLA_DK
    pv = 2 * GLA_DV
    cpi = GLA_CHUNKS_PER_ITER

    @pl.when(pl.program_id(0) == 0)
    def _():
        s_ref[...] = jnp.zeros_like(s_ref)

    tri3 = _cumsum_operator(c)
    causal = _iota2((c, pk), 0) >= _iota2((c, pk), 1) % c
    lane_lo = _iota2((c, pk), 1) < GLA_DK
    v_lo = _iota2((c, pv), 1) < GLA_DV
    s_mask = (_iota2((pv, pk), 0) // GLA_DV) == (_iota2((pv, pk), 1) // GLA_DK)
    norm_g = ng_ref[...]
    nt_dims = (((1,), (1,)), ((), ()))
    tn_dims = (((0,), (0,)), ((), ()))

    def step(it, carry):
        units = [(b, j, p) for b in range(nb) for j in range(cpi) for p in range(pairs)]
        rows = {(b, j): pl.ds(pl.multiple_of((it * cpi + j) * c, c), c) for b in range(nb) for j in range(cpi)}
        q_dec, k_inv, k_tail, dec, vv = {}, {}, {}, {}, {}
        for (b, j), rs in rows.items():
            q = z_ref[b, rs, 0:kw] * (GLA_DK ** -0.5)
            k = z_ref[b, rs, kw:2 * kw]
            log_a = z_ref[b, rs, 2 * kw + 2 * vw:3 * kw + 2 * vw]
            bcum = _chunk_cumsum(tri3, log_a)
            blast = bcum[c - 1:c, :]
            qd = q * jnp.exp(bcum)
            ki = k * jnp.exp(-bcum)
            kt = k * jnp.exp(blast - bcum)
            dc = jnp.exp(blast)
            for p in range(pairs):
                ks = slice(p * pk, (p + 1) * pk)
                q_dec[b, j, p] = qd[:, ks].astype(BF16)
                k_inv[b, j, p] = ki[:, ks]
                k_tail[b, j, p] = kt[:, ks].astype(BF16)
                dec[b, j, p] = dc[:, ks]
                vv[b, j, p] = z_ref[b, rs, 2 * kw + p * pv:2 * kw + (p + 1) * pv].astype(BF16)

        o_intra, d_state = {}, {}
        for u in units:
            scores = jnp.where(causal, lax.dot_general(q_dec[u], _bd(k_inv[u], lane_lo), nt_dims,
                                                       preferred_element_type=F32), 0.0)
            zero = jnp.zeros_like(vv[u])
            v_bd = jnp.concatenate([jnp.where(v_lo, vv[u], zero), jnp.where(v_lo, zero, vv[u])], axis=0)
            o_intra[u] = _mm1(scores, v_bd)
            d_state[u] = jnp.where(s_mask, lax.dot_general(vv[u], k_tail[u], tn_dims,
                                                           preferred_element_type=F32), 0.0)

        for b in range(nb):
            s_cur = [s_ref[b * pairs + p] for p in range(pairs)]
            for j in range(cpi):
                outs = []
                for p in range(pairs):
                    u = (b, j, p)
                    o = o_intra[u] + lax.dot_general(q_dec[u], s_cur[p].astype(BF16), nt_dims,
                                                     preferred_element_type=F32)
                    s_cur[p] = s_cur[p] * dec[u] + d_state[u]
                    for h in range(2):
                        oh = o[:, h * GLA_DV:(h + 1) * GLA_DV]
                        oh = oh * lax.rsqrt(jnp.mean(oh * oh, axis=-1, keepdims=True) + GLA_NORM_EPS) * norm_g
                        g0 = 2 * kw + vw + (2 * p + h) * GLA_DV
                        gh = z_ref[b, rows[b, j], g0:g0 + GLA_DV]
                        outs.append(oh * (gh * _sigmoid(gh)))
                o_ref[b, rows[b, j], :] = jnp.concatenate(outs, axis=-1).astype(o_ref.dtype)
            for p in range(pairs):
                s_ref[b * pairs + p] = s_cur[p]
        return carry

    lax.fori_loop(0, tt // (c * cpi), step, 0)


def _gla(zg, norm_g, tt):
    b, t, n = zg.shape
    vw = GLA_HEADS * GLA_DV
    return pl.pallas_call(
        _gla_kernel,
        grid=(t // tt,),
        in_specs=[pl.BlockSpec((b, tt, n), lambda j: (0, j, 0)),
                  pl.BlockSpec((1, GLA_DV), lambda j: (0, 0))],
        out_specs=pl.BlockSpec((b, tt, vw), lambda j: (0, j, 0)),
        out_shape=jax.ShapeDtypeStruct((b, t, vw), BF16),
        scratch_shapes=[pltpu.VMEM((b * GLA_HEADS // 2, 2 * GLA_DV, 2 * GLA_DK), F32)],
        compiler_params=_params(1),
        name="gla",
    )(zg, norm_g)


def _seg_sum(x, seg_ones):
    xb = x.astype(BF16)
    blk = seg_ones.shape[0]
    parts = []
    for s in range(x.shape[1] // blk):
        sl = slice(s * blk, (s + 1) * blk)
        parts.append(jnp.dot(xb[:, sl], seg_ones, preferred_element_type=F32))
    return jnp.concatenate(parts, axis=-1)


def _block_diag(z, lane_lo):
    zero = jnp.zeros_like(z)
    return jnp.concatenate([jnp.where(lane_lo, z, zero), jnp.where(lane_lo, zero, z)], axis=0)


def _bd(z, lane_lo):
    return _block_diag(z.astype(BF16), lane_lo)


def _mm1(x, r):
    return jnp.dot(x.astype(BF16), r, preferred_element_type=F32)


def _rwkv_kernel(z_ref, w0_ref, w2_ref, a0_ref, a2_ref, g2_ref, kk_ref, ka_ref, rk_ref, lng_ref, lnb_ref,
                 o_ref, s_ref, st_ref, y_ref):
    nb, tt, _ = z_ref.shape
    w = w0_ref.shape[-1]
    n = RWKV_HEAD
    pairs = w // LANES
    c = CHUNK
    o_wl = 3 * w
    o_al = o_wl + RWKV_DECAY_RANK
    o_gl = o_al + RWKV_AAA_RANK

    @pl.when(pl.program_id(0) == 0)
    def _():
        s_ref[...] = jnp.zeros_like(s_ref)

    seg = 2 * LANES
    seg_ones = (_iota2((seg, seg), 0) // n == _iota2((seg, seg), 1) // n).astype(BF16)

    cols = lambda lo, hi: z_ref[:, :, lo:hi].reshape(nb * tt, hi - lo)
    r = cols(0, w)
    k = cols(w, 2 * w)
    v = cols(2 * w, 3 * w)
    u = w0_ref[...] + _dot(jnp.tanh(cols(o_wl, o_al)), w2_ref[...])
    lw = -math.exp(-0.5) * _sigmoid(u)
    a = _sigmoid(a0_ref[...] + _dot(cols(o_al, o_gl), a2_ref[...]))
    kk = k * kk_ref[...]
    kk = kk * lax.rsqrt(jnp.maximum(_seg_sum(kk * kk, seg_ones), 1e-24))
    k2 = k * (1.0 + (a - 1.0) * ka_ref[...])
    st_ref[0] = r
    st_ref[1] = lw
    st_ref[2] = k2
    st_ref[3] = v
    st_ref[4] = -kk
    st_ref[5] = kk * a

    tri3 = _cumsum_operator(c)
    row = _iota2((c, LANES), 0)
    col = _iota2((c, LANES), 1) % c
    lower = row >= col
    strict = row > col
    eye = (row == col).astype(F32)
    blk16 = (row // 16) == (col // 16)
    blk32 = (row // 32) == (col // 32)
    off16 = jnp.logical_and(blk32, jnp.logical_not(blk16))
    lane_lo = _iota2((c, LANES), 1) < n
    bd_mask = (_iota2((LANES, LANES), 0) // n) == (_iota2((LANES, LANES), 1) // n)
    units = [(b, p) for b in range(nb) for p in range(pairs)]
    prange = range(len(units))
    nt_dims = (((1,), (1,)), ((), ()))

    def chunk(ci, carry):
        rows = [pl.ds(pl.multiple_of(b * tt + ci * c, c), c) for b in range(nb)]
        a_t, r_t, b_t, k_t, b_h, k_h, vc, g_c = [], [], [], [], [], [], [], []
        for b in range(nb):
            rc = st_ref[0, rows[b], :]
            lwc = st_ref[1, rows[b], :]
            kc = st_ref[2, rows[b], :]
            ac = st_ref[4, rows[b], :]
            bc = st_ref[5, rows[b], :]
            cum = _chunk_cumsum(tri3, lwc)
            last = cum[c - 1:c, :]
            e_neg = jnp.exp(-cum)
            e_tail = jnp.exp(last - cum)
            r_tb = rc * jnp.exp(cum)
            a_tb = ac * jnp.exp(cum - lwc)
            vb = st_ref[3, rows[b], :]
            for p in range(pairs):
                sl = slice(p * LANES, (p + 1) * LANES)
                a_t.append(a_tb[:, sl])
                r_t.append(r_tb[:, sl])
                b_t.append(bc[:, sl] * e_neg[:, sl])
                k_t.append(kc[:, sl] * e_neg[:, sl])
                b_h.append(bc[:, sl] * e_tail[:, sl])
                k_h.append(kc[:, sl] * e_tail[:, sl])
                vc.append(vb[:, sl])
                g_c.append(jnp.exp(last[:, sl]))

        a_ab, a_ak, q_b, q_k = [], [], [], []
        for p in prange:
            lhs = jnp.concatenate([a_t[p], r_t[p]], axis=0).astype(BF16)
            rhs = jnp.concatenate([_bd(b_t[p], lane_lo), _bd(k_t[p], lane_lo)], axis=0)
            aa = lax.dot_general(lhs, rhs, nt_dims, preferred_element_type=F32)
            a_ab.append(jnp.where(strict, aa[:c, :LANES], 0.0))
            a_ak.append(jnp.where(strict, aa[:c, LANES:], 0.0))
            q_b.append(jnp.where(lower, aa[c:, :LANES], 0.0))
            q_k.append(jnp.where(lower, aa[c:, LANES:], 0.0))

        ad = [jnp.where(blk16, x, 0.0) for x in a_ab]
        pw = [_mm1(ad[p], _bd(ad[p], lane_lo)) for p in prange]
        t = [eye + ad[p] for p in prange]
        for _ in range(2):
            both = [_mm1(jnp.concatenate([pw[p], t[p]], axis=0), _bd(pw[p], lane_lo)) for p in prange]
            pw = [x[:c] for x in both]
            t = [t[p] + both[p][c:] for p in prange]
        t = [t[p] + _mm1(t[p], _bd(pw[p], lane_lo)) for p in prange]
        for msk in (off16, jnp.logical_not(blk32)):
            te = [_mm1(t[p], _bd(jnp.where(msk, a_ab[p], 0.0), lane_lo)) for p in prange]
            t = [t[p] + _mm1(te[p], _bd(t[p], lane_lo)) for p in prange]

        v_r = [_bd(vc[p], lane_lo) for p in prange]
        pv = [_mm1(a_ak[p], v_r[p]) for p in prange]
        wu = [_mm1(t[p], jnp.concatenate([_bd(a_t[p], lane_lo), _bd(pv[p], lane_lo)], axis=1))
              for p in prange]

        for p in prange:
            bi, pi = units[p]
            s0 = s_ref[p]
            ws = lax.dot_general(jnp.concatenate([wu[p][:, :LANES], r_t[p]], axis=0).astype(BF16),
                                 s0.astype(BF16), nt_dims, preferred_element_type=F32)
            u_mat = ws[:c] + wu[p][:, LANES:]
            y = ws[c:] + _mm1(jnp.concatenate([q_b[p], q_k[p]], axis=1),
                              jnp.concatenate([_bd(u_mat, lane_lo), v_r[p]], axis=0))
            y_ref[rows[bi], pi * LANES:(pi + 1) * LANES] = y
            ds = _mm1(jnp.concatenate([u_mat, vc[p]], axis=0).T,
                      jnp.concatenate([b_h[p], k_h[p]], axis=0).astype(BF16))
            s_ref[p] = s0 * g_c[p] + jnp.where(bd_mask, ds, 0.0)
        return carry

    lax.fori_loop(0, tt // c, chunk, 0)

    y = y_ref[...]
    mean = _seg_sum(y, seg_ones) * (1.0 / n)
    dlt = y - mean
    var = _seg_sum(dlt * dlt, seg_ones) * (1.0 / n)
    yn = dlt * lax.rsqrt(var + RWKV_LN_EPS) * lng_ref[...] + lnb_ref[...]
    bonus = _seg_sum(st_ref[0] * st_ref[2] * rk_ref[...], seg_ones) * st_ref[3]
    gate = _dot(_sigmoid(cols(o_gl, o_gl + RWKV_GATE_RANK)), g2_ref[...])
    o_ref[...] = ((yn + bonus) * gate).reshape(nb, tt, w).astype(o_ref.dtype)


def _rwkv(zr, w0, w2, a0, a2, g2, k_k, k_a, r_k, ln_g, ln_b, tt):
    b, t, nz = zr.shape
    w = w0.shape[-1]
    const = lambda *shape: pl.BlockSpec(shape, lambda j: (0,) * len(shape))
    return pl.pallas_call(
        _rwkv_kernel,
        grid=(t // tt,),
        in_specs=[pl.BlockSpec((b, tt, nz), lambda j: (0, j, 0)),
                  const(1, w), const(*w2.shape), const(1, w), const(*a2.shape), const(*g2.shape),
                  const(1, w), const(1, w), const(1, w), const(1, w), const(1, w)],
        out_specs=pl.BlockSpec((b, tt, w), lambda j: (0, j, 0)),
        out_shape=jax.ShapeDtypeStruct((b, t, w), BF16),
        scratch_shapes=[pltpu.VMEM((b * w // LANES, LANES, LANES), F32),
                        pltpu.VMEM((6, b * tt, w), F32),
                        pltpu.VMEM((b * tt, w), F32)],
        compiler_params=_params(1),
        name="rwkv",
    )(zr, w0, w2, a0, a2, g2, k_k, k_a, r_k, ln_g, ln_b)


def _post_attn_kernel(x_ref, og_ref, or_ref, wog_ref, wor_ref, gx_ref, wq_ref, k_ref, v_ref, wo_ref, out_ref):
    d = x_ref.shape[-1]
    hd = d // MEM_HEADS
    x1 = (x_ref[...] + jnp.dot(og_ref[...], wog_ref[...], preferred_element_type=F32)
          + jnp.dot(or_ref[...], wor_ref[...], preferred_element_type=F32))
    q = _dot(_rms_norm(x1, gx_ref[...], NORM_EPS), wq_ref[...]).astype(BF16)
    heads = []
    for h in range(MEM_HEADS):
        hs = slice(h * hd, (h + 1) * hd)
        s = lax.dot_general(q[:, hs], k_ref[:, hs], (((1,), (1,)), ((), ())),
                            preferred_element_type=F32) * (hd ** -0.5)
        e = jnp.exp(s - jnp.max(s, axis=-1, keepdims=True))
        p = e / jnp.sum(e, axis=-1, keepdims=True)
        heads.append(_dot(p, v_ref[:, hs]))
    o = jnp.concatenate(heads, axis=-1)
    out_ref[...] = x1 + _dot(o, wo_ref[...])


def _post_attn(x, o_gla, o_rwkv, wo_g, wo_r, gx, wq, kmem, vmem, wo, tm):
    b, t, d = x.shape
    m = kmem.shape[1]
    const = lambda *shape: pl.BlockSpec(shape, lambda i, j: (0,) * len(shape))
    tile = lambda width: pl.BlockSpec((None, tm, width), lambda i, j: (i, j, 0))
    return pl.pallas_call(
        _post_attn_kernel,
        grid=(b, t // tm),
        in_specs=[tile(d), tile(o_gla.shape[-1]), tile(o_rwkv.shape[-1]),
                  const(*wo_g.shape), const(*wo_r.shape), const(1, d), const(*wq.shape),
                  pl.BlockSpec((None, m, d), lambda i, j: (i, 0, 0)),
                  pl.BlockSpec((None, m, d), lambda i, j: (i, 0, 0)),
                  const(*wo.shape)],
        out_specs=tile(d),
        out_shape=jax.ShapeDtypeStruct((b, t, d), F32),
        compiler_params=_params(2),
        name="post_attn",
    )(x, o_gla, o_rwkv, wo_g, wo_r, gx, wq, kmem, vmem, wo)


def _ffn_kernel(x_ref, g_ref, wgu_ref, wd_ref, gf_ref, out_ref, *, ff_chunk):
    x = x_ref[...]
    d_ff = wd_ref.shape[0]
    h = _rms_norm(x, g_ref[...], NORM_EPS).astype(BF16)
    acc = x
    for c0 in range(0, d_ff, ff_chunk):
        gate = jnp.dot(h, wgu_ref[:, c0:c0 + ff_chunk], preferred_element_type=F32)
        up = jnp.dot(h, wgu_ref[:, d_ff + c0:d_ff + c0 + ff_chunk], preferred_element_type=F32)
        act = (gate * _sigmoid(gate) * up).astype(BF16)
        acc = acc + jnp.dot(act, wd_ref[c0:c0 + ff_chunk, :], preferred_element_type=F32)
    out_ref[...] = _rms_norm(acc, gf_ref[...], NORM_EPS)


def _ffn(x, g, wgu, wd, gf, tm, ff_chunk):
    b, t, d = x.shape
    const = lambda *shape: pl.BlockSpec(shape, lambda i, j: (0,) * len(shape))
    tile = pl.BlockSpec((None, tm, d), lambda i, j: (i, j, 0))
    return pl.pallas_call(
        functools.partial(_ffn_kernel, ff_chunk=ff_chunk),
        grid=(b, t // tm),
        in_specs=[tile, const(1, d), const(*wgu.shape), const(*wd.shape), const(1, d)],
        out_specs=tile,
        out_shape=jax.ShapeDtypeStruct((b, t, d), F32),
        compiler_params=_params(2),
        name="ffn",
    )(x, g, wgu, wd, gf)


def _layer(x, mem, norm_mix_g, w_in, gla_wa2, gla_ba, gla_norm_g, rwkv_mu, rwkv_w0, rwkv_w2, rwkv_a0,
           rwkv_a2, rwkv_g2, rwkv_k_k, rwkv_k_a, rwkv_r_k, rwkv_ln_g, rwkv_ln_b, w_out, norm_mem_x_g,
           norm_mem_g, wq_mem, wkv_mem, wo_mem, norm_ffn_g, w_gate_up, w_down, final_g):
    b, t, d = x.shape
    m = mem.shape[1]
    row = lambda p: p.reshape(1, -1).astype(F32)
    kw = GLA_HEADS * GLA_DK
    vw = GLA_HEADS * GLA_DV
    n_gla_main = 2 * kw + 2 * vw
    n_gla = n_gla_main + GLA_GATE_RANK
    pad = LANES - GLA_GATE_RANK
    w_gla = jnp.pad(w_in[:, :n_gla], ((0, 0), (0, pad))).astype(BF16)
    w_rwkv = w_in[:, n_gla:].astype(BF16)
    wa2p = jnp.pad(gla_wa2, ((0, pad), (0, 0))).astype(BF16)

    kmem, vmem = _mem_kv(mem.reshape(b * m, d), row(norm_mem_g), wkv_mem.astype(BF16))
    zg, zr = _in_proj(x, row(norm_mix_g), w_gla, w_rwkv, wa2p, row(gla_ba), row(rwkv_mu), tm=512)
    o_gla = _gla(zg, row(gla_norm_g), tt=512)
    o_rwkv = _rwkv(zr, row(rwkv_w0), rwkv_w2.astype(BF16), row(rwkv_a0), rwkv_a2.astype(BF16),
                   rwkv_g2.astype(BF16), row(rwkv_k_k), row(rwkv_k_a), row(rwkv_r_k), row(rwkv_ln_g),
                   row(rwkv_ln_b), tt=256)
    w_out_b = w_out.astype(BF16)
    x2 = _post_attn(x, o_gla, o_rwkv, w_out_b[:vw], w_out_b[vw:], row(norm_mem_x_g), wq_mem.astype(BF16),
                    kmem.reshape(b, m, d), vmem.reshape(b, m, d), wo_mem.astype(BF16), tm=512)
    return _ffn(x2, row(norm_ffn_g), w_gate_up.astype(BF16), w_down.astype(BF16), row(final_g),
                tm=512, ff_chunk=256)


def kernel(x, mem, norm_mix_g, w_in, gla_wa2, gla_ba, gla_norm_g, rwkv_mu, rwkv_w0, rwkv_w2, rwkv_a0, rwkv_a2, rwkv_g2, rwkv_k_k, rwkv_k_a, rwkv_r_k, rwkv_ln_g, rwkv_ln_b, w_out, norm_mem_x_g, norm_mem_g, wq_mem, wkv_mem, wo_mem, norm_ffn_g, w_gate_up, w_down, norm_final_g):
    assert norm_mix_g.shape[0] == 1, "single-layer block"
    return _layer(x, mem, norm_mix_g[0], w_in[0], gla_wa2[0], gla_ba[0], gla_norm_g[0], rwkv_mu[0],
                  rwkv_w0[0], rwkv_w2[0], rwkv_a0[0], rwkv_a2[0], rwkv_g2[0], rwkv_k_k[0], rwkv_k_a[0],
                  rwkv_r_k[0], rwkv_ln_g[0], rwkv_ln_b[0], w_out[0], norm_mem_x_g[0], norm_mem_g[0],
                  wq_mem[0], wkv_mem[0], wo_mem[0], norm_ffn_g[0], w_gate_up[0], w_down[0], norm_final_g)
```

```python
import functools
import math

import jax
import jax.numpy as jnp
from jax import lax
from jax.experimental import pallas as pl
from jax.experimental.pallas import tpu as pltpu

F32 = jnp.float32
BF16 = jnp.bfloat16

MEM_HEADS = 4
GLA_HEADS = 4
GLA_DK = 64
GLA_DV = 128
GLA_GATE_RANK = 16
GLA_GATE_NORMALIZER = 16.0
GLA_NORM_EPS = 1e-5
RWKV_HEAD = 64
RWKV_DECAY_RANK = 64
RWKV_AAA_RANK = 64
RWKV_GATE_RANK = 128
RWKV_LN_EPS = 64e-5
NORM_EPS = 1e-6

CHUNK = 64
GLA_CHUNKS_PER_ITER = 2
SUBTILE_ROWS = 256
ROWS_IN_PROJ = 512
ROWS_GLA = 1024
ROWS_RWKV = 512
ROWS_POST_ATTN = 1024
ROWS_FFN = 1024
FF_CHUNK = 256
LANES = 128
VMEM_LIMIT = 56 * 1024 * 1024


def _params(n_grid_dims):
    return pltpu.CompilerParams(dimension_semantics=("arbitrary",) * n_grid_dims,
                                vmem_limit_bytes=VMEM_LIMIT)


def _const_spec(shape, grid_rank):
    zeros = (0,) * len(shape)
    index_map = (lambda j: zeros) if grid_rank == 1 else (lambda i, j: zeros)
    return pl.BlockSpec(shape, index_map, pipeline_mode=pl.Buffered(1))


def _dot(a, b):
    return jnp.dot(a.astype(BF16), b.astype(BF16), preferred_element_type=F32)


def _rms_norm(x, g, eps):
    return x * lax.rsqrt(jnp.mean(x * x, axis=-1, keepdims=True) + eps) * g


def _sigmoid(x):
    return 1.0 / (1.0 + jnp.exp(-x))


def _log_sigmoid(x):
    return jnp.minimum(x, 0.0) - jnp.log1p(jnp.exp(-jnp.abs(x)))


def _iota2(shape, dim):
    return lax.broadcasted_iota(jnp.int32, shape, dim)


def _cumsum_operator(c):
    return (_iota2((c, 3 * c), 0) >= _iota2((c, 3 * c), 1) % c).astype(BF16)


def _chunk_cumsum(tri3, x):
    p1 = x.astype(BF16)
    rem = x - p1.astype(F32)
    p2 = rem.astype(BF16)
    p3 = (rem - p2.astype(F32)).astype(BF16)
    return jnp.dot(tri3, jnp.concatenate([p1, p2, p3], axis=0), preferred_element_type=F32)


def _mem_kv_kernel(mem_ref, g_ref, w_ref, k_ref, v_ref):
    d = k_ref.shape[-1]
    m = _rms_norm(mem_ref[...], g_ref[...], NORM_EPS)
    kv = _dot(m, w_ref[...])
    k_ref[...] = kv[:, :d].astype(BF16)
    v_ref[...] = kv[:, d:].astype(BF16)


def _mem_kv(mem2d, g, wkv):
    n, d = mem2d.shape
    return pl.pallas_call(
        _mem_kv_kernel,
        out_shape=(jax.ShapeDtypeStruct((n, d), BF16), jax.ShapeDtypeStruct((n, d), BF16)),
        compiler_params=pltpu.CompilerParams(vmem_limit_bytes=VMEM_LIMIT),
        name="mem_kv",
    )(mem2d, g, wkv)


def _in_proj_kernel(x_ref, g_ref, wg_ref, wr_ref, wa2_ref, ba_ref, mu_ref, zg_ref, zr_ref, prev_ref):
    tm = x_ref.shape[0]
    n_main = zg_ref.shape[-1] - wa2_ref.shape[-1]

    @pl.when(pl.program_id(1) == 0)
    def _():
        prev_ref[...] = jnp.zeros_like(prev_ref)

    sub = min(SUBTILE_ROWS, tm)
    prev = prev_ref[0:1, :]
    for s in range(tm // sub):
        rows = slice(s * sub, (s + 1) * sub)
        h = _rms_norm(x_ref[rows, :], g_ref[...], NORM_EPS).astype(BF16)

        zg = jnp.dot(h, wg_ref[...], preferred_element_type=F32)
        zg_ref[rows, :n_main] = zg[:, :n_main]
        pre = _dot(zg[:, n_main:], wa2_ref[...]) + ba_ref[...]
        zg_ref[rows, n_main:] = _log_sigmoid(pre) * (1.0 / GLA_GATE_NORMALIZER)

        zr = jnp.dot(h, wr_ref[...], preferred_element_type=F32)
        rolled = pltpu.roll(zr, shift=1, axis=0)
        first = _iota2(zr.shape, 0) == 0
        shifted = jnp.where(first, jnp.broadcast_to(prev, zr.shape), rolled)
        zr_ref[rows, :] = zr + (shifted - zr) * mu_ref[...]
        prev = zr[sub - 1:sub, :]
    prev_ref[0:1, :] = prev


def _in_proj(x, g, w_gla, w_rwkv, wa2p, ba, mu, tm):
    b, t, d = x.shape
    ng = w_gla.shape[1] - LANES + wa2p.shape[1]
    nr = w_rwkv.shape[1]
    const = lambda *shape: _const_spec(shape, 2)
    return pl.pallas_call(
        _in_proj_kernel,
        grid=(b, t // tm),
        in_specs=[
            pl.BlockSpec((None, tm, d), lambda i, j: (i, j, 0)),
            const(1, d), const(*w_gla.shape), const(*w_rwkv.shape), const(*wa2p.shape),
            const(1, wa2p.shape[1]), const(1, nr),
        ],
        out_specs=(pl.BlockSpec((None, tm, ng), lambda i, j: (i, j, 0)),
                   pl.BlockSpec((None, tm, nr), lambda i, j: (i, j, 0))),
        out_shape=(jax.ShapeDtypeStruct((b, t, ng), F32), jax.ShapeDtypeStruct((b, t, nr), F32)),
        scratch_shapes=[pltpu.VMEM((8, nr), F32)],
        compiler_params=_params(2),
        name="in_proj",
    )(x, g, w_gla, w_rwkv, wa2p, ba, mu)


def _gla_kernel(z_ref, ng_ref, o_ref, s_ref):
    nb, tt, _ = z_ref.shape
    kw = GLA_HEADS * GLA_DK
    vw = GLA_HEADS * GLA_DV
    c = CHUNK
    pairs = GLA_HEADS // 2
    pk = 2 * GLA_DK
    pv = 2 * GLA_DV
    cpi = GLA_CHUNKS_PER_ITER

    @pl.when(pl.program_id(0) == 0)
    def _():
        s_ref[...] = jnp.zeros_like(s_ref)

    tri3 = _cumsum_operator(c)
    causal = _iota2((c, pk), 0) >= _iota2((c, pk), 1) % c
    lane_lo = _iota2((c, pk), 1) < GLA_DK
    v_lo = _iota2((c, pv), 1) < GLA_DV
    s_mask = (_iota2((pv, pk), 0) // GLA_DV) == (_iota2((pv, pk), 1) // GLA_DK)
    norm_g = ng_ref[...]
    nt_dims = (((1,), (1,)), ((), ()))
    tn_dims = (((0,), (0,)), ((), ()))

    def step(it, carry):
        units = [(b, j, p) for b in range(nb) for j in range(cpi) for p in range(pairs)]
        rows = {(b, j): pl.ds(pl.multiple_of((it * cpi + j) * c, c), c) for b in range(nb) for j in range(cpi)}
        q_dec, k_inv, k_tail, dec, vv = {}, {}, {}, {}, {}
        for (b, j), rs in rows.items():
            q = z_ref[b, rs, 0:kw] * (GLA_DK ** -0.5)
            k = z_ref[b, rs, kw:2 * kw]
            log_a = z_ref[b, rs, 2 * kw + 2 * vw:3 * kw + 2 * vw]
            bcum = _chunk_cumsum(tri3, log_a)
            blast = bcum[c - 1:c, :]
            qd = q * jnp.exp(bcum)
            ki = k * jnp.exp(-bcum)
            kt = k * jnp.exp(blast - bcum)
            dc = jnp.exp(blast)
            for p in range(pairs):
                ks = slice(p * pk, (p + 1) * pk)
                q_dec[b, j, p] = qd[:, ks].astype(BF16)
                k_inv[b, j, p] = ki[:, ks]
                k_tail[b, j, p] = kt[:, ks].astype(BF16)
                dec[b, j, p] = dc[:, ks]
                vv[b, j, p] = z_ref[b, rs, 2 * kw + p * pv:2 * kw + (p + 1) * pv].astype(BF16)

        o_intra, d_state = {}, {}
        for u in units:
            scores = jnp.where(causal, lax.dot_general(q_dec[u], _bd(k_inv[u], lane_lo), nt_dims,
                                                       preferred_element_type=F32), 0.0)
            zero = jnp.zeros_like(vv[u])
            v_bd = jnp.concatenate([jnp.where(v_lo, vv[u], zero), jnp.where(v_lo, zero, vv[u])], axis=0)
            o_intra[u] = _mm1(scores, v_bd)
            d_state[u] = jnp.where(s_mask, lax.dot_general(vv[u], k_tail[u], tn_dims,
                                                           preferred_element_type=F32), 0.0)

        for b in range(nb):
            s_cur = [s_ref[b * pairs + p] for p in range(pairs)]
            for j in range(cpi):
                outs = []
                for p in range(pairs):
                    u = (b, j, p)
                    o = o_intra[u] + lax.dot_general(q_dec[u], s_cur[p].astype(BF16), nt_dims,
                                                     preferred_element_type=F32)
                    s_cur[p] = s_cur[p] * dec[u] + d_state[u]
                    for h in range(2):
                        oh = o[:, h * GLA_DV:(h + 1) * GLA_DV]
                        oh = oh * lax.rsqrt(jnp.mean(oh * oh, axis=-1, keepdims=True) + GLA_NORM_EPS) * norm_g
                        g0 = 2 * kw + vw + (2 * p + h) * GLA_DV
                        gh = z_ref[b, rows[b, j], g0:g0 + GLA_DV]
                        outs.append(oh * (gh * _sigmoid(gh)))
                o_ref[b, rows[b, j], :] = jnp.concatenate(outs, axis=-1).astype(o_ref.dtype)
            for p in range(pairs):
                s_ref[b * pairs + p] = s_cur[p]
        return carry

    lax.fori_loop(0, tt // (c * cpi), step, 0)


def _gla(zg, norm_g, tt):
    b, t, n = zg.shape
    vw = GLA_HEADS * GLA_DV
    return pl.pallas_call(
        _gla_kernel,
        grid=(t // tt,),
        in_specs=[pl.BlockSpec((b, tt, n), lambda j: (0, j, 0)),
                  _const_spec((1, GLA_DV), 1)],
        out_specs=pl.BlockSpec((b, tt, vw), lambda j: (0, j, 0)),
        out_shape=jax.ShapeDtypeStruct((b, t, vw), BF16),
        scratch_shapes=[pltpu.VMEM((b * GLA_HEADS // 2, 2 * GLA_DV, 2 * GLA_DK), F32)],
        compiler_params=_params(1),
        name="gla",
    )(zg, norm_g)


def _seg_sum(x, seg_ones):
    xb = x.astype(BF16)
    blk = seg_ones.shape[0]
    parts = []
    for s in range(x.shape[1] // blk):
        sl = slice(s * blk, (s + 1) * blk)
        parts.append(jnp.dot(xb[:, sl], seg_ones, preferred_element_type=F32))
    return jnp.concatenate(parts, axis=-1)


def _block_diag(z, lane_lo):
    zero = jnp.zeros_like(z)
    return jnp.concatenate([jnp.where(lane_lo, z, zero), jnp.where(lane_lo, zero, z)], axis=0)


def _bd(z, lane_lo):
    return _block_diag(z.astype(BF16), lane_lo)


def _mm1(x, r):
    return jnp.dot(x.astype(BF16), r, preferred_element_type=F32)


def _rwkv_kernel(z_ref, w0_ref, w2_ref, a0_ref, a2_ref, g2_ref, kk_ref, ka_ref, rk_ref, lng_ref, lnb_ref,
                 o_ref, s_ref, st_ref, y_ref):
    nb, tt, _ = z_ref.shape
    w = w0_ref.shape[-1]
    n = RWKV_HEAD
    pairs = w // LANES
    c = CHUNK
    o_wl = 3 * w
    o_al = o_wl + RWKV_DECAY_RANK
    o_gl = o_al + RWKV_AAA_RANK

    @pl.when(pl.program_id(0) == 0)
    def _():
        s_ref[...] = jnp.zeros_like(s_ref)

    seg = 2 * LANES
    seg_ones = (_iota2((seg, seg), 0) // n == _iota2((seg, seg), 1) // n).astype(BF16)

    cols = lambda lo, hi: z_ref[:, :, lo:hi].reshape(nb * tt, hi - lo)
    r = cols(0, w)
    k = cols(w, 2 * w)
    v = cols(2 * w, 3 * w)
    u = w0_ref[...] + _dot(jnp.tanh(cols(o_wl, o_al)), w2_ref[...])
    lw = -math.exp(-0.5) * _sigmoid(u)
    a = _sigmoid(a0_ref[...] + _dot(cols(o_al, o_gl), a2_ref[...]))
    kk = k * kk_ref[...]
    kk = kk * lax.rsqrt(jnp.maximum(_seg_sum(kk * kk, seg_ones), 1e-24))
    k2 = k * (1.0 + (a - 1.0) * ka_ref[...])
    st_ref[0] = r
    st_ref[1] = lw
    st_ref[2] = k2
    st_ref[3] = v
    st_ref[4] = -kk
    st_ref[5] = kk * a

    tri3 = _cumsum_operator(c)
    row = _iota2((c, LANES), 0)
    col = _iota2((c, LANES), 1) % c
    lower = row >= col
    strict = row > col
    eye = (row == col).astype(F32)
    blk16 = (row // 16) == (col // 16)
    blk32 = (row // 32) == (col // 32)
    off16 = jnp.logical_and(blk32, jnp.logical_not(blk16))
    lane_lo = _iota2((c, LANES), 1) < n
    bd_mask = (_iota2((LANES, LANES), 0) // n) == (_iota2((LANES, LANES), 1) // n)
    units = [(b, p) for b in range(nb) for p in range(pairs)]
    prange = range(len(units))
    nt_dims = (((1,), (1,)), ((), ()))

    def chunk(ci, carry):
        rows = [pl.ds(pl.multiple_of(b * tt + ci * c, c), c) for b in range(nb)]
        a_t, r_t, b_t, k_t, b_h, k_h, vc, g_c = [], [], [], [], [], [], [], []
        for b in range(nb):
            rc = st_ref[0, rows[b], :]
            lwc = st_ref[1, rows[b], :]
            kc = st_ref[2, rows[b], :]
            ac = st_ref[4, rows[b], :]
            bc = st_ref[5, rows[b], :]
            cum = _chunk_cumsum(tri3, lwc)
            last = cum[c - 1:c, :]
            e_neg = jnp.exp(-cum)
            e_tail = jnp.exp(last - cum)
            r_tb = rc * jnp.exp(cum)
            a_tb = ac * jnp.exp(cum - lwc)
            vb = st_ref[3, rows[b], :]
            for p in range(pairs):
                sl = slice(p * LANES, (p + 1) * LANES)
                a_t.append(a_tb[:, sl])
                r_t.append(r_tb[:, sl])
                b_t.append(bc[:, sl] * e_neg[:, sl])
                k_t.append(kc[:, sl] * e_neg[:, sl])
                b_h.append(bc[:, sl] * e_tail[:, sl])
                k_h.append(kc[:, sl] * e_tail[:, sl])
                vc.append(vb[:, sl])
                g_c.append(jnp.exp(last[:, sl]))

        a_ab, a_ak, q_b, q_k = [], [], [], []
        for p in prange:
            lhs = jnp.concatenate([a_t[p], r_t[p]], axis=0).astype(BF16)
            rhs = jnp.concatenate([_bd(b_t[p], lane_lo), _bd(k_t[p], lane_lo)], axis=0)
            aa = lax.dot_general(lhs, rhs, nt_dims, preferred_element_type=F32)
            a_ab.append(jnp.where(strict, aa[:c, :LANES], 0.0))
            a_ak.append(jnp.where(strict, aa[:c, LANES:], 0.0))
            q_b.append(jnp.where(lower, aa[c:, :LANES], 0.0))
            q_k.append(jnp.where(lower, aa[c:, LANES:], 0.0))

        ad = [jnp.where(blk16, x, 0.0) for x in a_ab]
        pw = [_mm1(ad[p], _bd(ad[p], lane_lo)) for p in prange]
        t = [eye + ad[p] for p in prange]
        for _ in range(2):
            both = [_mm1(jnp.concatenate([pw[p], t[p]], axis=0), _bd(pw[p], lane_lo)) for p in prange]
            pw = [x[:c] for x in both]
            t = [t[p] + both[p][c:] for p in prange]
        t = [t[p] + _mm1(t[p], _bd(pw[p], lane_lo)) for p in prange]
        for msk in (off16, jnp.logical_not(blk32)):
            te = [_mm1(t[p], _bd(jnp.where(msk, a_ab[p], 0.0), lane_lo)) for p in prange]
            t = [t[p] + _mm1(te[p], _bd(t[p], lane_lo)) for p in prange]

        v_r = [_bd(vc[p], lane_lo) for p in prange]
        pv = [_mm1(a_ak[p], v_r[p]) for p in prange]
        wu = [_mm1(t[p], jnp.concatenate([_bd(a_t[p], lane_lo), _bd(pv[p], lane_lo)], axis=1))
              for p in prange]

        for p in prange:
            bi, pi = units[p]
            s0 = s_ref[p]
            ws = lax.dot_general(jnp.concatenate([wu[p][:, :LANES], r_t[p]], axis=0).astype(BF16),
                                 s0.astype(BF16), nt_dims, preferred_element_type=F32)
            u_mat = ws[:c] + wu[p][:, LANES:]
            y = ws[c:] + _mm1(jnp.concatenate([q_b[p], q_k[p]], axis=1),
                              jnp.concatenate([_bd(u_mat, lane_lo), v_r[p]], axis=0))
            y_ref[rows[bi], pi * LANES:(pi + 1) * LANES] = y
            ds = _mm1(jnp.concatenate([u_mat, vc[p]], axis=0).T,
                      jnp.concatenate([b_h[p], k_h[p]], axis=0).astype(BF16))
            s_ref[p] = s0 * g_c[p] + jnp.where(bd_mask, ds, 0.0)
        return carry

    lax.fori_loop(0, tt // c, chunk, 0)

    y = y_ref[...]
    mean = _seg_sum(y, seg_ones) * (1.0 / n)
    dlt = y - mean
    var = _seg_sum(dlt * dlt, seg_ones) * (1.0 / n)
    yn = dlt * lax.rsqrt(var + RWKV_LN_EPS) * lng_ref[...] + lnb_ref[...]
    bonus = _seg_sum(st_ref[0] * st_ref[2] * rk_ref[...], seg_ones) * st_ref[3]
    gate = _dot(_sigmoid(cols(o_gl, o_gl + RWKV_GATE_RANK)), g2_ref[...])
    o_ref[...] = ((yn + bonus) * gate).reshape(nb, tt, w).astype(o_ref.dtype)


def _rwkv(zr, w0, w2, a0, a2, g2, k_k, k_a, r_k, ln_g, ln_b, tt):
    b, t, nz = zr.shape
    w = w0.shape[-1]
    const = lambda *shape: _const_spec(shape, 1)
    return pl.pallas_call(
        _rwkv_kernel,
        grid=(t // tt,),
        in_specs=[pl.BlockSpec((b, tt, nz), lambda j: (0, j, 0)),
                  const(1, w), const(*w2.shape), const(1, w), const(*a2.shape), const(*g2.shape),
                  const(1, w), const(1, w), const(1, w), const(1, w), const(1, w)],
        out_specs=pl.BlockSpec((b, tt, w), lambda j: (0, j, 0)),
        out_shape=jax.ShapeDtypeStruct((b, t, w), BF16),
        scratch_shapes=[pltpu.VMEM((b * w // LANES, LANES, LANES), F32),
                        pltpu.VMEM((6, b * tt, w), F32),
                        pltpu.VMEM((b * tt, w), F32)],
        compiler_params=_params(1),
        name="rwkv",
    )(zr, w0, w2, a0, a2, g2, k_k, k_a, r_k, ln_g, ln_b)


def _post_attn_kernel(x_ref, og_ref, or_ref, wog_ref, wor_ref, gx_ref, wq_ref, k_ref, v_ref, wo_ref, out_ref):
    d = x_ref.shape[-1]
    hd = d // MEM_HEADS
    x1 = (x_ref[...] + jnp.dot(og_ref[...], wog_ref[...], preferred_element_type=F32)
          + jnp.dot(or_ref[...], wor_ref[...], preferred_element_type=F32))
    q = _dot(_rms_norm(x1, gx_ref[...], NORM_EPS), wq_ref[...]).astype(BF16)
    heads = []
    for h in range(MEM_HEADS):
        hs = slice(h * hd, (h + 1) * hd)
        s = lax.dot_general(q[:, hs], k_ref[:, hs], (((1,), (1,)), ((), ())),
                            preferred_element_type=F32) * (hd ** -0.5)
        e = jnp.exp(s - jnp.max(s, axis=-1, keepdims=True))
        p = e / jnp.sum(e, axis=-1, keepdims=True)
        heads.append(_dot(p, v_ref[:, hs]))
    o = jnp.concatenate(heads, axis=-1)
    out_ref[...] = x1 + _dot(o, wo_ref[...])


def _post_attn(x, o_gla, o_rwkv, wo_g, wo_r, gx, wq, kmem, vmem, wo, tm):
    b, t, d = x.shape
    m = kmem.shape[1]
    const = lambda *shape: _const_spec(shape, 2)
    tile = lambda width: pl.BlockSpec((None, tm, width), lambda i, j: (i, j, 0))
    return pl.pallas_call(
        _post_attn_kernel,
        grid=(b, t // tm),
        in_specs=[tile(d), tile(o_gla.shape[-1]), tile(o_rwkv.shape[-1]),
                  const(*wo_g.shape), const(*wo_r.shape), const(1, d), const(*wq.shape),
                  pl.BlockSpec((None, m, d), lambda i, j: (i, 0, 0)),
                  pl.BlockSpec((None, m, d), lambda i, j: (i, 0, 0)),
                  const(*wo.shape)],
        out_specs=tile(d),
        out_shape=jax.ShapeDtypeStruct((b, t, d), F32),
        compiler_params=_params(2),
        name="post_attn",
    )(x, o_gla, o_rwkv, wo_g, wo_r, gx, wq, kmem, vmem, wo)


def _ffn_kernel(x_ref, g_ref, wgu_ref, wd_ref, gf_ref, out_ref, *, ff_chunk):
    x = x_ref[...]
    d_ff = wd_ref.shape[0]
    h = _rms_norm(x, g_ref[...], NORM_EPS).astype(BF16)
    acc = x
    for c0 in range(0, d_ff, ff_chunk):
        gate = jnp.dot(h, wgu_ref[:, c0:c0 + ff_chunk], preferred_element_type=F32)
        up = jnp.dot(h, wgu_ref[:, d_ff + c0:d_ff + c0 + ff_chunk], preferred_element_type=F32)
        act = (gate * _sigmoid(gate) * up).astype(BF16)
        acc = acc + jnp.dot(act, wd_ref[c0:c0 + ff_chunk, :], preferred_element_type=F32)
    out_ref[...] = _rms_norm(acc, gf_ref[...], NORM_EPS)


def _ffn(x, g, wgu, wd, gf, tm, ff_chunk):
    b, t, d = x.shape
    const = lambda *shape: _const_spec(shape, 2)
    tile = pl.BlockSpec((None, tm, d), lambda i, j: (i, j, 0))
    return pl.pallas_call(
        functools.partial(_ffn_kernel, ff_chunk=ff_chunk),
        grid=(b, t // tm),
        in_specs=[tile, const(1, d), const(*wgu.shape), const(*wd.shape), const(1, d)],
        out_specs=tile,
        out_shape=jax.ShapeDtypeStruct((b, t, d), F32),
        compiler_params=_params(2),
        name="ffn",
    )(x, g, wgu, wd, gf)


def _layer(x, mem, norm_mix_g, w_in, gla_wa2, gla_ba, gla_norm_g, rwkv_mu, rwkv_w0, rwkv_w2, rwkv_a0,
           rwkv_a2, rwkv_g2, rwkv_k_k, rwkv_k_a, rwkv_r_k, rwkv_ln_g, rwkv_ln_b, w_out, norm_mem_x_g,
           norm_mem_g, wq_mem, wkv_mem, wo_mem, norm_ffn_g, w_gate_up, w_down, final_g):
    b, t, d = x.shape
    m = mem.shape[1]
    row = lambda p: p.reshape(1, -1).astype(F32)
    kw = GLA_HEADS * GLA_DK
    vw = GLA_HEADS * GLA_DV
    n_gla_main = 2 * kw + 2 * vw
    n_gla = n_gla_main + GLA_GATE_RANK
    pad = LANES - GLA_GATE_RANK
    w_gla = jnp.pad(w_in[:, :n_gla], ((0, 0), (0, pad))).astype(BF16)
    w_rwkv = w_in[:, n_gla:].astype(BF16)
    wa2p = jnp.pad(gla_wa2, ((0, pad), (0, 0))).astype(BF16)

    kmem, vmem = _mem_kv(mem.reshape(b * m, d), row(norm_mem_g), wkv_mem.astype(BF16))
    zg, zr = _in_proj(x, row(norm_mix_g), w_gla, w_rwkv, wa2p, row(gla_ba), row(rwkv_mu),
                      tm=min(ROWS_IN_PROJ, t))
    o_gla = _gla(zg, row(gla_norm_g), tt=min(ROWS_GLA, t))
    o_rwkv = _rwkv(zr, row(rwkv_w0), rwkv_w2.astype(BF16), row(rwkv_a0), rwkv_a2.astype(BF16),
                   rwkv_g2.astype(BF16), row(rwkv_k_k), row(rwkv_k_a), row(rwkv_r_k), row(rwkv_ln_g),
                   row(rwkv_ln_b), tt=min(ROWS_RWKV, t))
    w_out_b = w_out.astype(BF16)
    x2 = _post_attn(x, o_gla, o_rwkv, w_out_b[:vw], w_out_b[vw:], row(norm_mem_x_g), wq_mem.astype(BF16),
                    kmem.reshape(b, m, d), vmem.reshape(b, m, d), wo_mem.astype(BF16),
                    tm=min(ROWS_POST_ATTN, t))
    return _ffn(x2, row(norm_ffn_g), w_gate_up.astype(BF16), w_down.astype(BF16), row(final_g),
                tm=min(ROWS_FFN, t), ff_chunk=FF_CHUNK)


def kernel(x, mem, norm_mix_g, w_in, gla_wa2, gla_ba, gla_norm_g, rwkv_mu, rwkv_w0, rwkv_w2, rwkv_a0, rwkv_a2, rwkv_g2, rwkv_k_k, rwkv_k_a, rwkv_r_k, rwkv_ln_g, rwkv_ln_b, w_out, norm_mem_x_g, norm_mem_g, wq_mem, wkv_mem, wo_mem, norm_ffn_g, w_gate_up, w_down, norm_final_g):
    assert norm_mix_g.shape[0] == 1, "single-layer block"
    return _layer(x, mem, norm_mix_g[0], w_in[0], gla_wa2[0], gla_ba[0], gla_norm_g[0], rwkv_mu[0],
                  rwkv_w0[0], rwkv_w2[0], rwkv_a0[0], rwkv_a2[0], rwkv_g2[0], rwkv_k_k[0], rwkv_k_a[0],
                  rwkv_r_k[0], rwkv_ln_g[0], rwkv_ln_b[0], w_out[0], norm_mem_x_g[0], norm_mem_g[0],
                  wq_mem[0], wkv_mem[0], wo_mem[0], norm_ffn_g[0], w_gate_up[0], w_down[0], norm_final_g)
```

```python
import functools
import math

import jax
import jax.numpy as jnp
from jax import lax
from jax.experimental import pallas as pl
from jax.experimental.pallas import tpu as pltpu

F32 = jnp.float32
BF16 = jnp.bfloat16

MEM_HEADS = 4
GLA_HEADS = 4
GLA_DK = 64
GLA_DV = 128
GLA_GATE_RANK = 16
GLA_GATE_NORMALIZER = 16.0
GLA_NORM_EPS = 1e-5
RWKV_HEAD = 64
RWKV_DECAY_RANK = 64
RWKV_AAA_RANK = 64
RWKV_GATE_RANK = 128
RWKV_LN_EPS = 64e-5
NORM_EPS = 1e-6

CHUNK = 64
GLA_CHUNKS_PER_ITER = 2
SUBTILE_ROWS = 256
ROWS_IN_PROJ = 512
ROWS_GLA = 1024
ROWS_RWKV = 512
ROWS_POST_ATTN = 1024
ROWS_FFN = 512
FF_CHUNK = 256
LANES = 128
VMEM_LIMIT = 56 * 1024 * 1024


def _params(n_grid_dims):
    return pltpu.CompilerParams(dimension_semantics=("arbitrary",) * n_grid_dims,
                                vmem_limit_bytes=VMEM_LIMIT)


def _const_spec(shape, grid_rank):
    zeros = (0,) * len(shape)
    index_map = (lambda j: zeros) if grid_rank == 1 else (lambda i, j: zeros)
    return pl.BlockSpec(shape, index_map, pipeline_mode=pl.Buffered(1))


def _dot(a, b):
    return jnp.dot(a.astype(BF16), b.astype(BF16), preferred_element_type=F32)


def _rms_norm(x, g, eps):
    return x * lax.rsqrt(jnp.mean(x * x, axis=-1, keepdims=True) + eps) * g


def _sigmoid(x):
    return 1.0 / (1.0 + jnp.exp(-x))


def _log_sigmoid(x):
    return jnp.minimum(x, 0.0) - jnp.log1p(jnp.exp(-jnp.abs(x)))


def _iota2(shape, dim):
    return lax.broadcasted_iota(jnp.int32, shape, dim)


def _cumsum_operator(c):
    return (_iota2((c, 3 * c), 0) >= _iota2((c, 3 * c), 1) % c).astype(BF16)


def _chunk_cumsum(tri3, x):
    p1 = x.astype(BF16)
    rem = x - p1.astype(F32)
    p2 = rem.astype(BF16)
    p3 = (rem - p2.astype(F32)).astype(BF16)
    return jnp.dot(tri3, jnp.concatenate([p1, p2, p3], axis=0), preferred_element_type=F32)


def _mem_kv_kernel(mem_ref, g_ref, w_ref, k_ref, v_ref):
    d = k_ref.shape[-1]
    m = _rms_norm(mem_ref[...], g_ref[...], NORM_EPS)
    kv = _dot(m, w_ref[...])
    k_ref[...] = kv[:, :d].astype(BF16)
    v_ref[...] = kv[:, d:].astype(BF16)


def _mem_kv(mem2d, g, wkv):
    n, d = mem2d.shape
    return pl.pallas_call(
        _mem_kv_kernel,
        out_shape=(jax.ShapeDtypeStruct((n, d), BF16), jax.ShapeDtypeStruct((n, d), BF16)),
        compiler_params=pltpu.CompilerParams(vmem_limit_bytes=VMEM_LIMIT),
        name="mem_kv",
    )(mem2d, g, wkv)


def _in_proj_kernel(x_ref, g_ref, w_ref, wa2_ref, ba_ref, mu_ref, zg_ref, zr_ref, prev_ref, wg_ref, wr_ref):
    tm = x_ref.shape[0]
    n_main = zg_ref.shape[-1] - wa2_ref.shape[-1]
    n_gla = n_main + GLA_GATE_RANK

    @pl.when(jnp.logical_and(pl.program_id(0) == 0, pl.program_id(1) == 0))
    def _():
        blk = LANES
        for r0 in range(0, w_ref.shape[0], blk):
            wg_ref[r0:r0 + blk, :] = w_ref[r0:r0 + blk, :wg_ref.shape[1]].astype(BF16)
            wr_ref[r0:r0 + blk, :] = w_ref[r0:r0 + blk, n_gla:].astype(BF16)

    @pl.when(pl.program_id(1) == 0)
    def _():
        prev_ref[...] = jnp.zeros_like(prev_ref)

    sub = min(SUBTILE_ROWS, tm)
    prev = prev_ref[0:1, :]
    for s in range(tm // sub):
        rows = slice(s * sub, (s + 1) * sub)
        h = _rms_norm(x_ref[rows, :], g_ref[...], NORM_EPS).astype(BF16)

        zg = jnp.dot(h, wg_ref[...], preferred_element_type=F32)
        zg_ref[rows, :n_main] = zg[:, :n_main]
        pre = _dot(zg[:, n_main:], wa2_ref[...]) + ba_ref[...]
        zg_ref[rows, n_main:] = _log_sigmoid(pre) * (1.0 / GLA_GATE_NORMALIZER)

        zr = jnp.dot(h, wr_ref[...], preferred_element_type=F32)
        rolled = pltpu.roll(zr, shift=1, axis=0)
        first = _iota2(zr.shape, 0) == 0
        shifted = jnp.where(first, jnp.broadcast_to(prev, zr.shape), rolled)
        zr_ref[rows, :] = zr + (shifted - zr) * mu_ref[...]
        prev = zr[sub - 1:sub, :]
    prev_ref[0:1, :] = prev


def _in_proj(x, g, w_in, wa2p, ba, mu, tm):
    b, t, d = x.shape
    nr = mu.shape[1]
    n_main = w_in.shape[1] - nr - GLA_GATE_RANK
    ng = n_main + wa2p.shape[1]
    const = lambda *shape: _const_spec(shape, 2)
    return pl.pallas_call(
        _in_proj_kernel,
        grid=(b, t // tm),
        in_specs=[
            pl.BlockSpec((None, tm, d), lambda i, j: (i, j, 0)),
            const(1, d), const(*w_in.shape), const(*wa2p.shape),
            const(1, wa2p.shape[1]), const(1, nr),
        ],
        out_specs=(pl.BlockSpec((None, tm, ng), lambda i, j: (i, j, 0)),
                   pl.BlockSpec((None, tm, nr), lambda i, j: (i, j, 0))),
        out_shape=(jax.ShapeDtypeStruct((b, t, ng), F32), jax.ShapeDtypeStruct((b, t, nr), F32)),
        scratch_shapes=[pltpu.VMEM((8, nr), F32),
                        pltpu.VMEM((d, n_main + LANES), BF16),
                        pltpu.VMEM((d, nr), BF16)],
        compiler_params=_params(2),
        name="in_proj",
    )(x, g, w_in, wa2p, ba, mu)


def _gla_kernel(z_ref, ng_ref, o_ref, s_ref):
    nb, tt, _ = z_ref.shape
    kw = GLA_HEADS * GLA_DK
    vw = GLA_HEADS * GLA_DV
    c = CHUNK
    pairs = GLA_HEADS // 2
    pk = 2 * GLA_DK
    pv = 2 * GLA_DV
    cpi = GLA_CHUNKS_PER_ITER

    @pl.when(pl.program_id(0) == 0)
    def _():
        s_ref[...] = jnp.zeros_like(s_ref)

    tri3 = _cumsum_operator(c)
    causal = _iota2((c, pk), 0) >= _iota2((c, pk), 1) % c
    lane_lo = _iota2((c, pk), 1) < GLA_DK
    v_lo = _iota2((c, pv), 1) < GLA_DV
    s_mask = (_iota2((pv, pk), 0) // GLA_DV) == (_iota2((pv, pk), 1) // GLA_DK)
    norm_g = ng_ref[...]
    nt_dims = (((1,), (1,)), ((), ()))
    tn_dims = (((0,), (0,)), ((), ()))

    def step(it, carry):
        units = [(b, j, p) for b in range(nb) for j in range(cpi) for p in range(pairs)]
        rows = {(b, j): pl.ds(pl.multiple_of((it * cpi + j) * c, c), c) for b in range(nb) for j in range(cpi)}
        q_dec, k_inv, k_tail, dec, vv = {}, {}, {}, {}, {}
        for (b, j), rs in rows.items():
            q = z_ref[b, rs, 0:kw] * (GLA_DK ** -0.5)
            k = z_ref[b, rs, kw:2 * kw]
            log_a = z_ref[b, rs, 2 * kw + 2 * vw:3 * kw + 2 * vw]
            bcum = _chunk_cumsum(tri3, log_a)
            blast = bcum[c - 1:c, :]
            qd = q * jnp.exp(bcum)
            ki = k * jnp.exp(-bcum)
            kt = k * jnp.exp(blast - bcum)
            dc = jnp.exp(blast)
            for p in range(pairs):
                ks = slice(p * pk, (p + 1) * pk)
                q_dec[b, j, p] = qd[:, ks].astype(BF16)
                k_inv[b, j, p] = ki[:, ks]
                k_tail[b, j, p] = kt[:, ks].astype(BF16)
                dec[b, j, p] = dc[:, ks]
                vv[b, j, p] = z_ref[b, rs, 2 * kw + p * pv:2 * kw + (p + 1) * pv].astype(BF16)

        o_intra, d_state = {}, {}
        for u in units:
            scores = jnp.where(causal, lax.dot_general(q_dec[u], _bd(k_inv[u], lane_lo), nt_dims,
                                                       preferred_element_type=F32), 0.0)
            zero = jnp.zeros_like(vv[u])
            v_bd = jnp.concatenate([jnp.where(v_lo, vv[u], zero), jnp.where(v_lo, zero, vv[u])], axis=0)
            o_intra[u] = _mm1(scores, v_bd)
            d_state[u] = jnp.where(s_mask, lax.dot_general(vv[u], k_tail[u], tn_dims,
                                                           preferred_element_type=F32), 0.0)

        for b in range(nb):
            s_cur = [s_ref[b * pairs + p] for p in range(pairs)]
            for j in range(cpi):
                outs = []
                for p in range(pairs):
                    u = (b, j, p)
                    o = o_intra[u] + lax.dot_general(q_dec[u], s_cur[p].astype(BF16), nt_dims,
                                                     preferred_element_type=F32)
                    s_cur[p] = s_cur[p] * dec[u] + d_state[u]
                    for h in range(2):
                        oh = o[:, h * GLA_DV:(h + 1) * GLA_DV]
                        oh = oh * lax.rsqrt(jnp.mean(oh * oh, axis=-1, keepdims=True) + GLA_NORM_EPS) * norm_g
                        g0 = 2 * kw + vw + (2 * p + h) * GLA_DV
                        gh = z_ref[b, rows[b, j], g0:g0 + GLA_DV]
                        outs.append(oh * (gh * _sigmoid(gh)))
                o_ref[b, rows[b, j], :] = jnp.concatenate(outs, axis=-1).astype(o_ref.dtype)
            for p in range(pairs):
                s_ref[b * pairs + p] = s_cur[p]
        return carry

    lax.fori_loop(0, tt // (c * cpi), step, 0)


def _gla(zg, norm_g, tt):
    b, t, n = zg.shape
    vw = GLA_HEADS * GLA_DV
    return pl.pallas_call(
        _gla_kernel,
        grid=(t // tt,),
        in_specs=[pl.BlockSpec((b, tt, n), lambda j: (0, j, 0)),
                  _const_spec((1, GLA_DV), 1)],
        out_specs=pl.BlockSpec((b, tt, vw), lambda j: (0, j, 0)),
        out_shape=jax.ShapeDtypeStruct((b, t, vw), BF16),
        scratch_shapes=[pltpu.VMEM((b * GLA_HEADS // 2, 2 * GLA_DV, 2 * GLA_DK), F32)],
        compiler_params=_params(1),
        name="gla",
    )(zg, norm_g)


def _seg_sum(x, seg_ones):
    xb = x.astype(BF16)
    blk = seg_ones.shape[0]
    parts = []
    for s in range(x.shape[1] // blk):
        sl = slice(s * blk, (s + 1) * blk)
        parts.append(jnp.dot(xb[:, sl], seg_ones, preferred_element_type=F32))
    return jnp.concatenate(parts, axis=-1)


def _block_diag(z, lane_lo):
    zero = jnp.zeros_like(z)
    return jnp.concatenate([jnp.where(lane_lo, z, zero), jnp.where(lane_lo, zero, z)], axis=0)


def _bd(z, lane_lo):
    return _block_diag(z.astype(BF16), lane_lo)


def _mm1(x, r):
    return jnp.dot(x.astype(BF16), r, preferred_element_type=F32)


def _rwkv_kernel(z_ref, w0_ref, w2_ref, a0_ref, a2_ref, g2_ref, kk_ref, ka_ref, rk_ref, lng_ref, lnb_ref,
                 o_ref, s_ref, st_ref, y_ref):
    nb, tt, _ = z_ref.shape
    w = w0_ref.shape[-1]
    n = RWKV_HEAD
    pairs = w // LANES
    c = CHUNK
    o_wl = 3 * w
    o_al = o_wl + RWKV_DECAY_RANK
    o_gl = o_al + RWKV_AAA_RANK

    @pl.when(pl.program_id(0) == 0)
    def _():
        s_ref[...] = jnp.zeros_like(s_ref)

    seg = 2 * LANES
    seg_ones = (_iota2((seg, seg), 0) // n == _iota2((seg, seg), 1) // n).astype(BF16)

    cols = lambda lo, hi: z_ref[:, :, lo:hi].reshape(nb * tt, hi - lo)
    r = cols(0, w)
    k = cols(w, 2 * w)
    v = cols(2 * w, 3 * w)
    u = w0_ref[...] + _dot(jnp.tanh(cols(o_wl, o_al)), w2_ref[...])
    lw = -math.exp(-0.5) * _sigmoid(u)
    a = _sigmoid(a0_ref[...] + _dot(cols(o_al, o_gl), a2_ref[...]))
    kk = k * kk_ref[...]
    kk = kk * lax.rsqrt(jnp.maximum(_seg_sum(kk * kk, seg_ones), 1e-24))
    k2 = k * (1.0 + (a - 1.0) * ka_ref[...])
    st_ref[0] = r
    st_ref[1] = lw
    st_ref[2] = k2
    st_ref[3] = v
    st_ref[4] = -kk
    st_ref[5] = kk * a

    tri3 = _cumsum_operator(c)
    row = _iota2((c, LANES), 0)
    col = _iota2((c, LANES), 1) % c
    lower = row >= col
    strict = row > col
    eye = (row == col).astype(F32)
    blk16 = (row // 16) == (col // 16)
    blk32 = (row // 32) == (col // 32)
    off16 = jnp.logical_and(blk32, jnp.logical_not(blk16))
    lane_lo = _iota2((c, LANES), 1) < n
    bd_mask = (_iota2((LANES, LANES), 0) // n) == (_iota2((LANES, LANES), 1) // n)
    units = [(b, p) for b in range(nb) for p in range(pairs)]
    prange = range(len(units))
    nt_dims = (((1,), (1,)), ((), ()))

    def chunk(ci, carry):
        rows = [pl.ds(pl.multiple_of(b * tt + ci * c, c), c) for b in range(nb)]
        a_t, r_t, b_t, k_t, b_h, k_h, vc, g_c = [], [], [], [], [], [], [], []
        for b in range(nb):
            rc = st_ref[0, rows[b], :]
            lwc = st_ref[1, rows[b], :]
            kc = st_ref[2, rows[b], :]
            ac = st_ref[4, rows[b], :]
            bc = st_ref[5, rows[b], :]
            cum = _chunk_cumsum(tri3, lwc)
            last = cum[c - 1:c, :]
            e_neg = jnp.exp(-cum)
            e_tail = jnp.exp(last - cum)
            r_tb = rc * jnp.exp(cum)
            a_tb = ac * jnp.exp(cum - lwc)
            vb = st_ref[3, rows[b], :]
            for p in range(pairs):
                sl = slice(p * LANES, (p + 1) * LANES)
                a_t.append(a_tb[:, sl])
                r_t.append(r_tb[:, sl])
                b_t.append(bc[:, sl] * e_neg[:, sl])
                k_t.append(kc[:, sl] * e_neg[:, sl])
                b_h.append(bc[:, sl] * e_tail[:, sl])
                k_h.append(kc[:, sl] * e_tail[:, sl])
                vc.append(vb[:, sl])
                g_c.append(jnp.exp(last[:, sl]))

        a_ab, a_ak, q_b, q_k = [], [], [], []
        for p in prange:
            lhs = jnp.concatenate([a_t[p], r_t[p]], axis=0).astype(BF16)
            rhs = jnp.concatenate([_bd(b_t[p], lane_lo), _bd(k_t[p], lane_lo)], axis=0)
            aa = lax.dot_general(lhs, rhs, nt_dims, preferred_element_type=F32)
            a_ab.append(jnp.where(strict, aa[:c, :LANES], 0.0))
            a_ak.append(jnp.where(strict, aa[:c, LANES:], 0.0))
            q_b.append(jnp.where(lower, aa[c:, :LANES], 0.0))
            q_k.append(jnp.where(lower, aa[c:, LANES:], 0.0))

        ad = [jnp.where(blk16, x, 0.0) for x in a_ab]
        pw = [_mm1(ad[p], _bd(ad[p], lane_lo)) for p in prange]
        t = [eye + ad[p] for p in prange]
        for _ in range(2):
            both = [_mm1(jnp.concatenate([pw[p], t[p]], axis=0), _bd(pw[p], lane_lo)) for p in prange]
            pw = [x[:c] for x in both]
            t = [t[p] + both[p][c:] for p in prange]
        t = [t[p] + _mm1(t[p], _bd(pw[p], lane_lo)) for p in prange]
        for msk in (off16, jnp.logical_not(blk32)):
            te = [_mm1(t[p], _bd(jnp.where(msk, a_ab[p], 0.0), lane_lo)) for p in prange]
            t = [t[p] + _mm1(te[p], _bd(t[p], lane_lo)) for p in prange]

        v_r = [_bd(vc[p], lane_lo) for p in prange]
        pv = [_mm1(a_ak[p], v_r[p]) for p in prange]
        wu = [_mm1(t[p], jnp.concatenate([_bd(a_t[p], lane_lo), _bd(pv[p], lane_lo)], axis=1))
              for p in prange]

        for p in prange:
            bi, pi = units[p]
            s0 = s_ref[p]
            ws = lax.dot_general(jnp.concatenate([wu[p][:, :LANES], r_t[p]], axis=0).astype(BF16),
                                 s0.astype(BF16), nt_dims, preferred_element_type=F32)
            u_mat = ws[:c] + wu[p][:, LANES:]
            y = ws[c:] + _mm1(jnp.concatenate([q_b[p], q_k[p]], axis=1),
                              jnp.concatenate([_bd(u_mat, lane_lo), v_r[p]], axis=0))
            y_ref[rows[bi], pi * LANES:(pi + 1) * LANES] = y
            ds = _mm1(jnp.concatenate([u_mat, vc[p]], axis=0).T,
                      jnp.concatenate([b_h[p], k_h[p]], axis=0).astype(BF16))
            s_ref[p] = s0 * g_c[p] + jnp.where(bd_mask, ds, 0.0)
        return carry

    lax.fori_loop(0, tt // c, chunk, 0)

    y = y_ref[...]
    mean = _seg_sum(y, seg_ones) * (1.0 / n)
    dlt = y - mean
    var = _seg_sum(dlt * dlt, seg_ones) * (1.0 / n)
    yn = dlt * lax.rsqrt(var + RWKV_LN_EPS) * lng_ref[...] + lnb_ref[...]
    bonus = _seg_sum(st_ref[0] * st_ref[2] * rk_ref[...], seg_ones) * st_ref[3]
    gate = _dot(_sigmoid(cols(o_gl, o_gl + RWKV_GATE_RANK)), g2_ref[...])
    o_ref[...] = ((yn + bonus) * gate).reshape(nb, tt, w).astype(o_ref.dtype)


def _rwkv(zr, w0, w2, a0, a2, g2, k_k, k_a, r_k, ln_g, ln_b, tt):
    b, t, nz = zr.shape
    w = w0.shape[-1]
    const = lambda *shape: _const_spec(shape, 1)
    return pl.pallas_call(
        _rwkv_kernel,
        grid=(t // tt,),
        in_specs=[pl.BlockSpec((b, tt, nz), lambda j: (0, j, 0)),
                  const(1, w), const(*w2.shape), const(1, w), const(*a2.shape), const(*g2.shape),
                  const(1, w), const(1, w), const(1, w), const(1, w), const(1, w)],
        out_specs=pl.BlockSpec((b, tt, w), lambda j: (0, j, 0)),
        out_shape=jax.ShapeDtypeStruct((b, t, w), BF16),
        scratch_shapes=[pltpu.VMEM((b * w // LANES, LANES, LANES), F32),
                        pltpu.VMEM((6, b * tt, w), F32),
                        pltpu.VMEM((b * tt, w), F32)],
        compiler_params=_params(1),
        name="rwkv",
    )(zr, w0, w2, a0, a2, g2, k_k, k_a, r_k, ln_g, ln_b)


def _post_attn_kernel(x_ref, og_ref, or_ref, wog_ref, wor_ref, gx_ref, wq_ref, k_ref, v_ref, wo_ref, out_ref):
    d = x_ref.shape[-1]
    hd = d // MEM_HEADS
    x1 = (x_ref[...] + jnp.dot(og_ref[...], wog_ref[...], preferred_element_type=F32)
          + jnp.dot(or_ref[...], wor_ref[...], preferred_element_type=F32))
    q = _dot(_rms_norm(x1, gx_ref[...], NORM_EPS), wq_ref[...]).astype(BF16)
    heads = []
    for h in range(MEM_HEADS):
        hs = slice(h * hd, (h + 1) * hd)
        s = lax.dot_general(q[:, hs], k_ref[:, hs], (((1,), (1,)), ((), ())),
                            preferred_element_type=F32) * (hd ** -0.5)
        e = jnp.exp(s - jnp.max(s, axis=-1, keepdims=True))
        p = e / jnp.sum(e, axis=-1, keepdims=True)
        heads.append(_dot(p, v_ref[:, hs]))
    o = jnp.concatenate(heads, axis=-1)
    out_ref[...] = x1 + _dot(o, wo_ref[...])


def _post_attn(x, o_gla, o_rwkv, wo_g, wo_r, gx, wq, kmem, vmem, wo, tm):
    b, t, d = x.shape
    m = kmem.shape[1]
    const = lambda *shape: _const_spec(shape, 2)
    tile = lambda width: pl.BlockSpec((None, tm, width), lambda i, j: (i, j, 0))
    return pl.pallas_call(
        _post_attn_kernel,
        grid=(b, t // tm),
        in_specs=[tile(d), tile(o_gla.shape[-1]), tile(o_rwkv.shape[-1]),
                  const(*wo_g.shape), const(*wo_r.shape), const(1, d), const(*wq.shape),
                  pl.BlockSpec((None, m, d), lambda i, j: (i, 0, 0)),
                  pl.BlockSpec((None, m, d), lambda i, j: (i, 0, 0)),
                  const(*wo.shape)],
        out_specs=tile(d),
        out_shape=jax.ShapeDtypeStruct((b, t, d), F32),
        compiler_params=_params(2),
        name="post_attn",
    )(x, o_gla, o_rwkv, wo_g, wo_r, gx, wq, kmem, vmem, wo)


def _ffn_kernel(x_ref, g_ref, wgu_ref, wd_ref, gf_ref, out_ref, *, ff_chunk):
    x = x_ref[...]
    d_ff = wd_ref.shape[0]
    h = _rms_norm(x, g_ref[...], NORM_EPS).astype(BF16)
    acc = x
    for c0 in range(0, d_ff, ff_chunk):
        gate = _dot(h, wgu_ref[:, c0:c0 + ff_chunk])
        up = _dot(h, wgu_ref[:, d_ff + c0:d_ff + c0 + ff_chunk])
        act = gate * _sigmoid(gate) * up
        acc = acc + _dot(act, wd_ref[c0:c0 + ff_chunk, :])
    out_ref[...] = _rms_norm(acc, gf_ref[...], NORM_EPS)


def _ffn(x, g, wgu, wd, gf, tm, ff_chunk):
    b, t, d = x.shape
    const = lambda *shape: _const_spec(shape, 2)
    tile = pl.BlockSpec((None, tm, d), lambda i, j: (i, j, 0))
    return pl.pallas_call(
        functools.partial(_ffn_kernel, ff_chunk=ff_chunk),
        grid=(b, t // tm),
        in_specs=[tile, const(1, d), const(*wgu.shape), const(*wd.shape), const(1, d)],
        out_specs=tile,
        out_shape=jax.ShapeDtypeStruct((b, t, d), F32),
        compiler_params=_params(2),
        name="ffn",
    )(x, g, wgu, wd, gf)


def _layer(x, mem, norm_mix_g, w_in, gla_wa2, gla_ba, gla_norm_g, rwkv_mu, rwkv_w0, rwkv_w2, rwkv_a0,
           rwkv_a2, rwkv_g2, rwkv_k_k, rwkv_k_a, rwkv_r_k, rwkv_ln_g, rwkv_ln_b, w_out, norm_mem_x_g,
           norm_mem_g, wq_mem, wkv_mem, wo_mem, norm_ffn_g, w_gate_up, w_down, final_g):
    b, t, d = x.shape
    m = mem.shape[1]
    row = lambda p: p.reshape(1, -1).astype(F32)
    vw = GLA_HEADS * GLA_DV
    wa2p = jnp.pad(gla_wa2, ((0, LANES - GLA_GATE_RANK), (0, 0))).astype(BF16)

    kmem, vmem = _mem_kv(mem.reshape(b * m, d), row(norm_mem_g), wkv_mem)
    zg, zr = _in_proj(x, row(norm_mix_g), w_in, wa2p, row(gla_ba), row(rwkv_mu), tm=min(ROWS_IN_PROJ, t))
    o_gla = _gla(zg, row(gla_norm_g), tt=min(ROWS_GLA, t))
    o_rwkv = _rwkv(zr, row(rwkv_w0), rwkv_w2.astype(BF16), row(rwkv_a0), rwkv_a2.astype(BF16),
                   rwkv_g2.astype(BF16), row(rwkv_k_k), row(rwkv_k_a), row(rwkv_r_k), row(rwkv_ln_g),
                   row(rwkv_ln_b), tt=min(ROWS_RWKV, t))
    w_out_b = w_out.astype(BF16)
    x2 = _post_attn(x, o_gla, o_rwkv, w_out_b[:vw], w_out_b[vw:], row(norm_mem_x_g), wq_mem.astype(BF16),
                    kmem.reshape(b, m, d), vmem.reshape(b, m, d), wo_mem.astype(BF16),
                    tm=min(ROWS_POST_ATTN, t))
    return _ffn(x2, row(norm_ffn_g), w_gate_up, w_down, row(final_g), tm=min(ROWS_FFN, t), ff_chunk=FF_CHUNK)


def kernel(x, mem, norm_mix_g, w_in, gla_wa2, gla_ba, gla_norm_g, rwkv_mu, rwkv_w0, rwkv_w2, rwkv_a0, rwkv_a2, rwkv_g2, rwkv_k_k, rwkv_k_a, rwkv_r_k, rwkv_ln_g, rwkv_ln_b, w_out, norm_mem_x_g, norm_mem_g, wq_mem, wkv_mem, wo_mem, norm_ffn_g, w_gate_up, w_down, norm_final_g):
    assert norm_mix_g.shape[0] == 1, "single-layer block"
    return _layer(x, mem, norm_mix_g[0], w_in[0], gla_wa2[0], gla_ba[0], gla_norm_g[0], rwkv_mu[0],
                  rwkv_w0[0], rwkv_w2[0], rwkv_a0[0], rwkv_a2[0], rwkv_g2[0], rwkv_k_k[0], rwkv_k_a[0],
                  rwkv_r_k[0], rwkv_ln_g[0], rwkv_ln_b[0], w_out[0], norm_mem_x_g[0], norm_mem_g[0],
                  wq_mem[0], wkv_mem[0], wo_mem[0], norm_ffn_g[0], w_gate_up[0], w_down[0], norm_final_g)
```

```python
import functools
import math

import jax
import jax.numpy as jnp
from jax import lax
from jax.experimental import pallas as pl
from jax.experimental.pallas import tpu as pltpu

F32 = jnp.float32
BF16 = jnp.bfloat16

MEM_HEADS = 4
GLA_HEADS = 4
GLA_DK = 64
GLA_DV = 128
GLA_GATE_RANK = 16
GLA_GATE_NORMALIZER = 16.0
GLA_NORM_EPS = 1e-5
RWKV_HEAD = 64
RWKV_DECAY_RANK = 64
RWKV_AAA_RANK = 64
RWKV_GATE_RANK = 128
RWKV_LN_EPS = 64e-5
NORM_EPS = 1e-6

CHUNK = 64
GLA_CHUNKS_PER_ITER = 2
SUBTILE_ROWS = 256
ROWS_IN_PROJ = 512
ROWS_GLA = 1024
ROWS_RWKV = 512
ROWS_POST_ATTN = 1024
ROWS_FFN = 512
FF_CHUNK = 256
LANES = 128
VMEM_LIMIT = 56 * 1024 * 1024


def _params(n_grid_dims):
    return pltpu.CompilerParams(dimension_semantics=("arbitrary",) * n_grid_dims,
                                vmem_limit_bytes=VMEM_LIMIT)


def _const_spec(shape, grid_rank):
    zeros = (0,) * len(shape)
    index_map = (lambda j: zeros) if grid_rank == 1 else (lambda i, j: zeros)
    return pl.BlockSpec(shape, index_map, pipeline_mode=pl.Buffered(1))


def _dot(a, b):
    return jnp.dot(a.astype(BF16), b.astype(BF16), preferred_element_type=F32)


def _rms_norm(x, g, eps):
    return x * lax.rsqrt(jnp.mean(x * x, axis=-1, keepdims=True) + eps) * g


def _sigmoid(x):
    return 1.0 / (1.0 + jnp.exp(-x))


def _log_sigmoid(x):
    return jnp.minimum(x, 0.0) - jnp.log1p(jnp.exp(-jnp.abs(x)))


def _iota2(shape, dim):
    return lax.broadcasted_iota(jnp.int32, shape, dim)


def _cumsum_operator(c):
    return (_iota2((c, 3 * c), 0) >= _iota2((c, 3 * c), 1) % c).astype(BF16)


def _chunk_cumsum(tri3, x):
    p1 = x.astype(BF16)
    rem = x - p1.astype(F32)
    p2 = rem.astype(BF16)
    p3 = (rem - p2.astype(F32)).astype(BF16)
    return jnp.dot(tri3, jnp.concatenate([p1, p2, p3], axis=0), preferred_element_type=F32)


def _mem_kv_kernel(mem_ref, g_ref, w_ref, k_ref, v_ref):
    d = k_ref.shape[-1]
    m = _rms_norm(mem_ref[...], g_ref[...], NORM_EPS)
    kv = _dot(m, w_ref[...])
    k_ref[...] = kv[:, :d].astype(BF16)
    v_ref[...] = kv[:, d:].astype(BF16)


def _mem_kv(mem2d, g, wkv):
    n, d = mem2d.shape
    return pl.pallas_call(
        _mem_kv_kernel,
        out_shape=(jax.ShapeDtypeStruct((n, d), BF16), jax.ShapeDtypeStruct((n, d), BF16)),
        compiler_params=pltpu.CompilerParams(vmem_limit_bytes=VMEM_LIMIT),
        name="mem_kv",
    )(mem2d, g, wkv)


def _in_proj_kernel(x_ref, g_ref, w_ref, wa2_ref, ba_ref, mu_ref, zg_ref, zr_ref, prev_ref, wg_ref, wr_ref):
    tm = x_ref.shape[0]
    n_main = zg_ref.shape[-1] - wa2_ref.shape[-1]
    n_gla = n_main + GLA_GATE_RANK

    @pl.when(jnp.logical_and(pl.program_id(0) == 0, pl.program_id(1) == 0))
    def _():
        for dst, c_first in ((wg_ref, 0), (wr_ref, n_gla)):
            for c0 in range(0, dst.shape[1], LANES):
                slab = w_ref[c_first + c0:c_first + c0 + LANES, :]
                dst[:, c0:c0 + LANES] = slab.T.astype(BF16)

    @pl.when(pl.program_id(1) == 0)
    def _():
        prev_ref[...] = jnp.zeros_like(prev_ref)

    sub = min(SUBTILE_ROWS, tm)
    prev = prev_ref[0:1, :]
    for s in range(tm // sub):
        rows = slice(s * sub, (s + 1) * sub)
        h = _rms_norm(x_ref[rows, :], g_ref[...], NORM_EPS).astype(BF16)

        zg = jnp.dot(h, wg_ref[...], preferred_element_type=F32)
        zg_ref[rows, :n_main] = zg[:, :n_main]
        pre = _dot(zg[:, n_main:], wa2_ref[...]) + ba_ref[...]
        zg_ref[rows, n_main:] = _log_sigmoid(pre) * (1.0 / GLA_GATE_NORMALIZER)

        zr = jnp.dot(h, wr_ref[...], preferred_element_type=F32)
        rolled = pltpu.roll(zr, shift=1, axis=0)
        first = _iota2(zr.shape, 0) == 0
        shifted = jnp.where(first, jnp.broadcast_to(prev, zr.shape), rolled)
        zr_ref[rows, :] = zr + (shifted - zr) * mu_ref[...]
        prev = zr[sub - 1:sub, :]
    prev_ref[0:1, :] = prev


def _in_proj(x, g, w_in_t, wa2p, ba, mu, tm):
    b, t, d = x.shape
    nr = mu.shape[1]
    n_main = w_in_t.shape[0] - nr - GLA_GATE_RANK
    ng = n_main + wa2p.shape[1]
    const = lambda *shape: _const_spec(shape, 2)
    return pl.pallas_call(
        _in_proj_kernel,
        grid=(b, t // tm),
        in_specs=[
            pl.BlockSpec((None, tm, d), lambda i, j: (i, j, 0)),
            const(1, d), const(*w_in_t.shape), const(*wa2p.shape),
            const(1, wa2p.shape[1]), const(1, nr),
        ],
        out_specs=(pl.BlockSpec((None, tm, ng), lambda i, j: (i, j, 0)),
                   pl.BlockSpec((None, tm, nr), lambda i, j: (i, j, 0))),
        out_shape=(jax.ShapeDtypeStruct((b, t, ng), F32), jax.ShapeDtypeStruct((b, t, nr), F32)),
        scratch_shapes=[pltpu.VMEM((8, nr), F32),
                        pltpu.VMEM((d, n_main + LANES), BF16),
                        pltpu.VMEM((d, nr), BF16)],
        compiler_params=_params(2),
        name="in_proj",
    )(x, g, w_in_t, wa2p, ba, mu)


def _gla_kernel(z_ref, ng_ref, o_ref, s_ref):
    nb, tt, _ = z_ref.shape
    kw = GLA_HEADS * GLA_DK
    vw = GLA_HEADS * GLA_DV
    c = CHUNK
    pairs = GLA_HEADS // 2
    pk = 2 * GLA_DK
    pv = 2 * GLA_DV
    cpi = GLA_CHUNKS_PER_ITER

    @pl.when(pl.program_id(0) == 0)
    def _():
        s_ref[...] = jnp.zeros_like(s_ref)

    tri3 = _cumsum_operator(c)
    causal = _iota2((c, pk), 0) >= _iota2((c, pk), 1) % c
    lane_lo = _iota2((c, pk), 1) < GLA_DK
    v_lo = _iota2((c, pv), 1) < GLA_DV
    s_mask = (_iota2((pv, pk), 0) // GLA_DV) == (_iota2((pv, pk), 1) // GLA_DK)
    norm_g = ng_ref[...]
    nt_dims = (((1,), (1,)), ((), ()))
    tn_dims = (((0,), (0,)), ((), ()))

    def step(it, carry):
        units = [(b, j, p) for b in range(nb) for j in range(cpi) for p in range(pairs)]
        rows = {(b, j): pl.ds(pl.multiple_of((it * cpi + j) * c, c), c) for b in range(nb) for j in range(cpi)}
        q_dec, k_inv, k_tail, dec, vv = {}, {}, {}, {}, {}
        for (b, j), rs in rows.items():
            q = z_ref[b, rs, 0:kw] * (GLA_DK ** -0.5)
            k = z_ref[b, rs, kw:2 * kw]
            log_a = z_ref[b, rs, 2 * kw + 2 * vw:3 * kw + 2 * vw]
            bcum = _chunk_cumsum(tri3, log_a)
            blast = bcum[c - 1:c, :]
            qd = q * jnp.exp(bcum)
            ki = k * jnp.exp(-bcum)
            kt = k * jnp.exp(blast - bcum)
            dc = jnp.exp(blast)
            for p in range(pairs):
                ks = slice(p * pk, (p + 1) * pk)
                q_dec[b, j, p] = qd[:, ks].astype(BF16)
                k_inv[b, j, p] = ki[:, ks]
                k_tail[b, j, p] = kt[:, ks].astype(BF16)
                dec[b, j, p] = dc[:, ks]
                vv[b, j, p] = z_ref[b, rs, 2 * kw + p * pv:2 * kw + (p + 1) * pv].astype(BF16)

        o_intra, d_state = {}, {}
        for u in units:
            scores = jnp.where(causal, lax.dot_general(q_dec[u], _bd(k_inv[u], lane_lo), nt_dims,
                                                       preferred_element_type=F32), 0.0)
            zero = jnp.zeros_like(vv[u])
            v_bd = jnp.concatenate([jnp.where(v_lo, vv[u], zero), jnp.where(v_lo, zero, vv[u])], axis=0)
            o_intra[u] = _mm1(scores, v_bd)
            d_state[u] = jnp.where(s_mask, lax.dot_general(vv[u], k_tail[u], tn_dims,
                                                           preferred_element_type=F32), 0.0)

        for b in range(nb):
            s_cur = [s_ref[b * pairs + p] for p in range(pairs)]
            for j in range(cpi):
                outs = []
                for p in range(pairs):
                    u = (b, j, p)
                    o = o_intra[u] + lax.dot_general(q_dec[u], s_cur[p].astype(BF16), nt_dims,
                                                     preferred_element_type=F32)
                    s_cur[p] = s_cur[p] * dec[u] + d_state[u]
                    for h in range(2):
                        oh = o[:, h * GLA_DV:(h + 1) * GLA_DV]
                        oh = oh * lax.rsqrt(jnp.mean(oh * oh, axis=-1, keepdims=True) + GLA_NORM_EPS) * norm_g
                        g0 = 2 * kw + vw + (2 * p + h) * GLA_DV
                        gh = z_ref[b, rows[b, j], g0:g0 + GLA_DV]
                        outs.append(oh * (gh * _sigmoid(gh)))
                o_ref[b, rows[b, j], :] = jnp.concatenate(outs, axis=-1).astype(o_ref.dtype)
            for p in range(pairs):
                s_ref[b * pairs + p] = s_cur[p]
        return carry

    lax.fori_loop(0, tt // (c * cpi), step, 0)


def _gla(zg, norm_g, tt):
    b, t, n = zg.shape
    vw = GLA_HEADS * GLA_DV
    return pl.pallas_call(
        _gla_kernel,
        grid=(t // tt,),
        in_specs=[pl.BlockSpec((b, tt, n), lambda j: (0, j, 0)),
                  _const_spec((1, GLA_DV), 1)],
        out_specs=pl.BlockSpec((b, tt, vw), lambda j: (0, j, 0)),
        out_shape=jax.ShapeDtypeStruct((b, t, vw), BF16),
        scratch_shapes=[pltpu.VMEM((b * GLA_HEADS // 2, 2 * GLA_DV, 2 * GLA_DK), F32)],
        compiler_params=_params(1),
        name="gla",
    )(zg, norm_g)


def _seg_sum(x, seg_ones):
    xb = x.astype(BF16)
    blk = seg_ones.shape[0]
    parts = []
    for s in range(x.shape[1] // blk):
        sl = slice(s * blk, (s + 1) * blk)
        parts.append(jnp.dot(xb[:, sl], seg_ones, preferred_element_type=F32))
    return jnp.concatenate(parts, axis=-1)


def _block_diag(z, lane_lo):
    zero = jnp.zeros_like(z)
    return jnp.concatenate([jnp.where(lane_lo, z, zero), jnp.where(lane_lo, zero, z)], axis=0)


def _bd(z, lane_lo):
    return _block_diag(z.astype(BF16), lane_lo)


def _mm1(x, r):
    return jnp.dot(x.astype(BF16), r, preferred_element_type=F32)


def _rwkv_kernel(z_ref, w0_ref, w2_ref, a0_ref, a2_ref, g2_ref, kk_ref, ka_ref, rk_ref, lng_ref, lnb_ref,
                 o_ref, s_ref, st_ref, y_ref):
    nb, tt, _ = z_ref.shape
    w = w0_ref.shape[-1]
    n = RWKV_HEAD
    pairs = w // LANES
    c = CHUNK
    o_wl = 3 * w
    o_al = o_wl + RWKV_DECAY_RANK
    o_gl = o_al + RWKV_AAA_RANK

    @pl.when(pl.program_id(0) == 0)
    def _():
        s_ref[...] = jnp.zeros_like(s_ref)

    seg = 2 * LANES
    seg_ones = (_iota2((seg, seg), 0) // n == _iota2((seg, seg), 1) // n).astype(BF16)

    cols = lambda lo, hi: z_ref[:, :, lo:hi].reshape(nb * tt, hi - lo)
    r = cols(0, w)
    k = cols(w, 2 * w)
    v = cols(2 * w, 3 * w)
    u = w0_ref[...] + _dot(jnp.tanh(cols(o_wl, o_al)), w2_ref[...])
    lw = -math.exp(-0.5) * _sigmoid(u)
    a = _sigmoid(a0_ref[...] + _dot(cols(o_al, o_gl), a2_ref[...]))
    kk = k * kk_ref[...]
    kk = kk * lax.rsqrt(jnp.maximum(_seg_sum(kk * kk, seg_ones), 1e-24))
    k2 = k * (1.0 + (a - 1.0) * ka_ref[...])
    st_ref[0] = r
    st_ref[1] = lw
    st_ref[2] = k2
    st_ref[3] = v
    st_ref[4] = -kk
    st_ref[5] = kk * a

    tri3 = _cumsum_operator(c)
    row = _iota2((c, LANES), 0)
    col = _iota2((c, LANES), 1) % c
    lower = row >= col
    strict = row > col
    eye = (row == col).astype(F32)
    blk16 = (row // 16) == (col // 16)
    blk32 = (row // 32) == (col // 32)
    off16 = jnp.logical_and(blk32, jnp.logical_not(blk16))
    lane_lo = _iota2((c, LANES), 1) < n
    bd_mask = (_iota2((LANES, LANES), 0) // n) == (_iota2((LANES, LANES), 1) // n)
    units = [(b, p) for b in range(nb) for p in range(pairs)]
    prange = range(len(units))
    nt_dims = (((1,), (1,)), ((), ()))

    def chunk(ci, carry):
        rows = [pl.ds(pl.multiple_of(b * tt + ci * c, c), c) for b in range(nb)]
        a_t, r_t, b_t, k_t, b_h, k_h, vc, g_c = [], [], [], [], [], [], [], []
        for b in range(nb):
            rc = st_ref[0, rows[b], :]
            lwc = st_ref[1, rows[b], :]
            kc = st_ref[2, rows[b], :]
            ac = st_ref[4, rows[b], :]
            bc = st_ref[5, rows[b], :]
            cum = _chunk_cumsum(tri3, lwc)
            last = cum[c - 1:c, :]
            e_neg = jnp.exp(-cum)
            e_tail = jnp.exp(last - cum)
            r_tb = rc * jnp.exp(cum)
            a_tb = ac * jnp.exp(cum - lwc)
            vb = st_ref[3, rows[b], :]
            for p in range(pairs):
                sl = slice(p * LANES, (p + 1) * LANES)
                a_t.append(a_tb[:, sl])
                r_t.append(r_tb[:, sl])
                b_t.append(bc[:, sl] * e_neg[:, sl])
                k_t.append(kc[:, sl] * e_neg[:, sl])
                b_h.append(bc[:, sl] * e_tail[:, sl])
                k_h.append(kc[:, sl] * e_tail[:, sl])
                vc.append(vb[:, sl])
                g_c.append(jnp.exp(last[:, sl]))

        a_ab, a_ak, q_b, q_k = [], [], [], []
        for p in prange:
            lhs = jnp.concatenate([a_t[p], r_t[p]], axis=0).astype(BF16)
            rhs = jnp.concatenate([_bd(b_t[p], lane_lo), _bd(k_t[p], lane_lo)], axis=0)
            aa = lax.dot_general(lhs, rhs, nt_dims, preferred_element_type=F32)
            a_ab.append(jnp.where(strict, aa[:c, :LANES], 0.0))
            a_ak.append(jnp.where(strict, aa[:c, LANES:], 0.0))
            q_b.append(jnp.where(lower, aa[c:, :LANES], 0.0))
            q_k.append(jnp.where(lower, aa[c:, LANES:], 0.0))

        ad = [jnp.where(blk16, x, 0.0) for x in a_ab]
        pw = [_mm1(ad[p], _bd(ad[p], lane_lo)) for p in prange]
        t = [eye + ad[p] for p in prange]
        for _ in range(2):
            both = [_mm1(jnp.concatenate([pw[p], t[p]], axis=0), _bd(pw[p], lane_lo)) for p in prange]
            pw = [x[:c] for x in both]
            t = [t[p] + both[p][c:] for p in prange]
        t = [t[p] + _mm1(t[p], _bd(pw[p], lane_lo)) for p in prange]
        for msk in (off16, jnp.logical_not(blk32)):
            te = [_mm1(t[p], _bd(jnp.where(msk, a_ab[p], 0.0), lane_lo)) for p in prange]
            t = [t[p] + _mm1(te[p], _bd(t[p], lane_lo)) for p in prange]

        v_r = [_bd(vc[p], lane_lo) for p in prange]
        pv = [_mm1(a_ak[p], v_r[p]) for p in prange]
        wu = [_mm1(t[p], jnp.concatenate([_bd(a_t[p], lane_lo), _bd(pv[p], lane_lo)], axis=1))
              for p in prange]

        for p in prange:
            bi, pi = units[p]
            s0 = s_ref[p]
            ws = lax.dot_general(jnp.concatenate([wu[p][:, :LANES], r_t[p]], axis=0).astype(BF16),
                                 s0.astype(BF16), nt_dims, preferred_element_type=F32)
            u_mat = ws[:c] + wu[p][:, LANES:]
            y = ws[c:] + _mm1(jnp.concatenate([q_b[p], q_k[p]], axis=1),
                              jnp.concatenate([_bd(u_mat, lane_lo), v_r[p]], axis=0))
            y_ref[rows[bi], pi * LANES:(pi + 1) * LANES] = y
            ds = _mm1(jnp.concatenate([u_mat, vc[p]], axis=0).T,
                      jnp.concatenate([b_h[p], k_h[p]], axis=0).astype(BF16))
            s_ref[p] = s0 * g_c[p] + jnp.where(bd_mask, ds, 0.0)
        return carry

    lax.fori_loop(0, tt // c, chunk, 0)

    y = y_ref[...]
    mean = _seg_sum(y, seg_ones) * (1.0 / n)
    dlt = y - mean
    var = _seg_sum(dlt * dlt, seg_ones) * (1.0 / n)
    yn = dlt * lax.rsqrt(var + RWKV_LN_EPS) * lng_ref[...] + lnb_ref[...]
    bonus = _seg_sum(st_ref[0] * st_ref[2] * rk_ref[...], seg_ones) * st_ref[3]
    gate = _dot(_sigmoid(cols(o_gl, o_gl + RWKV_GATE_RANK)), g2_ref[...])
    o_ref[...] = ((yn + bonus) * gate).reshape(nb, tt, w).astype(o_ref.dtype)


def _rwkv(zr, w0, w2, a0, a2, g2, k_k, k_a, r_k, ln_g, ln_b, tt):
    b, t, nz = zr.shape
    w = w0.shape[-1]
    const = lambda *shape: _const_spec(shape, 1)
    return pl.pallas_call(
        _rwkv_kernel,
        grid=(t // tt,),
        in_specs=[pl.BlockSpec((b, tt, nz), lambda j: (0, j, 0)),
                  const(1, w), const(*w2.shape), const(1, w), const(*a2.shape), const(*g2.shape),
                  const(1, w), const(1, w), const(1, w), const(1, w), const(1, w)],
        out_specs=pl.BlockSpec((b, tt, w), lambda j: (0, j, 0)),
        out_shape=jax.ShapeDtypeStruct((b, t, w), BF16),
        scratch_shapes=[pltpu.VMEM((b * w // LANES, LANES, LANES), F32),
                        pltpu.VMEM((6, b * tt, w), F32),
                        pltpu.VMEM((b * tt, w), F32)],
        compiler_params=_params(1),
        name="rwkv",
    )(zr, w0, w2, a0, a2, g2, k_k, k_a, r_k, ln_g, ln_b)


def _post_attn_kernel(x_ref, og_ref, or_ref, wog_ref, wor_ref, gx_ref, wq_ref, k_ref, v_ref, wo_ref, out_ref):
    d = x_ref.shape[-1]
    hd = d // MEM_HEADS
    x1 = (x_ref[...] + jnp.dot(og_ref[...], wog_ref[...], preferred_element_type=F32)
          + jnp.dot(or_ref[...], wor_ref[...], preferred_element_type=F32))
    q = _dot(_rms_norm(x1, gx_ref[...], NORM_EPS), wq_ref[...]).astype(BF16)
    heads = []
    for h in range(MEM_HEADS):
        hs = slice(h * hd, (h + 1) * hd)
        s = lax.dot_general(q[:, hs], k_ref[:, hs], (((1,), (1,)), ((), ())),
                            preferred_element_type=F32) * (hd ** -0.5)
        e = jnp.exp(s - jnp.max(s, axis=-1, keepdims=True))
        p = e / jnp.sum(e, axis=-1, keepdims=True)
        heads.append(_dot(p, v_ref[:, hs]))
    o = jnp.concatenate(heads, axis=-1)
    out_ref[...] = x1 + _dot(o, wo_ref[...])


def _post_attn(x, o_gla, o_rwkv, wo_g, wo_r, gx, wq, kmem, vmem, wo, tm):
    b, t, d = x.shape
    m = kmem.shape[1]
    const = lambda *shape: _const_spec(shape, 2)
    tile = lambda width: pl.BlockSpec((None, tm, width), lambda i, j: (i, j, 0))
    return pl.pallas_call(
        _post_attn_kernel,
        grid=(b, t // tm),
        in_specs=[tile(d), tile(o_gla.shape[-1]), tile(o_rwkv.shape[-1]),
                  const(*wo_g.shape), const(*wo_r.shape), const(1, d), const(*wq.shape),
                  pl.BlockSpec((None, m, d), lambda i, j: (i, 0, 0)),
                  pl.BlockSpec((None, m, d), lambda i, j: (i, 0, 0)),
                  const(*wo.shape)],
        out_specs=tile(d),
        out_shape=jax.ShapeDtypeStruct((b, t, d), F32),
        compiler_params=_params(2),
        name="post_attn",
    )(x, o_gla, o_rwkv, wo_g, wo_r, gx, wq, kmem, vmem, wo)


def _ffn_kernel(x_ref, g_ref, wgu_ref, wd_ref, gf_ref, out_ref, *, ff_chunk):
    x = x_ref[...]
    d_ff = wd_ref.shape[0]
    h = _rms_norm(x, g_ref[...], NORM_EPS).astype(BF16)
    acc = x
    for c0 in range(0, d_ff, ff_chunk):
        gate = _dot(h, wgu_ref[:, c0:c0 + ff_chunk])
        up = _dot(h, wgu_ref[:, d_ff + c0:d_ff + c0 + ff_chunk])
        act = gate * _sigmoid(gate) * up
        acc = acc + _dot(act, wd_ref[c0:c0 + ff_chunk, :])
    out_ref[...] = _rms_norm(acc, gf_ref[...], NORM_EPS)


def _ffn(x, g, wgu, wd, gf, tm, ff_chunk):
    b, t, d = x.shape
    const = lambda *shape: _const_spec(shape, 2)
    tile = pl.BlockSpec((None, tm, d), lambda i, j: (i, j, 0))
    return pl.pallas_call(
        functools.partial(_ffn_kernel, ff_chunk=ff_chunk),
        grid=(b, t // tm),
        in_specs=[tile, const(1, d), const(*wgu.shape), const(*wd.shape), const(1, d)],
        out_specs=tile,
        out_shape=jax.ShapeDtypeStruct((b, t, d), F32),
        compiler_params=_params(2),
        name="ffn",
    )(x, g, wgu, wd, gf)


def _layer(x, mem, norm_mix_g, w_in, gla_wa2, gla_ba, gla_norm_g, rwkv_mu, rwkv_w0, rwkv_w2, rwkv_a0,
           rwkv_a2, rwkv_g2, rwkv_k_k, rwkv_k_a, rwkv_r_k, rwkv_ln_g, rwkv_ln_b, w_out, norm_mem_x_g,
           norm_mem_g, wq_mem, wkv_mem, wo_mem, norm_ffn_g, w_gate_up, w_down, final_g):
    b, t, d = x.shape
    m = mem.shape[1]
    row = lambda p: p.reshape(1, -1).astype(F32)
    vw = GLA_HEADS * GLA_DV
    wa2p = jnp.pad(gla_wa2, ((0, LANES - GLA_GATE_RANK), (0, 0))).astype(BF16)

    kmem, vmem = _mem_kv(mem.reshape(b * m, d), row(norm_mem_g), wkv_mem)
    zg, zr = _in_proj(x, row(norm_mix_g), w_in.T, wa2p, row(gla_ba), row(rwkv_mu), tm=min(ROWS_IN_PROJ, t))
    o_gla = _gla(zg, row(gla_norm_g), tt=min(ROWS_GLA, t))
    o_rwkv = _rwkv(zr, row(rwkv_w0), rwkv_w2.astype(BF16), row(rwkv_a0), rwkv_a2.astype(BF16),
                   rwkv_g2.astype(BF16), row(rwkv_k_k), row(rwkv_k_a), row(rwkv_r_k), row(rwkv_ln_g),
                   row(rwkv_ln_b), tt=min(ROWS_RWKV, t))
    w_out_b = w_out.astype(BF16)
    x2 = _post_attn(x, o_gla, o_rwkv, w_out_b[:vw], w_out_b[vw:], row(norm_mem_x_g), wq_mem.astype(BF16),
                    kmem.reshape(b, m, d), vmem.reshape(b, m, d), wo_mem.astype(BF16),
                    tm=min(ROWS_POST_ATTN, t))
    return _ffn(x2, row(norm_ffn_g), w_gate_up, w_down, row(final_g), tm=min(ROWS_FFN, t), ff_chunk=FF_CHUNK)


def kernel(x, mem, norm_mix_g, w_in, gla_wa2, gla_ba, gla_norm_g, rwkv_mu, rwkv_w0, rwkv_w2, rwkv_a0, rwkv_a2, rwkv_g2, rwkv_k_k, rwkv_k_a, rwkv_r_k, rwkv_ln_g, rwkv_ln_b, w_out, norm_mem_x_g, norm_mem_g, wq_mem, wkv_mem, wo_mem, norm_ffn_g, w_gate_up, w_down, norm_final_g):
    assert norm_mix_g.shape[0] == 1, "single-layer block"
    return _layer(x, mem, norm_mix_g[0], w_in[0], gla_wa2[0], gla_ba[0], gla_norm_g[0], rwkv_mu[0],
                  rwkv_w0[0], rwkv_w2[0], rwkv_a0[0], rwkv_a2[0], rwkv_g2[0], rwkv_k_k[0], rwkv_k_a[0],
                  rwkv_r_k[0], rwkv_ln_g[0], rwkv_ln_b[0], w_out[0], norm_mem_x_g[0], norm_mem_g[0],
                  wq_mem[0], wkv_mem[0], wo_mem[0], norm_ffn_g[0], w_gate_up[0], w_down[0], norm_final_g)
```

```python
import functools
import math

import jax
import jax.numpy as jnp
from jax import lax
from jax.experimental import pallas as pl
from jax.experimental.pallas import tpu as pltpu

F32 = jnp.float32
BF16 = jnp.bfloat16

MEM_HEADS = 4
GLA_HEADS = 4
GLA_DK = 64
GLA_DV = 128
GLA_GATE_RANK = 16
GLA_GATE_NORMALIZER = 16.0
GLA_NORM_EPS = 1e-5
RWKV_HEAD = 64
RWKV_DECAY_RANK = 64
RWKV_AAA_RANK = 64
RWKV_GATE_RANK = 128
RWKV_LN_EPS = 64e-5
NORM_EPS = 1e-6

CHUNK = 64
GLA_CHUNKS_PER_ITER = 2
SUBTILE_ROWS = 256
ROWS_IN_PROJ = 512
ROWS_GLA = 1024
ROWS_RWKV = 512
ROWS_POST_ATTN = 1024
ROWS_FFN = 1024
FF_CHUNK = 256
LANES = 128
VMEM_LIMIT = 56 * 1024 * 1024


def _params(n_grid_dims):
    return pltpu.CompilerParams(dimension_semantics=("arbitrary",) * n_grid_dims,
                                vmem_limit_bytes=VMEM_LIMIT)


def _const_spec(shape, grid_rank):
    zeros = (0,) * len(shape)
    index_map = (lambda j: zeros) if grid_rank == 1 else (lambda i, j: zeros)
    return pl.BlockSpec(shape, index_map, pipeline_mode=pl.Buffered(1))


def _dot(a, b):
    return jnp.dot(a.astype(BF16), b.astype(BF16), preferred_element_type=F32)


def _rms_norm(x, g, eps):
    return x * lax.rsqrt(jnp.mean(x * x, axis=-1, keepdims=True) + eps) * g


def _sigmoid(x):
    return 1.0 / (1.0 + jnp.exp(-x))


def _log_sigmoid(x):
    return jnp.minimum(x, 0.0) - jnp.log1p(jnp.exp(-jnp.abs(x)))


def _iota2(shape, dim):
    return lax.broadcasted_iota(jnp.int32, shape, dim)


def _cumsum_operator(c):
    return (_iota2((c, 3 * c), 0) >= _iota2((c, 3 * c), 1) % c).astype(BF16)


def _chunk_cumsum(tri3, x):
    p1 = x.astype(BF16)
    rem = x - p1.astype(F32)
    p2 = rem.astype(BF16)
    p3 = (rem - p2.astype(F32)).astype(BF16)
    return jnp.dot(tri3, jnp.concatenate([p1, p2, p3], axis=0), preferred_element_type=F32)


def _mem_kv_kernel(mem_ref, g_ref, w_ref, k_ref, v_ref):
    d = k_ref.shape[-1]
    m = _rms_norm(mem_ref[...], g_ref[...], NORM_EPS)
    kv = _dot(m, w_ref[...])
    k_ref[...] = kv[:, :d].astype(BF16)
    v_ref[...] = kv[:, d:].astype(BF16)


def _mem_kv(mem2d, g, wkv):
    n, d = mem2d.shape
    return pl.pallas_call(
        _mem_kv_kernel,
        out_shape=(jax.ShapeDtypeStruct((n, d), BF16), jax.ShapeDtypeStruct((n, d), BF16)),
        compiler_params=pltpu.CompilerParams(vmem_limit_bytes=VMEM_LIMIT),
        name="mem_kv",
    )(mem2d, g, wkv)


def _in_proj_kernel(x_ref, g_ref, w_ref, wa2_ref, ba_ref, mu_ref, zg_ref, zr_ref, prev_ref, wg_ref, wr_ref):
    tm = x_ref.shape[0]
    n_main = zg_ref.shape[-1] - wa2_ref.shape[-1]
    n_gla = n_main + GLA_GATE_RANK

    @pl.when(jnp.logical_and(pl.program_id(0) == 0, pl.program_id(1) == 0))
    def _():
        for dst, c_first in ((wg_ref, 0), (wr_ref, n_gla)):
            for c0 in range(0, dst.shape[1], LANES):
                slab = w_ref[c_first + c0:c_first + c0 + LANES, :]
                dst[:, c0:c0 + LANES] = slab.T.astype(BF16)

    @pl.when(pl.program_id(1) == 0)
    def _():
        prev_ref[...] = jnp.zeros_like(prev_ref)

    sub = min(SUBTILE_ROWS, tm)
    prev = prev_ref[0:1, :]
    for s in range(tm // sub):
        rows = slice(s * sub, (s + 1) * sub)
        h = _rms_norm(x_ref[rows, :], g_ref[...], NORM_EPS).astype(BF16)

        zr = jnp.dot(h, wr_ref[...], preferred_element_type=F32)
        rolled = pltpu.roll(zr, shift=1, axis=0)
        first = _iota2(zr.shape, 0) == 0
        shifted = jnp.where(first, jnp.broadcast_to(prev, zr.shape), rolled)
        zr_ref[rows, :] = zr + (shifted - zr) * mu_ref[...]
        prev = zr[sub - 1:sub, :]

        a_lo = jnp.dot(h, wg_ref[:, n_main:], preferred_element_type=F32)
        pre = _dot(a_lo, wa2_ref[...]) + ba_ref[...]
        zg_ref[rows, n_main:] = _log_sigmoid(pre) * (1.0 / GLA_GATE_NORMALIZER)
        zg_ref[rows, :n_main] = jnp.dot(h, wg_ref[:, :n_main], preferred_element_type=F32)
    prev_ref[0:1, :] = prev


def _in_proj(x, g, w_in_t, wa2p, ba, mu, tm):
    b, t, d = x.shape
    nr = mu.shape[1]
    n_main = w_in_t.shape[0] - nr - GLA_GATE_RANK
    ng = n_main + wa2p.shape[1]
    const = lambda *shape: _const_spec(shape, 2)
    return pl.pallas_call(
        _in_proj_kernel,
        grid=(b, t // tm),
        in_specs=[
            pl.BlockSpec((None, tm, d), lambda i, j: (i, j, 0)),
            const(1, d), const(*w_in_t.shape), const(*wa2p.shape),
            const(1, wa2p.shape[1]), const(1, nr),
        ],
        out_specs=(pl.BlockSpec((None, tm, ng), lambda i, j: (i, j, 0)),
                   pl.BlockSpec((None, tm, nr), lambda i, j: (i, j, 0))),
        out_shape=(jax.ShapeDtypeStruct((b, t, ng), F32), jax.ShapeDtypeStruct((b, t, nr), F32)),
        scratch_shapes=[pltpu.VMEM((8, nr), F32),
                        pltpu.VMEM((d, n_main + LANES), BF16),
                        pltpu.VMEM((d, nr), BF16)],
        compiler_params=_params(2),
        name="in_proj",
    )(x, g, w_in_t, wa2p, ba, mu)


def _gla_kernel(z_ref, ng_ref, o_ref, s_ref):
    nb, tt, _ = z_ref.shape
    kw = GLA_HEADS * GLA_DK
    vw = GLA_HEADS * GLA_DV
    c = CHUNK
    pairs = GLA_HEADS // 2
    pk = 2 * GLA_DK
    pv = 2 * GLA_DV
    cpi = GLA_CHUNKS_PER_ITER

    @pl.when(pl.program_id(0) == 0)
    def _():
        s_ref[...] = jnp.zeros_like(s_ref)

    tri3 = _cumsum_operator(c)
    causal = _iota2((c, pk), 0) >= _iota2((c, pk), 1) % c
    lane_lo = _iota2((c, pk), 1) < GLA_DK
    v_lo = _iota2((c, pv), 1) < GLA_DV
    s_mask = (_iota2((pv, pk), 0) // GLA_DV) == (_iota2((pv, pk), 1) // GLA_DK)
    norm_g = ng_ref[...]
    nt_dims = (((1,), (1,)), ((), ()))
    tn_dims = (((0,), (0,)), ((), ()))

    def step(it, carry):
        units = [(b, j, p) for b in range(nb) for j in range(cpi) for p in range(pairs)]
        rows = {(b, j): pl.ds(pl.multiple_of((it * cpi + j) * c, c), c) for b in range(nb) for j in range(cpi)}
        q_dec, k_inv, k_tail, dec, vv = {}, {}, {}, {}, {}
        for (b, j), rs in rows.items():
            q = z_ref[b, rs, 0:kw] * (GLA_DK ** -0.5)
            k = z_ref[b, rs, kw:2 * kw]
            log_a = z_ref[b, rs, 2 * kw + 2 * vw:3 * kw + 2 * vw]
            bcum = _chunk_cumsum(tri3, log_a)
            blast = bcum[c - 1:c, :]
            qd = q * jnp.exp(bcum)
            ki = k * jnp.exp(-bcum)
            kt = k * jnp.exp(blast - bcum)
            dc = jnp.exp(blast)
            for p in range(pairs):
                ks = slice(p * pk, (p + 1) * pk)
                q_dec[b, j, p] = qd[:, ks].astype(BF16)
                k_inv[b, j, p] = ki[:, ks]
                k_tail[b, j, p] = kt[:, ks].astype(BF16)
                dec[b, j, p] = dc[:, ks]
                vv[b, j, p] = z_ref[b, rs, 2 * kw + p * pv:2 * kw + (p + 1) * pv].astype(BF16)

        o_intra, d_state = {}, {}
        for u in units:
            scores = jnp.where(causal, lax.dot_general(q_dec[u], _bd(k_inv[u], lane_lo), nt_dims,
                                                       preferred_element_type=F32), 0.0)
            zero = jnp.zeros_like(vv[u])
            v_bd = jnp.concatenate([jnp.where(v_lo, vv[u], zero), jnp.where(v_lo, zero, vv[u])], axis=0)
            o_intra[u] = _mm1(scores, v_bd)
            d_state[u] = jnp.where(s_mask, lax.dot_general(vv[u], k_tail[u], tn_dims,
                                                           preferred_element_type=F32), 0.0)

        for b in range(nb):
            s_cur = [s_ref[b * pairs + p] for p in range(pairs)]
            for j in range(cpi):
                outs = []
                for p in range(pairs):
                    u = (b, j, p)
                    o = o_intra[u] + lax.dot_general(q_dec[u], s_cur[p].astype(BF16), nt_dims,
                                                     preferred_element_type=F32)
                    s_cur[p] = s_cur[p] * dec[u] + d_state[u]
                    for h in range(2):
                        oh = o[:, h * GLA_DV:(h + 1) * GLA_DV]
                        oh = oh * lax.rsqrt(jnp.mean(oh * oh, axis=-1, keepdims=True) + GLA_NORM_EPS) * norm_g
                        g0 = 2 * kw + vw + (2 * p + h) * GLA_DV
                        gh = z_ref[b, rows[b, j], g0:g0 + GLA_DV]
                        outs.append(oh * (gh * _sigmoid(gh)))
                o_ref[b, rows[b, j], :] = jnp.concatenate(outs, axis=-1).astype(o_ref.dtype)
            for p in range(pairs):
                s_ref[b * pairs + p] = s_cur[p]
        return carry

    lax.fori_loop(0, tt // (c * cpi), step, 0)


def _gla(zg, norm_g, tt):
    b, t, n = zg.shape
    vw = GLA_HEADS * GLA_DV
    return pl.pallas_call(
        _gla_kernel,
        grid=(t // tt,),
        in_specs=[pl.BlockSpec((b, tt, n), lambda j: (0, j, 0)),
                  _const_spec((1, GLA_DV), 1)],
        out_specs=pl.BlockSpec((b, tt, vw), lambda j: (0, j, 0)),
        out_shape=jax.ShapeDtypeStruct((b, t, vw), BF16),
        scratch_shapes=[pltpu.VMEM((b * GLA_HEADS // 2, 2 * GLA_DV, 2 * GLA_DK), F32)],
        compiler_params=_params(1),
        name="gla",
    )(zg, norm_g)


def _seg_sum(x, seg_ones):
    xb = x.astype(BF16)
    blk = seg_ones.shape[0]
    parts = []
    for s in range(x.shape[1] // blk):
        sl = slice(s * blk, (s + 1) * blk)
        parts.append(jnp.dot(xb[:, sl], seg_ones, preferred_element_type=F32))
    return jnp.concatenate(parts, axis=-1)


def _block_diag(z, lane_lo):
    zero = jnp.zeros_like(z)
    return jnp.concatenate([jnp.where(lane_lo, z, zero), jnp.where(lane_lo, zero, z)], axis=0)


def _bd(z, lane_lo):
    return _block_diag(z.astype(BF16), lane_lo)


def _mm1(x, r):
    return jnp.dot(x.astype(BF16), r, preferred_element_type=F32)


def _rwkv_kernel(z_ref, w0_ref, w2_ref, a0_ref, a2_ref, g2_ref, kk_ref, ka_ref, rk_ref, lng_ref, lnb_ref,
                 o_ref, s_ref, st_ref, y_ref):
    nb, tt, _ = z_ref.shape
    w = w0_ref.shape[-1]
    n = RWKV_HEAD
    pairs = w // LANES
    c = CHUNK
    o_wl = 3 * w
    o_al = o_wl + RWKV_DECAY_RANK
    o_gl = o_al + RWKV_AAA_RANK

    @pl.when(pl.program_id(0) == 0)
    def _():
        s_ref[...] = jnp.zeros_like(s_ref)

    seg = 2 * LANES
    seg_ones = (_iota2((seg, seg), 0) // n == _iota2((seg, seg), 1) // n).astype(BF16)

    cols = lambda lo, hi: z_ref[:, :, lo:hi].reshape(nb * tt, hi - lo)
    r = cols(0, w)
    k = cols(w, 2 * w)
    v = cols(2 * w, 3 * w)
    u = w0_ref[...] + _dot(jnp.tanh(cols(o_wl, o_al)), w2_ref[...])
    lw = -math.exp(-0.5) * _sigmoid(u)
    a = _sigmoid(a0_ref[...] + _dot(cols(o_al, o_gl), a2_ref[...]))
    kk = k * kk_ref[...]
    kk = kk * lax.rsqrt(jnp.maximum(_seg_sum(kk * kk, seg_ones), 1e-24))
    k2 = k * (1.0 + (a - 1.0) * ka_ref[...])
    st_ref[0] = r
    st_ref[1] = lw
    st_ref[2] = k2
    st_ref[3] = v
    st_ref[4] = -kk
    st_ref[5] = kk * a

    tri3 = _cumsum_operator(c)
    row = _iota2((c, LANES), 0)
    col = _iota2((c, LANES), 1) % c
    lower = row >= col
    strict = row > col
    eye = (row == col).astype(F32)
    blk16 = (row // 16) == (col // 16)
    blk32 = (row // 32) == (col // 32)
    off16 = jnp.logical_and(blk32, jnp.logical_not(blk16))
    lane_lo = _iota2((c, LANES), 1) < n
    bd_mask = (_iota2((LANES, LANES), 0) // n) == (_iota2((LANES, LANES), 1) // n)
    units = [(b, p) for b in range(nb) for p in range(pairs)]
    prange = range(len(units))
    nt_dims = (((1,), (1,)), ((), ()))

    def chunk(ci, carry):
        rows = [pl.ds(pl.multiple_of(b * tt + ci * c, c), c) for b in range(nb)]
        a_t, r_t, b_t, k_t, b_h, k_h, vc, g_c = [], [], [], [], [], [], [], []
        for b in range(nb):
            rc = st_ref[0, rows[b], :]
            lwc = st_ref[1, rows[b], :]
            kc = st_ref[2, rows[b], :]
            ac = st_ref[4, rows[b], :]
            bc = st_ref[5, rows[b], :]
            cum = _chunk_cumsum(tri3, lwc)
            last = cum[c - 1:c, :]
            e_neg = jnp.exp(-cum)
            e_tail = jnp.exp(last - cum)
            r_tb = rc * jnp.exp(cum)
            a_tb = ac * jnp.exp(cum - lwc)
            vb = st_ref[3, rows[b], :]
            for p in range(pairs):
                sl = slice(p * LANES, (p + 1) * LANES)
                a_t.append(a_tb[:, sl])
                r_t.append(r_tb[:, sl])
                b_t.append(bc[:, sl] * e_neg[:, sl])
                k_t.append(kc[:, sl] * e_neg[:, sl])
                b_h.append(bc[:, sl] * e_tail[:, sl])
                k_h.append(kc[:, sl] * e_tail[:, sl])
                vc.append(vb[:, sl])
                g_c.append(jnp.exp(last[:, sl]))

        a_ab, a_ak, q_b, q_k = [], [], [], []
        for p in prange:
            lhs = jnp.concatenate([a_t[p], r_t[p]], axis=0).astype(BF16)
            rhs = jnp.concatenate([_bd(b_t[p], lane_lo), _bd(k_t[p], lane_lo)], axis=0)
            aa = lax.dot_general(lhs, rhs, nt_dims, preferred_element_type=F32)
            a_ab.append(jnp.where(strict, aa[:c, :LANES], 0.0))
            a_ak.append(jnp.where(strict, aa[:c, LANES:], 0.0))
            q_b.append(jnp.where(lower, aa[c:, :LANES], 0.0))
            q_k.append(jnp.where(lower, aa[c:, LANES:], 0.0))

        ad = [jnp.where(blk16, x, 0.0) for x in a_ab]
        pw = [_mm1(ad[p], _bd(ad[p], lane_lo)) for p in prange]
        t = [eye + ad[p] for p in prange]
        for _ in range(2):
            both = [_mm1(jnp.concatenate([pw[p], t[p]], axis=0), _bd(pw[p], lane_lo)) for p in prange]
            pw = [x[:c] for x in both]
            t = [t[p] + both[p][c:] for p in prange]
        t = [t[p] + _mm1(t[p], _bd(pw[p], lane_lo)) for p in prange]
        for msk in (off16, jnp.logical_not(blk32)):
            te = [_mm1(t[p], _bd(jnp.where(msk, a_ab[p], 0.0), lane_lo)) for p in prange]
            t = [t[p] + _mm1(te[p], _bd(t[p], lane_lo)) for p in prange]

        v_r = [_bd(vc[p], lane_lo) for p in prange]
        pv = [_mm1(a_ak[p], v_r[p]) for p in prange]
        wu = [_mm1(t[p], jnp.concatenate([_bd(a_t[p], lane_lo), _bd(pv[p], lane_lo)], axis=1))
              for p in prange]

        for p in prange:
            bi, pi = units[p]
            s0 = s_ref[p]
            ws = lax.dot_general(jnp.concatenate([wu[p][:, :LANES], r_t[p]], axis=0).astype(BF16),
                                 s0.astype(BF16), nt_dims, preferred_element_type=F32)
            u_mat = ws[:c] + wu[p][:, LANES:]
            y = ws[c:] + _mm1(jnp.concatenate([q_b[p], q_k[p]], axis=1),
                              jnp.concatenate([_bd(u_mat, lane_lo), v_r[p]], axis=0))
            y_ref[rows[bi], pi * LANES:(pi + 1) * LANES] = y
            ds = _mm1(jnp.concatenate([u_mat, vc[p]], axis=0).T,
                      jnp.concatenate([b_h[p], k_h[p]], axis=0).astype(BF16))
            s_ref[p] = s0 * g_c[p] + jnp.where(bd_mask, ds, 0.0)
        return carry

    lax.fori_loop(0, tt // c, chunk, 0)

    y = y_ref[...]
    mean = _seg_sum(y, seg_ones) * (1.0 / n)
    dlt = y - mean
    var = _seg_sum(dlt * dlt, seg_ones) * (1.0 / n)
    yn = dlt * lax.rsqrt(var + RWKV_LN_EPS) * lng_ref[...] + lnb_ref[...]
    bonus = _seg_sum(st_ref[0] * st_ref[2] * rk_ref[...], seg_ones) * st_ref[3]
    gate = _dot(_sigmoid(cols(o_gl, o_gl + RWKV_GATE_RANK)), g2_ref[...])
    o_ref[...] = ((yn + bonus) * gate).reshape(nb, tt, w).astype(o_ref.dtype)


def _rwkv(zr, w0, w2, a0, a2, g2, k_k, k_a, r_k, ln_g, ln_b, tt):
    b, t, nz = zr.shape
    w = w0.shape[-1]
    const = lambda *shape: _const_spec(shape, 1)
    return pl.pallas_call(
        _rwkv_kernel,
        grid=(t // tt,),
        in_specs=[pl.BlockSpec((b, tt, nz), lambda j: (0, j, 0)),
                  const(1, w), const(*w2.shape), const(1, w), const(*a2.shape), const(*g2.shape),
                  const(1, w), const(1, w), const(1, w), const(1, w), const(1, w)],
        out_specs=pl.BlockSpec((b, tt, w), lambda j: (0, j, 0)),
        out_shape=jax.ShapeDtypeStruct((b, t, w), BF16),
        scratch_shapes=[pltpu.VMEM((b * w // LANES, LANES, LANES), F32),
                        pltpu.VMEM((6, b * tt, w), F32),
                        pltpu.VMEM((b * tt, w), F32)],
        compiler_params=_params(1),
        name="rwkv",
    )(zr, w0, w2, a0, a2, g2, k_k, k_a, r_k, ln_g, ln_b)


def _post_attn_kernel(x_ref, og_ref, or_ref, wog_ref, wor_ref, gx_ref, wq_ref, k_ref, v_ref, wo_ref,
                      wgu_ref, wd_ref, out_ref, wgu_out_ref, wd_out_ref):
    wgu_out_ref[...] = wgu_ref[...].astype(BF16)
    wd_out_ref[...] = wd_ref[...].astype(BF16)
    d = x_ref.shape[-1]
    hd = d // MEM_HEADS
    x1 = (x_ref[...] + jnp.dot(og_ref[...], wog_ref[...], preferred_element_type=F32)
          + jnp.dot(or_ref[...], wor_ref[...], preferred_element_type=F32))
    q = _dot(_rms_norm(x1, gx_ref[...], NORM_EPS), wq_ref[...]).astype(BF16)
    heads = []
    for h in range(MEM_HEADS):
        hs = slice(h * hd, (h + 1) * hd)
        s = lax.dot_general(q[:, hs], k_ref[:, hs], (((1,), (1,)), ((), ())),
                            preferred_element_type=F32) * (hd ** -0.5)
        e = jnp.exp(s - jnp.max(s, axis=-1, keepdims=True))
        p = e / jnp.sum(e, axis=-1, keepdims=True)
        heads.append(_dot(p, v_ref[:, hs]))
    o = jnp.concatenate(heads, axis=-1)
    out_ref[...] = x1 + _dot(o, wo_ref[...])


def _post_attn(x, o_gla, o_rwkv, wo_g, wo_r, gx, wq, kmem, vmem, wo, wgu, wd, tm):
    b, t, d = x.shape
    m = kmem.shape[1]
    nj = t // tm
    steps = b * nj
    bf16_rows = 16
    assert wgu.shape[0] % (steps * bf16_rows) == 0 and wd.shape[0] % (steps * bf16_rows) == 0
    const = lambda *shape: _const_spec(shape, 2)
    tile = lambda width: pl.BlockSpec((None, tm, width), lambda i, j: (i, j, 0))
    slab = lambda w: pl.BlockSpec((w.shape[0] // steps, w.shape[1]), lambda i, j: (i * nj + j, 0))
    return pl.pallas_call(
        _post_attn_kernel,
        grid=(b, nj),
        in_specs=[tile(d), tile(o_gla.shape[-1]), tile(o_rwkv.shape[-1]),
                  const(*wo_g.shape), const(*wo_r.shape), const(1, d), const(*wq.shape),
                  pl.BlockSpec((None, m, d), lambda i, j: (i, 0, 0)),
                  pl.BlockSpec((None, m, d), lambda i, j: (i, 0, 0)),
                  const(*wo.shape), slab(wgu), slab(wd)],
        out_specs=(tile(d), slab(wgu), slab(wd)),
        out_shape=(jax.ShapeDtypeStruct((b, t, d), F32), jax.ShapeDtypeStruct(wgu.shape, BF16),
                   jax.ShapeDtypeStruct(wd.shape, BF16)),
        compiler_params=_params(2),
        name="post_attn",
    )(x, o_gla, o_rwkv, wo_g, wo_r, gx, wq, kmem, vmem, wo, wgu, wd)


def _ffn_kernel(x_ref, g_ref, wgu_ref, wd_ref, gf_ref, out_ref, *, ff_chunk):
    x = x_ref[...]
    d_ff = wd_ref.shape[0]
    h = _rms_norm(x, g_ref[...], NORM_EPS).astype(BF16)
    acc = x
    for c0 in range(0, d_ff, ff_chunk):
        gate = jnp.dot(h, wgu_ref[:, c0:c0 + ff_chunk], preferred_element_type=F32)
        up = jnp.dot(h, wgu_ref[:, d_ff + c0:d_ff + c0 + ff_chunk], preferred_element_type=F32)
        act = (gate * _sigmoid(gate) * up).astype(BF16)
        acc = acc + jnp.dot(act, wd_ref[c0:c0 + ff_chunk, :], preferred_element_type=F32)
    out_ref[...] = _rms_norm(acc, gf_ref[...], NORM_EPS)


def _ffn(x, g, wgu, wd, gf, tm, ff_chunk):
    b, t, d = x.shape
    const = lambda *shape: _const_spec(shape, 2)
    tile = pl.BlockSpec((None, tm, d), lambda i, j: (i, j, 0))
    return pl.pallas_call(
        functools.partial(_ffn_kernel, ff_chunk=ff_chunk),
        grid=(b, t // tm),
        in_specs=[tile, const(1, d), const(*wgu.shape), const(*wd.shape), const(1, d)],
        out_specs=tile,
        out_shape=jax.ShapeDtypeStruct((b, t, d), F32),
        compiler_params=_params(2),
        name="ffn",
    )(x, g, wgu, wd, gf)


def _layer(x, mem, norm_mix_g, w_in, gla_wa2, gla_ba, gla_norm_g, rwkv_mu, rwkv_w0, rwkv_w2, rwkv_a0,
           rwkv_a2, rwkv_g2, rwkv_k_k, rwkv_k_a, rwkv_r_k, rwkv_ln_g, rwkv_ln_b, w_out, norm_mem_x_g,
           norm_mem_g, wq_mem, wkv_mem, wo_mem, norm_ffn_g, w_gate_up, w_down, final_g):
    b, t, d = x.shape
    m = mem.shape[1]
    row = lambda p: p.reshape(1, -1).astype(F32)
    vw = GLA_HEADS * GLA_DV
    wa2p = jnp.pad(gla_wa2, ((0, LANES - GLA_GATE_RANK), (0, 0))).astype(BF16)

    kmem, vmem = _mem_kv(mem.reshape(b * m, d), row(norm_mem_g), wkv_mem)
    zg, zr = _in_proj(x, row(norm_mix_g), w_in.T, wa2p, row(gla_ba), row(rwkv_mu), tm=min(ROWS_IN_PROJ, t))
    o_gla = _gla(zg, row(gla_norm_g), tt=min(ROWS_GLA, t))
    o_rwkv = _rwkv(zr, row(rwkv_w0), rwkv_w2.astype(BF16), row(rwkv_a0), rwkv_a2.astype(BF16),
                   rwkv_g2.astype(BF16), row(rwkv_k_k), row(rwkv_k_a), row(rwkv_r_k), row(rwkv_ln_g),
                   row(rwkv_ln_b), tt=min(ROWS_RWKV, t))
    w_out_b = w_out.astype(BF16)
    x2, wgu_b, wd_b = _post_attn(x, o_gla, o_rwkv, w_out_b[:vw], w_out_b[vw:], row(norm_mem_x_g),
                                 wq_mem.astype(BF16), kmem.reshape(b, m, d), vmem.reshape(b, m, d),
                                 wo_mem.astype(BF16), w_gate_up, w_down, tm=min(ROWS_POST_ATTN, t))
    return _ffn(x2, row(norm_ffn_g), wgu_b, wd_b, row(final_g), tm=min(ROWS_FFN, t), ff_chunk=FF_CHUNK)


def kernel(x, mem, norm_mix_g, w_in, gla_wa2, gla_ba, gla_norm_g, rwkv_mu, rwkv_w0, rwkv_w2, rwkv_a0, rwkv_a2, rwkv_g2, rwkv_k_k, rwkv_k_a, rwkv_r_k, rwkv_ln_g, rwkv_ln_b, w_out, norm_mem_x_g, norm_mem_g, wq_mem, wkv_mem, wo_mem, norm_ffn_g, w_gate_up, w_down, norm_final_g):
    assert norm_mix_g.shape[0] == 1, "single-layer block"
    return _layer(x, mem, norm_mix_g[0], w_in[0], gla_wa2[0], gla_ba[0], gla_norm_g[0], rwkv_mu[0],
                  rwkv_w0[0], rwkv_w2[0], rwkv_a0[0], rwkv_a2[0], rwkv_g2[0], rwkv_k_k[0], rwkv_k_a[0],
                  rwkv_r_k[0], rwkv_ln_g[0], rwkv_ln_b[0], w_out[0], norm_mem_x_g[0], norm_mem_g[0],
                  wq_mem[0], wkv_mem[0], wo_mem[0], norm_ffn_g[0], w_gate_up[0], w_down[0], norm_final_g)
```

```python
import functools
import math

import jax
import jax.numpy as jnp
from jax import lax
from jax.experimental import pallas as pl
from jax.experimental.pallas import tpu as pltpu

F32 = jnp.float32
BF16 = jnp.bfloat16

MEM_HEADS = 4
GLA_HEADS = 4
GLA_DK = 64
GLA_DV = 128
GLA_GATE_RANK = 16
GLA_GATE_NORMALIZER = 16.0
GLA_NORM_EPS = 1e-5
RWKV_HEAD = 64
RWKV_DECAY_RANK = 64
RWKV_AAA_RANK = 64
RWKV_GATE_RANK = 128
RWKV_LN_EPS = 64e-5
NORM_EPS = 1e-6

CHUNK = 64
GLA_CHUNKS_PER_ITER = 2
GLA_UNROLL = 2
RWKV_UNROLL = 4
SUBTILE_ROWS = 256
ROWS_IN_PROJ = 512
ROWS_GLA = 1024
ROWS_RWKV = 512
ROWS_POST_ATTN = 1024
ROWS_FFN = 1024
FF_CHUNK = 256
LANES = 128
VMEM_LIMIT = 56 * 1024 * 1024


def _params(n_grid_dims):
    return pltpu.CompilerParams(dimension_semantics=("arbitrary",) * n_grid_dims,
                                vmem_limit_bytes=VMEM_LIMIT)


def _const_spec(shape, grid_rank):
    zeros = (0,) * len(shape)
    index_map = (lambda j: zeros) if grid_rank == 1 else (lambda i, j: zeros)
    return pl.BlockSpec(shape, index_map, pipeline_mode=pl.Buffered(1))


def _dot(a, b):
    return jnp.dot(a.astype(BF16), b.astype(BF16), preferred_element_type=F32)


def _rms_norm(x, g, eps):
    return x * lax.rsqrt(jnp.mean(x * x, axis=-1, keepdims=True) + eps) * g


def _sigmoid(x):
    return 1.0 / (1.0 + jnp.exp(-x))


def _log_sigmoid(x):
    return jnp.minimum(x, 0.0) - jnp.log1p(jnp.exp(-jnp.abs(x)))


def _iota2(shape, dim):
    return lax.broadcasted_iota(jnp.int32, shape, dim)


def _cumsum_operator(c):
    return (_iota2((c, 3 * c), 0) >= _iota2((c, 3 * c), 1) % c).astype(BF16)


def _chunk_cumsum(tri3, x):
    p1 = x.astype(BF16)
    rem = x - p1.astype(F32)
    p2 = rem.astype(BF16)
    p3 = (rem - p2.astype(F32)).astype(BF16)
    return jnp.dot(tri3, jnp.concatenate([p1, p2, p3], axis=0), preferred_element_type=F32)


def _mem_kv_kernel(mem_ref, g_ref, w_ref, k_ref, v_ref):
    d = k_ref.shape[-1]
    m = _rms_norm(mem_ref[...], g_ref[...], NORM_EPS)
    kv = _dot(m, w_ref[...])
    k_ref[...] = kv[:, :d].astype(BF16)
    v_ref[...] = kv[:, d:].astype(BF16)


def _mem_kv(mem2d, g, wkv):
    n, d = mem2d.shape
    return pl.pallas_call(
        _mem_kv_kernel,
        out_shape=(jax.ShapeDtypeStruct((n, d), BF16), jax.ShapeDtypeStruct((n, d), BF16)),
        compiler_params=pltpu.CompilerParams(vmem_limit_bytes=VMEM_LIMIT),
        name="mem_kv",
    )(mem2d, g, wkv)


def _in_proj_kernel(x_ref, g_ref, w_ref, wa2_ref, ba_ref, mu_ref, zg_ref, zr_ref, prev_ref, wg_ref, wr_ref):
    tm = x_ref.shape[0]
    n_main = zg_ref.shape[-1] - wa2_ref.shape[-1]
    n_gla = n_main + GLA_GATE_RANK

    @pl.when(jnp.logical_and(pl.program_id(0) == 0, pl.program_id(1) == 0))
    def _():
        for dst, c_first in ((wg_ref, 0), (wr_ref, n_gla)):
            for c0 in range(0, dst.shape[1], LANES):
                slab = w_ref[c_first + c0:c_first + c0 + LANES, :]
                dst[:, c0:c0 + LANES] = slab.T.astype(BF16)

    @pl.when(pl.program_id(1) == 0)
    def _():
        prev_ref[...] = jnp.zeros_like(prev_ref)

    sub = min(SUBTILE_ROWS, tm)
    prev = prev_ref[0:1, :]
    for s in range(tm // sub):
        rows = slice(s * sub, (s + 1) * sub)
        h = _rms_norm(x_ref[rows, :], g_ref[...], NORM_EPS).astype(BF16)

        zr = jnp.dot(h, wr_ref[...], preferred_element_type=F32)
        rolled = pltpu.roll(zr, shift=1, axis=0)
        first = _iota2(zr.shape, 0) == 0
        shifted = jnp.where(first, jnp.broadcast_to(prev, zr.shape), rolled)
        zr_ref[rows, :] = zr + (shifted - zr) * mu_ref[...]
        prev = zr[sub - 1:sub, :]

        a_lo = jnp.dot(h, wg_ref[:, n_main:], preferred_element_type=F32)
        pre = _dot(a_lo, wa2_ref[...]) + ba_ref[...]
        zg_ref[rows, n_main:] = _log_sigmoid(pre) * (1.0 / GLA_GATE_NORMALIZER)
        zg_ref[rows, :n_main] = jnp.dot(h, wg_ref[:, :n_main], preferred_element_type=F32)
    prev_ref[0:1, :] = prev


def _in_proj(x, g, w_in_t, wa2p, ba, mu, tm):
    b, t, d = x.shape
    nr = mu.shape[1]
    n_main = w_in_t.shape[0] - nr - GLA_GATE_RANK
    ng = n_main + wa2p.shape[1]
    const = lambda *shape: _const_spec(shape, 2)
    return pl.pallas_call(
        _in_proj_kernel,
        grid=(b, t // tm),
        in_specs=[
            pl.BlockSpec((None, tm, d), lambda i, j: (i, j, 0)),
            const(1, d), const(*w_in_t.shape), const(*wa2p.shape),
            const(1, wa2p.shape[1]), const(1, nr),
        ],
        out_specs=(pl.BlockSpec((None, tm, ng), lambda i, j: (i, j, 0)),
                   pl.BlockSpec((None, tm, nr), lambda i, j: (i, j, 0))),
        out_shape=(jax.ShapeDtypeStruct((b, t, ng), F32), jax.ShapeDtypeStruct((b, t, nr), F32)),
        scratch_shapes=[pltpu.VMEM((8, nr), F32),
                        pltpu.VMEM((d, n_main + LANES), BF16),
                        pltpu.VMEM((d, nr), BF16)],
        compiler_params=_params(2),
        name="in_proj",
    )(x, g, w_in_t, wa2p, ba, mu)


def _gla_kernel(z_ref, ng_ref, o_ref, s_ref):
    nb, tt, _ = z_ref.shape
    kw = GLA_HEADS * GLA_DK
    vw = GLA_HEADS * GLA_DV
    c = CHUNK
    pairs = GLA_HEADS // 2
    pk = 2 * GLA_DK
    pv = 2 * GLA_DV
    cpi = GLA_CHUNKS_PER_ITER

    @pl.when(pl.program_id(0) == 0)
    def _():
        s_ref[...] = jnp.zeros_like(s_ref)

    tri3 = _cumsum_operator(c)
    causal = _iota2((c, pk), 0) >= _iota2((c, pk), 1) % c
    lane_lo = _iota2((c, pk), 1) < GLA_DK
    v_lo = _iota2((c, pv), 1) < GLA_DV
    s_mask = (_iota2((pv, pk), 0) // GLA_DV) == (_iota2((pv, pk), 1) // GLA_DK)
    norm_g = ng_ref[...]
    nt_dims = (((1,), (1,)), ((), ()))
    tn_dims = (((0,), (0,)), ((), ()))

    def step(it, carry):
        units = [(b, j, p) for b in range(nb) for j in range(cpi) for p in range(pairs)]
        rows = {(b, j): pl.ds(pl.multiple_of((it * cpi + j) * c, c), c) for b in range(nb) for j in range(cpi)}
        q_dec, k_inv, k_tail, dec, vv = {}, {}, {}, {}, {}
        for (b, j), rs in rows.items():
            q = z_ref[b, rs, 0:kw] * (GLA_DK ** -0.5)
            k = z_ref[b, rs, kw:2 * kw]
            log_a = z_ref[b, rs, 2 * kw + 2 * vw:3 * kw + 2 * vw]
            bcum = _chunk_cumsum(tri3, log_a)
            blast = bcum[c - 1:c, :]
            qd = q * jnp.exp(bcum)
            ki = k * jnp.exp(-bcum)
            kt = k * jnp.exp(blast - bcum)
            dc = jnp.exp(blast)
            for p in range(pairs):
                ks = slice(p * pk, (p + 1) * pk)
                q_dec[b, j, p] = qd[:, ks].astype(BF16)
                k_inv[b, j, p] = ki[:, ks]
                k_tail[b, j, p] = kt[:, ks].astype(BF16)
                dec[b, j, p] = dc[:, ks]
                vv[b, j, p] = z_ref[b, rs, 2 * kw + p * pv:2 * kw + (p + 1) * pv].astype(BF16)

        o_intra, d_state = {}, {}
        for u in units:
            scores = jnp.where(causal, lax.dot_general(q_dec[u], _bd(k_inv[u], lane_lo), nt_dims,
                                                       preferred_element_type=F32), 0.0)
            zero = jnp.zeros_like(vv[u])
            v_bd = jnp.concatenate([jnp.where(v_lo, vv[u], zero), jnp.where(v_lo, zero, vv[u])], axis=0)
            o_intra[u] = _mm1(scores, v_bd)
            d_state[u] = jnp.where(s_mask, lax.dot_general(vv[u], k_tail[u], tn_dims,
                                                           preferred_element_type=F32), 0.0)

        for b in range(nb):
            s_cur = [s_ref[b * pairs + p] for p in range(pairs)]
            for j in range(cpi):
                outs = []
                for p in range(pairs):
                    u = (b, j, p)
                    o = o_intra[u] + lax.dot_general(q_dec[u], s_cur[p].astype(BF16), nt_dims,
                                                     preferred_element_type=F32)
                    s_cur[p] = s_cur[p] * dec[u] + d_state[u]
                    for h in range(2):
                        oh = o[:, h * GLA_DV:(h + 1) * GLA_DV]
                        oh = oh * lax.rsqrt(jnp.mean(oh * oh, axis=-1, keepdims=True) + GLA_NORM_EPS) * norm_g
                        g0 = 2 * kw + vw + (2 * p + h) * GLA_DV
                        gh = z_ref[b, rows[b, j], g0:g0 + GLA_DV]
                        outs.append(oh * (gh * _sigmoid(gh)))
                o_ref[b, rows[b, j], :] = jnp.concatenate(outs, axis=-1).astype(o_ref.dtype)
            for p in range(pairs):
                s_ref[b * pairs + p] = s_cur[p]
        return carry

    lax.fori_loop(0, tt // (c * cpi), step, 0, unroll=min(GLA_UNROLL, tt // (c * cpi)))


def _gla(zg, norm_g, tt):
    b, t, n = zg.shape
    vw = GLA_HEADS * GLA_DV
    return pl.pallas_call(
        _gla_kernel,
        grid=(t // tt,),
        in_specs=[pl.BlockSpec((b, tt, n), lambda j: (0, j, 0)),
                  _const_spec((1, GLA_DV), 1)],
        out_specs=pl.BlockSpec((b, tt, vw), lambda j: (0, j, 0)),
        out_shape=jax.ShapeDtypeStruct((b, t, vw), BF16),
        scratch_shapes=[pltpu.VMEM((b * GLA_HEADS // 2, 2 * GLA_DV, 2 * GLA_DK), F32)],
        compiler_params=_params(1),
        name="gla",
    )(zg, norm_g)


def _seg_sum(x, seg_ones):
    xb = x.astype(BF16)
    blk = seg_ones.shape[0]
    parts = []
    for s in range(x.shape[1] // blk):
        sl = slice(s * blk, (s + 1) * blk)
        parts.append(jnp.dot(xb[:, sl], seg_ones, preferred_element_type=F32))
    return jnp.concatenate(parts, axis=-1)


def _block_diag(z, lane_lo):
    zero = jnp.zeros_like(z)
    return jnp.concatenate([jnp.where(lane_lo, z, zero), jnp.where(lane_lo, zero, z)], axis=0)


def _bd(z, lane_lo):
    return _block_diag(z.astype(BF16), lane_lo)


def _mm1(x, r):
    return jnp.dot(x.astype(BF16), r, preferred_element_type=F32)


def _rwkv_kernel(z_ref, w0_ref, w2_ref, a0_ref, a2_ref, g2_ref, kk_ref, ka_ref, rk_ref, lng_ref, lnb_ref,
                 o_ref, s_ref, st_ref, y_ref):
    nb, tt, _ = z_ref.shape
    w = w0_ref.shape[-1]
    n = RWKV_HEAD
    pairs = w // LANES
    c = CHUNK
    o_wl = 3 * w
    o_al = o_wl + RWKV_DECAY_RANK
    o_gl = o_al + RWKV_AAA_RANK

    @pl.when(pl.program_id(0) == 0)
    def _():
        s_ref[...] = jnp.zeros_like(s_ref)

    seg = 2 * LANES
    seg_ones = (_iota2((seg, seg), 0) // n == _iota2((seg, seg), 1) // n).astype(BF16)

    cols = lambda lo, hi: z_ref[:, :, lo:hi].reshape(nb * tt, hi - lo)
    r = cols(0, w)
    k = cols(w, 2 * w)
    v = cols(2 * w, 3 * w)
    u = w0_ref[...] + _dot(jnp.tanh(cols(o_wl, o_al)), w2_ref[...])
    lw = -math.exp(-0.5) * _sigmoid(u)
    a = _sigmoid(a0_ref[...] + _dot(cols(o_al, o_gl), a2_ref[...]))
    kk = k * kk_ref[...]
    kk = kk * lax.rsqrt(jnp.maximum(_seg_sum(kk * kk, seg_ones), 1e-24))
    k2 = k * (1.0 + (a - 1.0) * ka_ref[...])
    st_ref[0] = r
    st_ref[1] = lw
    st_ref[2] = k2
    st_ref[3] = v
    st_ref[4] = -kk
    st_ref[5] = kk * a

    tri3 = _cumsum_operator(c)
    row = _iota2((c, LANES), 0)
    col = _iota2((c, LANES), 1) % c
    lower = row >= col
    strict = row > col
    eye = (row == col).astype(F32)
    blk16 = (row // 16) == (col // 16)
    blk32 = (row // 32) == (col // 32)
    off16 = jnp.logical_and(blk32, jnp.logical_not(blk16))
    lane_lo = _iota2((c, LANES), 1) < n
    bd_mask = (_iota2((LANES, LANES), 0) // n) == (_iota2((LANES, LANES), 1) // n)
    units = [(b, p) for b in range(nb) for p in range(pairs)]
    prange = range(len(units))
    nt_dims = (((1,), (1,)), ((), ()))

    def chunk(ci, carry):
        rows = [pl.ds(pl.multiple_of(b * tt + ci * c, c), c) for b in range(nb)]
        a_t, r_t, b_t, k_t, b_h, k_h, vc, g_c = [], [], [], [], [], [], [], []
        for b in range(nb):
            rc = st_ref[0, rows[b], :]
            lwc = st_ref[1, rows[b], :]
            kc = st_ref[2, rows[b], :]
            ac = st_ref[4, rows[b], :]
            bc = st_ref[5, rows[b], :]
            cum = _chunk_cumsum(tri3, lwc)
            last = cum[c - 1:c, :]
            e_neg = jnp.exp(-cum)
            e_tail = jnp.exp(last - cum)
            r_tb = rc * jnp.exp(cum)
            a_tb = ac * jnp.exp(cum - lwc)
            vb = st_ref[3, rows[b], :]
            for p in range(pairs):
                sl = slice(p * LANES, (p + 1) * LANES)
                a_t.append(a_tb[:, sl])
                r_t.append(r_tb[:, sl])
                b_t.append(bc[:, sl] * e_neg[:, sl])
                k_t.append(kc[:, sl] * e_neg[:, sl])
                b_h.append(bc[:, sl] * e_tail[:, sl])
                k_h.append(kc[:, sl] * e_tail[:, sl])
                vc.append(vb[:, sl])
                g_c.append(jnp.exp(last[:, sl]))

        a_ab, a_ak, q_b, q_k = [], [], [], []
        for p in prange:
            lhs = jnp.concatenate([a_t[p], r_t[p]], axis=0).astype(BF16)
            rhs = jnp.concatenate([_bd(b_t[p], lane_lo), _bd(k_t[p], lane_lo)], axis=0)
            aa = lax.dot_general(lhs, rhs, nt_dims, preferred_element_type=F32)
            a_ab.append(jnp.where(strict, aa[:c, :LANES], 0.0))
            a_ak.append(jnp.where(strict, aa[:c, LANES:], 0.0))
            q_b.append(jnp.where(lower, aa[c:, :LANES], 0.0))
            q_k.append(jnp.where(lower, aa[c:, LANES:], 0.0))

        ad = [jnp.where(blk16, x, 0.0) for x in a_ab]
        pw = [_mm1(ad[p], _bd(ad[p], lane_lo)) for p in prange]
        t = [eye + ad[p] for p in prange]
        for _ in range(2):
            both = [_mm1(jnp.concatenate([pw[p], t[p]], axis=0), _bd(pw[p], lane_lo)) for p in prange]
            pw = [x[:c] for x in both]
            t = [t[p] + both[p][c:] for p in prange]
        t = [t[p] + _mm1(t[p], _bd(pw[p], lane_lo)) for p in prange]
        for msk in (off16, jnp.logical_not(blk32)):
            te = [_mm1(t[p], _bd(jnp.where(msk, a_ab[p], 0.0), lane_lo)) for p in prange]
            t = [t[p] + _mm1(te[p], _bd(t[p], lane_lo)) for p in prange]

        v_r = [_bd(vc[p], lane_lo) for p in prange]
        pv = [_mm1(a_ak[p], v_r[p]) for p in prange]
        wu = [_mm1(t[p], jnp.concatenate([_bd(a_t[p], lane_lo), _bd(pv[p], lane_lo)], axis=1))
              for p in prange]

        for p in prange:
            bi, pi = units[p]
            s0 = s_ref[p]
            ws = lax.dot_general(jnp.concatenate([wu[p][:, :LANES], r_t[p]], axis=0).astype(BF16),
                                 s0.astype(BF16), nt_dims, preferred_element_type=F32)
            u_mat = ws[:c] + wu[p][:, LANES:]
            y = ws[c:] + _mm1(jnp.concatenate([q_b[p], q_k[p]], axis=1),
                              jnp.concatenate([_bd(u_mat, lane_lo), v_r[p]], axis=0))
            y_ref[rows[bi], pi * LANES:(pi + 1) * LANES] = y
            ds = _mm1(jnp.concatenate([u_mat, vc[p]], axis=0).T,
                      jnp.concatenate([b_h[p], k_h[p]], axis=0).astype(BF16))
            s_ref[p] = s0 * g_c[p] + jnp.where(bd_mask, ds, 0.0)
        return carry

    lax.fori_loop(0, tt // c, chunk, 0, unroll=min(RWKV_UNROLL, tt // c))

    y = y_ref[...]
    mean = _seg_sum(y, seg_ones) * (1.0 / n)
    dlt = y - mean
    var = _seg_sum(dlt * dlt, seg_ones) * (1.0 / n)
    yn = dlt * lax.rsqrt(var + RWKV_LN_EPS) * lng_ref[...] + lnb_ref[...]
    bonus = _seg_sum(st_ref[0] * st_ref[2] * rk_ref[...], seg_ones) * st_ref[3]
    gate = _dot(_sigmoid(cols(o_gl, o_gl + RWKV_GATE_RANK)), g2_ref[...])
    o_ref[...] = ((yn + bonus) * gate).reshape(nb, tt, w).astype(o_ref.dtype)


def _rwkv(zr, w0, w2, a0, a2, g2, k_k, k_a, r_k, ln_g, ln_b, tt):
    b, t, nz = zr.shape
    w = w0.shape[-1]
    const = lambda *shape: _const_spec(shape, 1)
    return pl.pallas_call(
        _rwkv_kernel,
        grid=(t // tt,),
        in_specs=[pl.BlockSpec((b, tt, nz), lambda j: (0, j, 0)),
                  const(1, w), const(*w2.shape), const(1, w), const(*a2.shape), const(*g2.shape),
                  const(1, w), const(1, w), const(1, w), const(1, w), const(1, w)],
        out_specs=pl.BlockSpec((b, tt, w), lambda j: (0, j, 0)),
        out_shape=jax.ShapeDtypeStruct((b, t, w), BF16),
        scratch_shapes=[pltpu.VMEM((b * w // LANES, LANES, LANES), F32),
                        pltpu.VMEM((6, b * tt, w), F32),
                        pltpu.VMEM((b * tt, w), F32)],
        compiler_params=_params(1),
        name="rwkv",
    )(zr, w0, w2, a0, a2, g2, k_k, k_a, r_k, ln_g, ln_b)


def _post_attn_kernel(x_ref, og_ref, or_ref, wog_ref, wor_ref, gx_ref, wq_ref, k_ref, v_ref, wo_ref,
                      wgu_ref, wd_ref, out_ref, wgu_out_ref, wd_out_ref):
    wgu_out_ref[...] = wgu_ref[...].astype(BF16)
    wd_out_ref[...] = wd_ref[...].astype(BF16)
    d = x_ref.shape[-1]
    hd = d // MEM_HEADS
    x1 = (x_ref[...] + jnp.dot(og_ref[...], wog_ref[...], preferred_element_type=F32)
          + jnp.dot(or_ref[...], wor_ref[...], preferred_element_type=F32))
    q = _dot(_rms_norm(x1, gx_ref[...], NORM_EPS), wq_ref[...]).astype(BF16)
    heads = []
    for h in range(MEM_HEADS):
        hs = slice(h * hd, (h + 1) * hd)
        s = lax.dot_general(q[:, hs], k_ref[:, hs], (((1,), (1,)), ((), ())),
                            preferred_element_type=F32) * (hd ** -0.5)
        e = jnp.exp(s - jnp.max(s, axis=-1, keepdims=True))
        p = e / jnp.sum(e, axis=-1, keepdims=True)
        heads.append(_dot(p, v_ref[:, hs]))
    o = jnp.concatenate(heads, axis=-1)
    out_ref[...] = x1 + _dot(o, wo_ref[...])


def _post_attn(x, o_gla, o_rwkv, wo_g, wo_r, gx, wq, kmem, vmem, wo, wgu, wd, tm):
    b, t, d = x.shape
    m = kmem.shape[1]
    nj = t // tm
    steps = b * nj
    bf16_rows = 16
    assert wgu.shape[0] % (steps * bf16_rows) == 0 and wd.shape[0] % (steps * bf16_rows) == 0
    const = lambda *shape: _const_spec(shape, 2)
    tile = lambda width: pl.BlockSpec((None, tm, width), lambda i, j: (i, j, 0))
    slab = lambda w: pl.BlockSpec((w.shape[0] // steps, w.shape[1]), lambda i, j: (i * nj + j, 0))
    return pl.pallas_call(
        _post_attn_kernel,
        grid=(b, nj),
        in_specs=[tile(d), tile(o_gla.shape[-1]), tile(o_rwkv.shape[-1]),
                  const(*wo_g.shape), const(*wo_r.shape), const(1, d), const(*wq.shape),
                  pl.BlockSpec((None, m, d), lambda i, j: (i, 0, 0)),
                  pl.BlockSpec((None, m, d), lambda i, j: (i, 0, 0)),
                  const(*wo.shape), slab(wgu), slab(wd)],
        out_specs=(tile(d), slab(wgu), slab(wd)),
        out_shape=(jax.ShapeDtypeStruct((b, t, d), F32), jax.ShapeDtypeStruct(wgu.shape, BF16),
                   jax.ShapeDtypeStruct(wd.shape, BF16)),
        compiler_params=_params(2),
        name="post_attn",
    )(x, o_gla, o_rwkv, wo_g, wo_r, gx, wq, kmem, vmem, wo, wgu, wd)


def _ffn_kernel(x_ref, g_ref, wgu_ref, wd_ref, gf_ref, out_ref, *, ff_chunk):
    x = x_ref[...]
    d_ff = wd_ref.shape[0]
    h = _rms_norm(x, g_ref[...], NORM_EPS).astype(BF16)
    acc = x
    for c0 in range(0, d_ff, ff_chunk):
        gate = jnp.dot(h, wgu_ref[:, c0:c0 + ff_chunk], preferred_element_type=F32)
        up = jnp.dot(h, wgu_ref[:, d_ff + c0:d_ff + c0 + ff_chunk], preferred_element_type=F32)
        act = (gate * _sigmoid(gate) * up).astype(BF16)
        acc = acc + jnp.dot(act, wd_ref[c0:c0 + ff_chunk, :], preferred_element_type=F32)
    out_ref[...] = _rms_norm(acc, gf_ref[...], NORM_EPS)


def _ffn(x, g, wgu, wd, gf, tm, ff_chunk):
    b, t, d = x.shape
    const = lambda *shape: _const_spec(shape, 2)
    tile = pl.BlockSpec((None, tm, d), lambda i, j: (i, j, 0))
    return pl.pallas_call(
        functools.partial(_ffn_kernel, ff_chunk=ff_chunk),
        grid=(b, t // tm),
        in_specs=[tile, const(1, d), const(*wgu.shape), const(*wd.shape), const(1, d)],
        out_specs=tile,
        out_shape=jax.ShapeDtypeStruct((b, t, d), F32),
        compiler_params=_params(2),
        name="ffn",
    )(x, g, wgu, wd, gf)


def _layer(x, mem, norm_mix_g, w_in, gla_wa2, gla_ba, gla_norm_g, rwkv_mu, rwkv_w0, rwkv_w2, rwkv_a0,
           rwkv_a2, rwkv_g2, rwkv_k_k, rwkv_k_a, rwkv_r_k, rwkv_ln_g, rwkv_ln_b, w_out, norm_mem_x_g,
           norm_mem_g, wq_mem, wkv_mem, wo_mem, norm_ffn_g, w_gate_up, w_down, final_g):
    b, t, d = x.shape
    m = mem.shape[1]
    row = lambda p: p.reshape(1, -1).astype(F32)
    vw = GLA_HEADS * GLA_DV
    wa2p = jnp.pad(gla_wa2, ((0, LANES - GLA_GATE_RANK), (0, 0))).astype(BF16)

    kmem, vmem = _mem_kv(mem.reshape(b * m, d), row(norm_mem_g), wkv_mem)
    zg, zr = _in_proj(x, row(norm_mix_g), w_in.T, wa2p, row(gla_ba), row(rwkv_mu), tm=min(ROWS_IN_PROJ, t))
    o_gla = _gla(zg, row(gla_norm_g), tt=min(ROWS_GLA, t))
    o_rwkv = _rwkv(zr, row(rwkv_w0), rwkv_w2.astype(BF16), row(rwkv_a0), rwkv_a2.astype(BF16),
                   rwkv_g2.astype(BF16), row(rwkv_k_k), row(rwkv_k_a), row(rwkv_r_k), row(rwkv_ln_g),
                   row(rwkv_ln_b), tt=min(ROWS_RWKV, t))
    w_out_b = w_out.astype(BF16)
    x2, wgu_b, wd_b = _post_attn(x, o_gla, o_rwkv, w_out_b[:vw], w_out_b[vw:], row(norm_mem_x_g),
                                 wq_mem.astype(BF16), kmem.reshape(b, m, d), vmem.reshape(b, m, d),
                                 wo_mem.astype(BF16), w_gate_up, w_down, tm=min(ROWS_POST_ATTN, t))
    return _ffn(x2, row(norm_ffn_g), wgu_b, wd_b, row(final_g), tm=min(ROWS_FFN, t), ff_chunk=FF_CHUNK)


def kernel(x, mem, norm_mix_g, w_in, gla_wa2, gla_ba, gla_norm_g, rwkv_mu, rwkv_w0, rwkv_w2, rwkv_a0, rwkv_a2, rwkv_g2, rwkv_k_k, rwkv_k_a, rwkv_r_k, rwkv_ln_g, rwkv_ln_b, w_out, norm_mem_x_g, norm_mem_g, wq_mem, wkv_mem, wo_mem, norm_ffn_g, w_gate_up, w_down, norm_final_g):
    assert norm_mix_g.shape[0] == 1, "single-layer block"
    return _layer(x, mem, norm_mix_g[0], w_in[0], gla_wa2[0], gla_ba[0], gla_norm_g[0], rwkv_mu[0],
                  rwkv_w0[0], rwkv_w2[0], rwkv_a0[0], rwkv_a2[0], rwkv_g2[0], rwkv_k_k[0], rwkv_k_a[0],
                  rwkv_r_k[0], rwkv_ln_g[0], rwkv_ln_b[0], w_out[0], norm_mem_x_g[0], norm_mem_g[0],
                  wq_mem[0], wkv_mem[0], wo_mem[0], norm_ffn_g[0], w_gate_up[0], w_down[0], norm_final_g)
```

```python
import functools
import math

import jax
import jax.numpy as jnp
from jax import lax
from jax.experimental import pallas as pl
from jax.experimental.pallas import tpu as pltpu

F32 = jnp.float32
BF16 = jnp.bfloat16

MEM_HEADS = 4
GLA_HEADS = 4
GLA_DK = 64
GLA_DV = 128
GLA_GATE_RANK = 16
GLA_GATE_NORMALIZER = 16.0
GLA_NORM_EPS = 1e-5
RWKV_HEAD = 64
RWKV_DECAY_RANK = 64
RWKV_AAA_RANK = 64
RWKV_GATE_RANK = 128
RWKV_LN_EPS = 64e-5
NORM_EPS = 1e-6

CHUNK = 64
GLA_CHUNKS_PER_ITER = 2
GLA_UNROLL = 2
RWKV_UNROLL = 4
SUBTILE_ROWS = 256
POST_ATTN_SUBTILE_ROWS = 512
ROWS_IN_PROJ = 512
ROWS_GLA = 1024
ROWS_RWKV = 512
ROWS_POST_ATTN = 1024
ROWS_FFN = 1024
FF_CHUNK = 256
LANES = 128
VMEM_LIMIT = 56 * 1024 * 1024


def _params(n_grid_dims):
    return pltpu.CompilerParams(dimension_semantics=("arbitrary",) * n_grid_dims,
                                vmem_limit_bytes=VMEM_LIMIT)


def _const_spec(shape, grid_rank):
    zeros = (0,) * len(shape)
    index_map = (lambda j: zeros) if grid_rank == 1 else (lambda i, j: zeros)
    return pl.BlockSpec(shape, index_map, pipeline_mode=pl.Buffered(1))


def _dot(a, b):
    return jnp.dot(a.astype(BF16), b.astype(BF16), preferred_element_type=F32)


def _rms_norm(x, g, eps):
    return x * lax.rsqrt(jnp.mean(x * x, axis=-1, keepdims=True) + eps) * g


def _sigmoid(x):
    return 1.0 / (1.0 + jnp.exp(-x))


def _log_sigmoid(x):
    return jnp.minimum(x, 0.0) - jnp.log1p(jnp.exp(-jnp.abs(x)))


def _iota2(shape, dim):
    return lax.broadcasted_iota(jnp.int32, shape, dim)


def _cumsum_operator(c):
    return (_iota2((c, 3 * c), 0) >= _iota2((c, 3 * c), 1) % c).astype(BF16)


def _chunk_cumsum(tri3, x):
    p1 = x.astype(BF16)
    rem = x - p1.astype(F32)
    p2 = rem.astype(BF16)
    p3 = (rem - p2.astype(F32)).astype(BF16)
    return jnp.dot(tri3, jnp.concatenate([p1, p2, p3], axis=0), preferred_element_type=F32)


def _mem_kv_kernel(mem_ref, g_ref, w_ref, k_ref, v_ref):
    d = k_ref.shape[-1]
    m = _rms_norm(mem_ref[...], g_ref[...], NORM_EPS)
    kv = _dot(m, w_ref[...])
    k_ref[...] = kv[:, :d].astype(BF16)
    v_ref[...] = kv[:, d:].astype(BF16)


def _mem_kv(mem2d, g, wkv):
    n, d = mem2d.shape
    return pl.pallas_call(
        _mem_kv_kernel,
        out_shape=(jax.ShapeDtypeStruct((n, d), BF16), jax.ShapeDtypeStruct((n, d), BF16)),
        compiler_params=pltpu.CompilerParams(vmem_limit_bytes=VMEM_LIMIT),
        name="mem_kv",
    )(mem2d, g, wkv)


def _in_proj_kernel(x_ref, g_ref, w_ref, wa2_ref, ba_ref, mu_ref, zg_ref, zr_ref, prev_ref, wg_ref, wr_ref):
    tm = x_ref.shape[0]
    n_main = zg_ref.shape[-1] - wa2_ref.shape[-1]
    n_gla = n_main + GLA_GATE_RANK

    @pl.when(jnp.logical_and(pl.program_id(0) == 0, pl.program_id(1) == 0))
    def _():
        for dst, c_first in ((wg_ref, 0), (wr_ref, n_gla)):
            for c0 in range(0, dst.shape[1], LANES):
                slab = w_ref[c_first + c0:c_first + c0 + LANES, :]
                dst[:, c0:c0 + LANES] = slab.T.astype(BF16)

    @pl.when(pl.program_id(1) == 0)
    def _():
        prev_ref[...] = jnp.zeros_like(prev_ref)

    sub = min(SUBTILE_ROWS, tm)
    prev = prev_ref[0:1, :]
    for s in range(tm // sub):
        rows = slice(s * sub, (s + 1) * sub)
        h = _rms_norm(x_ref[rows, :], g_ref[...], NORM_EPS).astype(BF16)

        zr = jnp.dot(h, wr_ref[...], preferred_element_type=F32)
        rolled = pltpu.roll(zr, shift=1, axis=0)
        first = _iota2(zr.shape, 0) == 0
        shifted = jnp.where(first, jnp.broadcast_to(prev, zr.shape), rolled)
        zr_ref[rows, :] = zr + (shifted - zr) * mu_ref[...]
        prev = zr[sub - 1:sub, :]

        a_lo = jnp.dot(h, wg_ref[:, n_main:], preferred_element_type=F32)
        pre = _dot(a_lo, wa2_ref[...]) + ba_ref[...]
        zg_ref[rows, n_main:] = _log_sigmoid(pre) * (1.0 / GLA_GATE_NORMALIZER)
        zg_ref[rows, :n_main] = jnp.dot(h, wg_ref[:, :n_main], preferred_element_type=F32)
    prev_ref[0:1, :] = prev


def _in_proj(x, g, w_in_t, wa2p, ba, mu, tm):
    b, t, d = x.shape
    nr = mu.shape[1]
    n_main = w_in_t.shape[0] - nr - GLA_GATE_RANK
    ng = n_main + wa2p.shape[1]
    const = lambda *shape: _const_spec(shape, 2)
    return pl.pallas_call(
        _in_proj_kernel,
        grid=(b, t // tm),
        in_specs=[
            pl.BlockSpec((None, tm, d), lambda i, j: (i, j, 0)),
            const(1, d), const(*w_in_t.shape), const(*wa2p.shape),
            const(1, wa2p.shape[1]), const(1, nr),
        ],
        out_specs=(pl.BlockSpec((None, tm, ng), lambda i, j: (i, j, 0)),
                   pl.BlockSpec((None, tm, nr), lambda i, j: (i, j, 0))),
        out_shape=(jax.ShapeDtypeStruct((b, t, ng), F32), jax.ShapeDtypeStruct((b, t, nr), F32)),
        scratch_shapes=[pltpu.VMEM((8, nr), F32),
                        pltpu.VMEM((d, n_main + LANES), BF16),
                        pltpu.VMEM((d, nr), BF16)],
        compiler_params=_params(2),
        name="in_proj",
    )(x, g, w_in_t, wa2p, ba, mu)


def _gla_kernel(z_ref, ng_ref, o_ref, s_ref):
    nb, tt, _ = z_ref.shape
    kw = GLA_HEADS * GLA_DK
    vw = GLA_HEADS * GLA_DV
    c = CHUNK
    pairs = GLA_HEADS // 2
    pk = 2 * GLA_DK
    pv = 2 * GLA_DV
    cpi = GLA_CHUNKS_PER_ITER

    @pl.when(pl.program_id(0) == 0)
    def _():
        s_ref[...] = jnp.zeros_like(s_ref)

    tri3 = _cumsum_operator(c)
    causal = _iota2((c, pk), 0) >= _iota2((c, pk), 1) % c
    lane_lo = _iota2((c, pk), 1) < GLA_DK
    v_lo = _iota2((c, pv), 1) < GLA_DV
    s_mask = (_iota2((pv, pk), 0) // GLA_DV) == (_iota2((pv, pk), 1) // GLA_DK)
    norm_g = ng_ref[...]
    nt_dims = (((1,), (1,)), ((), ()))
    tn_dims = (((0,), (0,)), ((), ()))

    def step(it, carry):
        units = [(b, j, p) for b in range(nb) for j in range(cpi) for p in range(pairs)]
        rows = {(b, j): pl.ds(pl.multiple_of((it * cpi + j) * c, c), c) for b in range(nb) for j in range(cpi)}
        q_dec, k_inv, k_tail, dec, vv = {}, {}, {}, {}, {}
        for (b, j), rs in rows.items():
            q = z_ref[b, rs, 0:kw] * (GLA_DK ** -0.5)
            k = z_ref[b, rs, kw:2 * kw]
            log_a = z_ref[b, rs, 2 * kw + 2 * vw:3 * kw + 2 * vw]
            bcum = _chunk_cumsum(tri3, log_a)
            blast = bcum[c - 1:c, :]
            qd = q * jnp.exp(bcum)
            ki = k * jnp.exp(-bcum)
            kt = k * jnp.exp(blast - bcum)
            dc = jnp.exp(blast)
            for p in range(pairs):
                ks = slice(p * pk, (p + 1) * pk)
                q_dec[b, j, p] = qd[:, ks].astype(BF16)
                k_inv[b, j, p] = ki[:, ks]
                k_tail[b, j, p] = kt[:, ks].astype(BF16)
                dec[b, j, p] = dc[:, ks]
                vv[b, j, p] = z_ref[b, rs, 2 * kw + p * pv:2 * kw + (p + 1) * pv].astype(BF16)

        o_intra, d_state = {}, {}
        for u in units:
            scores = jnp.where(causal, lax.dot_general(q_dec[u], _bd(k_inv[u], lane_lo), nt_dims,
                                                       preferred_element_type=F32), 0.0)
            zero = jnp.zeros_like(vv[u])
            v_bd = jnp.concatenate([jnp.where(v_lo, vv[u], zero), jnp.where(v_lo, zero, vv[u])], axis=0)
            o_intra[u] = _mm1(scores, v_bd)
            d_state[u] = jnp.where(s_mask, lax.dot_general(vv[u], k_tail[u], tn_dims,
                                                           preferred_element_type=F32), 0.0)

        for b in range(nb):
            s_cur = [s_ref[b * pairs + p] for p in range(pairs)]
            for j in range(cpi):
                outs = []
                for p in range(pairs):
                    u = (b, j, p)
                    o = o_intra[u] + lax.dot_general(q_dec[u], s_cur[p].astype(BF16), nt_dims,
                                                     preferred_element_type=F32)
                    s_cur[p] = s_cur[p] * dec[u] + d_state[u]
                    for h in range(2):
                        oh = o[:, h * GLA_DV:(h + 1) * GLA_DV]
                        oh = oh * lax.rsqrt(jnp.mean(oh * oh, axis=-1, keepdims=True) + GLA_NORM_EPS) * norm_g
                        g0 = 2 * kw + vw + (2 * p + h) * GLA_DV
                        gh = z_ref[b, rows[b, j], g0:g0 + GLA_DV]
                        outs.append(oh * (gh * _sigmoid(gh)))
                o_ref[b, rows[b, j], :] = jnp.concatenate(outs, axis=-1).astype(o_ref.dtype)
            for p in range(pairs):
                s_ref[b * pairs + p] = s_cur[p]
        return carry

    lax.fori_loop(0, tt // (c * cpi), step, 0, unroll=min(GLA_UNROLL, tt // (c * cpi)))


def _gla(zg, norm_g, tt):
    b, t, n = zg.shape
    vw = GLA_HEADS * GLA_DV
    return pl.pallas_call(
        _gla_kernel,
        grid=(t // tt,),
        in_specs=[pl.BlockSpec((b, tt, n), lambda j: (0, j, 0)),
                  _const_spec((1, GLA_DV), 1)],
        out_specs=pl.BlockSpec((b, tt, vw), lambda j: (0, j, 0)),
        out_shape=jax.ShapeDtypeStruct((b, t, vw), BF16),
        scratch_shapes=[pltpu.VMEM((b * GLA_HEADS // 2, 2 * GLA_DV, 2 * GLA_DK), F32)],
        compiler_params=_params(1),
        name="gla",
    )(zg, norm_g)


def _seg_sum(x, seg_ones):
    xb = x.astype(BF16)
    blk = seg_ones.shape[0]
    parts = []
    for s in range(x.shape[1] // blk):
        sl = slice(s * blk, (s + 1) * blk)
        parts.append(jnp.dot(xb[:, sl], seg_ones, preferred_element_type=F32))
    return jnp.concatenate(parts, axis=-1)


def _block_diag(z, lane_lo):
    zero = jnp.zeros_like(z)
    return jnp.concatenate([jnp.where(lane_lo, z, zero), jnp.where(lane_lo, zero, z)], axis=0)


def _bd(z, lane_lo):
    return _block_diag(z.astype(BF16), lane_lo)


def _mm1(x, r):
    return jnp.dot(x.astype(BF16), r, preferred_element_type=F32)


def _rwkv_kernel(z_ref, w0_ref, w2_ref, a0_ref, a2_ref, g2_ref, kk_ref, ka_ref, rk_ref, lng_ref, lnb_ref,
                 o_ref, s_ref, st_ref, y_ref):
    nb, tt, _ = z_ref.shape
    w = w0_ref.shape[-1]
    n = RWKV_HEAD
    pairs = w // LANES
    c = CHUNK
    o_wl = 3 * w
    o_al = o_wl + RWKV_DECAY_RANK
    o_gl = o_al + RWKV_AAA_RANK

    @pl.when(pl.program_id(0) == 0)
    def _():
        s_ref[...] = jnp.zeros_like(s_ref)

    seg = 2 * LANES
    seg_ones = (_iota2((seg, seg), 0) // n == _iota2((seg, seg), 1) // n).astype(BF16)

    cols = lambda lo, hi: z_ref[:, :, lo:hi].reshape(nb * tt, hi - lo)
    r = cols(0, w)
    k = cols(w, 2 * w)
    v = cols(2 * w, 3 * w)
    u = w0_ref[...] + _dot(jnp.tanh(cols(o_wl, o_al)), w2_ref[...])
    lw = -math.exp(-0.5) * _sigmoid(u)
    a = _sigmoid(a0_ref[...] + _dot(cols(o_al, o_gl), a2_ref[...]))
    kk = k * kk_ref[...]
    kk = kk * lax.rsqrt(jnp.maximum(_seg_sum(kk * kk, seg_ones), 1e-24))
    k2 = k * (1.0 + (a - 1.0) * ka_ref[...])
    st_ref[0] = r
    st_ref[1] = lw
    st_ref[2] = k2
    st_ref[3] = v
    st_ref[4] = -kk
    st_ref[5] = kk * a

    tri3 = _cumsum_operator(c)
    row = _iota2((c, LANES), 0)
    col = _iota2((c, LANES), 1) % c
    lower = row >= col
    strict = row > col
    eye = (row == col).astype(F32)
    blk16 = (row // 16) == (col // 16)
    blk32 = (row // 32) == (col // 32)
    off16 = jnp.logical_and(blk32, jnp.logical_not(blk16))
    lane_lo = _iota2((c, LANES), 1) < n
    bd_mask = (_iota2((LANES, LANES), 0) // n) == (_iota2((LANES, LANES), 1) // n)
    units = [(b, p) for b in range(nb) for p in range(pairs)]
    prange = range(len(units))
    nt_dims = (((1,), (1,)), ((), ()))

    def chunk(ci, carry):
        rows = [pl.ds(pl.multiple_of(b * tt + ci * c, c), c) for b in range(nb)]
        a_t, r_t, b_t, k_t, b_h, k_h, vc, g_c = [], [], [], [], [], [], [], []
        for b in range(nb):
            rc = st_ref[0, rows[b], :]
            lwc = st_ref[1, rows[b], :]
            kc = st_ref[2, rows[b], :]
            ac = st_ref[4, rows[b], :]
            bc = st_ref[5, rows[b], :]
            cum = _chunk_cumsum(tri3, lwc)
            last = cum[c - 1:c, :]
            e_neg = jnp.exp(-cum)
            e_tail = jnp.exp(last - cum)
            r_tb = rc * jnp.exp(cum)
            a_tb = ac * jnp.exp(cum - lwc)
            vb = st_ref[3, rows[b], :]
            for p in range(pairs):
                sl = slice(p * LANES, (p + 1) * LANES)
                a_t.append(a_tb[:, sl])
                r_t.append(r_tb[:, sl])
                b_t.append(bc[:, sl] * e_neg[:, sl])
                k_t.append(kc[:, sl] * e_neg[:, sl])
                b_h.append(bc[:, sl] * e_tail[:, sl])
                k_h.append(kc[:, sl] * e_tail[:, sl])
                vc.append(vb[:, sl])
                g_c.append(jnp.exp(last[:, sl]))

        a_ab, a_ak, q_b, q_k = [], [], [], []
        for p in prange:
            lhs = jnp.concatenate([a_t[p], r_t[p]], axis=0).astype(BF16)
            rhs = jnp.concatenate([_bd(b_t[p], lane_lo), _bd(k_t[p], lane_lo)], axis=0)
            aa = lax.dot_general(lhs, rhs, nt_dims, preferred_element_type=F32)
            a_ab.append(jnp.where(strict, aa[:c, :LANES], 0.0))
            a_ak.append(jnp.where(strict, aa[:c, LANES:], 0.0))
            q_b.append(jnp.where(lower, aa[c:, :LANES], 0.0))
            q_k.append(jnp.where(lower, aa[c:, LANES:], 0.0))

        ad = [jnp.where(blk16, x, 0.0) for x in a_ab]
        pw = [_mm1(ad[p], _bd(ad[p], lane_lo)) for p in prange]
        t = [eye + ad[p] for p in prange]
        for _ in range(2):
            both = [_mm1(jnp.concatenate([pw[p], t[p]], axis=0), _bd(pw[p], lane_lo)) for p in prange]
            pw = [x[:c] for x in both]
            t = [t[p] + both[p][c:] for p in prange]
        t = [t[p] + _mm1(t[p], _bd(pw[p], lane_lo)) for p in prange]
        for msk in (off16, jnp.logical_not(blk32)):
            te = [_mm1(t[p], _bd(jnp.where(msk, a_ab[p], 0.0), lane_lo)) for p in prange]
            t = [t[p] + _mm1(te[p], _bd(t[p], lane_lo)) for p in prange]

        v_r = [_bd(vc[p], lane_lo) for p in prange]
        pv = [_mm1(a_ak[p], v_r[p]) for p in prange]
        wu = [_mm1(t[p], jnp.concatenate([_bd(a_t[p], lane_lo), _bd(pv[p], lane_lo)], axis=1))
              for p in prange]

        for p in prange:
            bi, pi = units[p]
            s0 = s_ref[p]
            ws = lax.dot_general(jnp.concatenate([wu[p][:, :LANES], r_t[p]], axis=0).astype(BF16),
                                 s0.astype(BF16), nt_dims, preferred_element_type=F32)
            u_mat = ws[:c] + wu[p][:, LANES:]
            y = ws[c:] + _mm1(jnp.concatenate([q_b[p], q_k[p]], axis=1),
                              jnp.concatenate([_bd(u_mat, lane_lo), v_r[p]], axis=0))
            y_ref[rows[bi], pi * LANES:(pi + 1) * LANES] = y
            ds = _mm1(jnp.concatenate([u_mat, vc[p]], axis=0).T,
                      jnp.concatenate([b_h[p], k_h[p]], axis=0).astype(BF16))
            s_ref[p] = s0 * g_c[p] + jnp.where(bd_mask, ds, 0.0)
        return carry

    lax.fori_loop(0, tt // c, chunk, 0, unroll=min(RWKV_UNROLL, tt // c))

    y = y_ref[...]
    mean = _seg_sum(y, seg_ones) * (1.0 / n)
    dlt = y - mean
    var = _seg_sum(dlt * dlt, seg_ones) * (1.0 / n)
    yn = dlt * lax.rsqrt(var + RWKV_LN_EPS) * lng_ref[...] + lnb_ref[...]
    bonus = _seg_sum(st_ref[0] * st_ref[2] * rk_ref[...], seg_ones) * st_ref[3]
    gate = _dot(_sigmoid(cols(o_gl, o_gl + RWKV_GATE_RANK)), g2_ref[...])
    o_ref[...] = ((yn + bonus) * gate).reshape(nb, tt, w).astype(o_ref.dtype)


def _rwkv(zr, w0, w2, a0, a2, g2, k_k, k_a, r_k, ln_g, ln_b, tt):
    b, t, nz = zr.shape
    w = w0.shape[-1]
    const = lambda *shape: _const_spec(shape, 1)
    return pl.pallas_call(
        _rwkv_kernel,
        grid=(t // tt,),
        in_specs=[pl.BlockSpec((b, tt, nz), lambda j: (0, j, 0)),
                  const(1, w), const(*w2.shape), const(1, w), const(*a2.shape), const(*g2.shape),
                  const(1, w), const(1, w), const(1, w), const(1, w), const(1, w)],
        out_specs=pl.BlockSpec((b, tt, w), lambda j: (0, j, 0)),
        out_shape=jax.ShapeDtypeStruct((b, t, w), BF16),
        scratch_shapes=[pltpu.VMEM((b * w // LANES, LANES, LANES), F32),
                        pltpu.VMEM((6, b * tt, w), F32),
                        pltpu.VMEM((b * tt, w), F32)],
        compiler_params=_params(1),
        name="rwkv",
    )(zr, w0, w2, a0, a2, g2, k_k, k_a, r_k, ln_g, ln_b)


def _post_attn_kernel(x_ref, og_ref, or_ref, wog_ref, wor_ref, gx_ref, wq_ref, k_ref, v_ref, wo_ref,
                      wgu_ref, wd_ref, out_ref, wgu_out_ref, wd_out_ref):
    wgu_out_ref[...] = wgu_ref[...].astype(BF16)
    wd_out_ref[...] = wd_ref[...].astype(BF16)
    tm, d = x_ref.shape
    hd = d // MEM_HEADS
    sub = min(POST_ATTN_SUBTILE_ROWS, tm)
    rows = [slice(s0, s0 + sub) for s0 in range(0, tm, sub)]
    nt_dims = (((1,), (1,)), ((), ()))
    x1 = [x_ref[r, :] + jnp.dot(og_ref[r, :], wog_ref[...], preferred_element_type=F32)
          + jnp.dot(or_ref[r, :], wor_ref[...], preferred_element_type=F32) for r in rows]
    q = [_dot(_rms_norm(x, gx_ref[...], NORM_EPS), wq_ref[...]).astype(BF16) for x in x1]
    heads = [[] for _ in rows]
    for h in range(MEM_HEADS):
        hs = slice(h * hd, (h + 1) * hd)
        s = [lax.dot_general(qi[:, hs], k_ref[:, hs], nt_dims, preferred_element_type=F32) * (hd ** -0.5)
             for qi in q]
        e = [jnp.exp(si - jnp.max(si, axis=-1, keepdims=True)) for si in s]
        p = [ei / jnp.sum(ei, axis=-1, keepdims=True) for ei in e]
        for i, pi in enumerate(p):
            heads[i].append(_dot(pi, v_ref[:, hs]))
    for i, r in enumerate(rows):
        out_ref[r, :] = x1[i] + _dot(jnp.concatenate(heads[i], axis=-1), wo_ref[...])


def _post_attn(x, o_gla, o_rwkv, wo_g, wo_r, gx, wq, kmem, vmem, wo, wgu, wd, tm):
    b, t, d = x.shape
    m = kmem.shape[1]
    nj = t // tm
    steps = b * nj
    bf16_rows = 16
    assert wgu.shape[0] % (steps * bf16_rows) == 0 and wd.shape[0] % (steps * bf16_rows) == 0
    const = lambda *shape: _const_spec(shape, 2)
    tile = lambda width: pl.BlockSpec((None, tm, width), lambda i, j: (i, j, 0))
    slab = lambda w: pl.BlockSpec((w.shape[0] // steps, w.shape[1]), lambda i, j: (i * nj + j, 0))
    return pl.pallas_call(
        _post_attn_kernel,
        grid=(b, nj),
        in_specs=[tile(d), tile(o_gla.shape[-1]), tile(o_rwkv.shape[-1]),
                  const(*wo_g.shape), const(*wo_r.shape), const(1, d), const(*wq.shape),
                  pl.BlockSpec((None, m, d), lambda i, j: (i, 0, 0)),
                  pl.BlockSpec((None, m, d), lambda i, j: (i, 0, 0)),
                  const(*wo.shape), slab(wgu), slab(wd)],
        out_specs=(tile(d), slab(wgu), slab(wd)),
        out_shape=(jax.ShapeDtypeStruct((b, t, d), F32), jax.ShapeDtypeStruct(wgu.shape, BF16),
                   jax.ShapeDtypeStruct(wd.shape, BF16)),
        compiler_params=_params(2),
        name="post_attn",
    )(x, o_gla, o_rwkv, wo_g, wo_r, gx, wq, kmem, vmem, wo, wgu, wd)


def _ffn_kernel(x_ref, g_ref, wgu_ref, wd_ref, gf_ref, out_ref, *, ff_chunk):
    x = x_ref[...]
    d_ff = wd_ref.shape[0]
    h = _rms_norm(x, g_ref[...], NORM_EPS).astype(BF16)
    acc = x
    for c0 in range(0, d_ff, ff_chunk):
        gate = jnp.dot(h, wgu_ref[:, c0:c0 + ff_chunk], preferred_element_type=F32)
        up = jnp.dot(h, wgu_ref[:, d_ff + c0:d_ff + c0 + ff_chunk], preferred_element_type=F32)
        act = (gate * _sigmoid(gate) * up).astype(BF16)
        acc = acc + jnp.dot(act, wd_ref[c0:c0 + ff_chunk, :], preferred_element_type=F32)
    out_ref[...] = _rms_norm(acc, gf_ref[...], NORM_EPS)


def _ffn(x, g, wgu, wd, gf, tm, ff_chunk):
    b, t, d = x.shape
    const = lambda *shape: _const_spec(shape, 2)
    tile = pl.BlockSpec((None, tm, d), lambda i, j: (i, j, 0))
    return pl.pallas_call(
        functools.partial(_ffn_kernel, ff_chunk=ff_chunk),
        grid=(b, t // tm),
        in_specs=[tile, const(1, d), const(*wgu.shape), const(*wd.shape), const(1, d)],
        out_specs=tile,
        out_shape=jax.ShapeDtypeStruct((b, t, d), F32),
        compiler_params=_params(2),
        name="ffn",
    )(x, g, wgu, wd, gf)


def _layer(x, mem, norm_mix_g, w_in, gla_wa2, gla_ba, gla_norm_g, rwkv_mu, rwkv_w0, rwkv_w2, rwkv_a0,
           rwkv_a2, rwkv_g2, rwkv_k_k, rwkv_k_a, rwkv_r_k, rwkv_ln_g, rwkv_ln_b, w_out, norm_mem_x_g,
           norm_mem_g, wq_mem, wkv_mem, wo_mem, norm_ffn_g, w_gate_up, w_down, final_g):
    b, t, d = x.shape
    m = mem.shape[1]
    row = lambda p: p.reshape(1, -1).astype(F32)
    vw = GLA_HEADS * GLA_DV
    wa2p = jnp.pad(gla_wa2, ((0, LANES - GLA_GATE_RANK), (0, 0))).astype(BF16)

    kmem, vmem = _mem_kv(mem.reshape(b * m, d), row(norm_mem_g), wkv_mem)
    zg, zr = _in_proj(x, row(norm_mix_g), w_in.T, wa2p, row(gla_ba), row(rwkv_mu), tm=min(ROWS_IN_PROJ, t))
    o_gla = _gla(zg, row(gla_norm_g), tt=min(ROWS_GLA, t))
    o_rwkv = _rwkv(zr, row(rwkv_w0), rwkv_w2.astype(BF16), row(rwkv_a0), rwkv_a2.astype(BF16),
                   rwkv_g2.astype(BF16), row(rwkv_k_k), row(rwkv_k_a), row(rwkv_r_k), row(rwkv_ln_g),
                   row(rwkv_ln_b), tt=min(ROWS_RWKV, t))
    w_out_b = w_out.astype(BF16)
    x2, wgu_b, wd_b = _post_attn(x, o_gla, o_rwkv, w_out_b[:vw], w_out_b[vw:], row(norm_mem_x_g),
                                 wq_mem.astype(BF16), kmem.reshape(b, m, d), vmem.reshape(b, m, d),
                                 wo_mem.astype(BF16), w_gate_up, w_down, tm=min(ROWS_POST_ATTN, t))
    return _ffn(x2, row(norm_ffn_g), wgu_b, wd_b, row(final_g), tm=min(ROWS_FFN, t), ff_chunk=FF_CHUNK)


def kernel(x, mem, norm_mix_g, w_in, gla_wa2, gla_ba, gla_norm_g, rwkv_mu, rwkv_w0, rwkv_w2, rwkv_a0, rwkv_a2, rwkv_g2, rwkv_k_k, rwkv_k_a, rwkv_r_k, rwkv_ln_g, rwkv_ln_b, w_out, norm_mem_x_g, norm_mem_g, wq_mem, wkv_mem, wo_mem, norm_ffn_g, w_gate_up, w_down, norm_final_g):
    assert norm_mix_g.shape[0] == 1, "single-layer block"
    return _layer(x, mem, norm_mix_g[0], w_in[0], gla_wa2[0], gla_ba[0], gla_norm_g[0], rwkv_mu[0],
                  rwkv_w0[0], rwkv_w2[0], rwkv_a0[0], rwkv_a2[0], rwkv_g2[0], rwkv_k_k[0], rwkv_k_a[0],
                  rwkv_r_k[0], rwkv_ln_g[0], rwkv_ln_b[0], w_out[0], norm_mem_x_g[0], norm_mem_g[0],
                  wq_mem[0], wkv_mem[0], wo_mem[0], norm_ffn_g[0], w_gate_up[0], w_down[0], norm_final_g)
```

```python
import functools
import math

import jax
import jax.numpy as jnp
from jax import lax
from jax.experimental import pallas as pl
from jax.experimental.pallas import tpu as pltpu

F32 = jnp.float32
BF16 = jnp.bfloat16

MEM_HEADS = 4
GLA_HEADS = 4
GLA_DK = 64
GLA_DV = 128
GLA_GATE_RANK = 16
GLA_GATE_NORMALIZER = 16.0
GLA_NORM_EPS = 1e-5
RWKV_HEAD = 64
RWKV_DECAY_RANK = 64
RWKV_AAA_RANK = 64
RWKV_GATE_RANK = 128
RWKV_LN_EPS = 64e-5
NORM_EPS = 1e-6

CHUNK = 64
GLA_CHUNKS_PER_ITER = 2
GLA_UNROLL = 4
RWKV_UNROLL = 8
SUBTILE_ROWS = 256
POST_ATTN_SUBTILE_ROWS = 512
ROWS_IN_PROJ = 512
ROWS_GLA = 1024
ROWS_RWKV = 512
ROWS_POST_ATTN = 1024
ROWS_FFN = 1024
FF_CHUNK = 256
LANES = 128
VMEM_LIMIT = 56 * 1024 * 1024


def _params(n_grid_dims):
    return pltpu.CompilerParams(dimension_semantics=("arbitrary",) * n_grid_dims,
                                vmem_limit_bytes=VMEM_LIMIT)


def _const_spec(shape, grid_rank):
    zeros = (0,) * len(shape)
    index_map = (lambda j: zeros) if grid_rank == 1 else (lambda i, j: zeros)
    return pl.BlockSpec(shape, index_map, pipeline_mode=pl.Buffered(1))


def _dot(a, b):
    return jnp.dot(a.astype(BF16), b.astype(BF16), preferred_element_type=F32)


def _rms_norm(x, g, eps):
    return x * lax.rsqrt(jnp.mean(x * x, axis=-1, keepdims=True) + eps) * g


def _sigmoid(x):
    return 1.0 / (1.0 + jnp.exp(-x))


def _log_sigmoid(x):
    return jnp.minimum(x, 0.0) - jnp.log1p(jnp.exp(-jnp.abs(x)))


def _iota2(shape, dim):
    return lax.broadcasted_iota(jnp.int32, shape, dim)


def _cumsum_operator(c):
    return (_iota2((c, 3 * c), 0) >= _iota2((c, 3 * c), 1) % c).astype(BF16)


def _chunk_cumsum(tri3, x):
    p1 = x.astype(BF16)
    rem = x - p1.astype(F32)
    p2 = rem.astype(BF16)
    p3 = (rem - p2.astype(F32)).astype(BF16)
    return jnp.dot(tri3, jnp.concatenate([p1, p2, p3], axis=0), preferred_element_type=F32)


def _mem_kv_kernel(mem_ref, g_ref, w_ref, k_ref, v_ref):
    d = k_ref.shape[-1]
    m = _rms_norm(mem_ref[...], g_ref[...], NORM_EPS)
    kv = _dot(m, w_ref[...])
    k_ref[...] = kv[:, :d].astype(BF16)
    v_ref[...] = kv[:, d:].astype(BF16)


def _mem_kv(mem2d, g, wkv):
    n, d = mem2d.shape
    return pl.pallas_call(
        _mem_kv_kernel,
        out_shape=(jax.ShapeDtypeStruct((n, d), BF16), jax.ShapeDtypeStruct((n, d), BF16)),
        compiler_params=pltpu.CompilerParams(vmem_limit_bytes=VMEM_LIMIT),
        name="mem_kv",
    )(mem2d, g, wkv)


def _in_proj_kernel(x_ref, g_ref, w_ref, wa2_ref, ba_ref, mu_ref, zg_ref, zr_ref, prev_ref, wg_ref, wr_ref):
    tm = x_ref.shape[0]
    n_main = zg_ref.shape[-1] - wa2_ref.shape[-1]
    n_gla = n_main + GLA_GATE_RANK

    @pl.when(jnp.logical_and(pl.program_id(0) == 0, pl.program_id(1) == 0))
    def _():
        for dst, c_first in ((wg_ref, 0), (wr_ref, n_gla)):
            for c0 in range(0, dst.shape[1], LANES):
                slab = w_ref[c_first + c0:c_first + c0 + LANES, :]
                dst[:, c0:c0 + LANES] = slab.T.astype(BF16)

    @pl.when(pl.program_id(1) == 0)
    def _():
        prev_ref[...] = jnp.zeros_like(prev_ref)

    sub = min(SUBTILE_ROWS, tm)
    prev = prev_ref[0:1, :]
    for s in range(tm // sub):
        rows = slice(s * sub, (s + 1) * sub)
        h = _rms_norm(x_ref[rows, :], g_ref[...], NORM_EPS).astype(BF16)

        zr = jnp.dot(h, wr_ref[...], preferred_element_type=F32)
        rolled = pltpu.roll(zr, shift=1, axis=0)
        first = _iota2(zr.shape, 0) == 0
        shifted = jnp.where(first, jnp.broadcast_to(prev, zr.shape), rolled)
        zr_ref[rows, :] = zr + (shifted - zr) * mu_ref[...]
        prev = zr[sub - 1:sub, :]

        a_lo = jnp.dot(h, wg_ref[:, n_main:], preferred_element_type=F32)
        pre = _dot(a_lo, wa2_ref[...]) + ba_ref[...]
        zg_ref[rows, n_main:] = _log_sigmoid(pre) * (1.0 / GLA_GATE_NORMALIZER)
        zg_ref[rows, :n_main] = jnp.dot(h, wg_ref[:, :n_main], preferred_element_type=F32)
    prev_ref[0:1, :] = prev


def _in_proj(x, g, w_in_t, wa2p, ba, mu, tm):
    b, t, d = x.shape
    nr = mu.shape[1]
    n_main = w_in_t.shape[0] - nr - GLA_GATE_RANK
    ng = n_main + wa2p.shape[1]
    const = lambda *shape: _const_spec(shape, 2)
    return pl.pallas_call(
        _in_proj_kernel,
        grid=(b, t // tm),
        in_specs=[
            pl.BlockSpec((None, tm, d), lambda i, j: (i, j, 0)),
            const(1, d), const(*w_in_t.shape), const(*wa2p.shape),
            const(1, wa2p.shape[1]), const(1, nr),
        ],
        out_specs=(pl.BlockSpec((None, tm, ng), lambda i, j: (i, j, 0)),
                   pl.BlockSpec((None, tm, nr), lambda i, j: (i, j, 0))),
        out_shape=(jax.ShapeDtypeStruct((b, t, ng), F32), jax.ShapeDtypeStruct((b, t, nr), F32)),
        scratch_shapes=[pltpu.VMEM((8, nr), F32),
                        pltpu.VMEM((d, n_main + LANES), BF16),
                        pltpu.VMEM((d, nr), BF16)],
        compiler_params=_params(2),
        name="in_proj",
    )(x, g, w_in_t, wa2p, ba, mu)


def _gla_kernel(z_ref, ng_ref, o_ref, s_ref):
    nb, tt, _ = z_ref.shape
    kw = GLA_HEADS * GLA_DK
    vw = GLA_HEADS * GLA_DV
    c = CHUNK
    pairs = GLA_HEADS // 2
    pk = 2 * GLA_DK
    pv = 2 * GLA_DV
    cpi = GLA_CHUNKS_PER_ITER

    @pl.when(pl.program_id(0) == 0)
    def _():
        s_ref[...] = jnp.zeros_like(s_ref)

    tri3 = _cumsum_operator(c)
    causal = _iota2((c, pk), 0) >= _iota2((c, pk), 1) % c
    lane_lo = _iota2((c, pk), 1) < GLA_DK
    v_lo = _iota2((c, pv), 1) < GLA_DV
    s_mask = (_iota2((pv, pk), 0) // GLA_DV) == (_iota2((pv, pk), 1) // GLA_DK)
    norm_g = ng_ref[...]
    nt_dims = (((1,), (1,)), ((), ()))
    tn_dims = (((0,), (0,)), ((), ()))

    def step(it, carry):
        units = [(b, j, p) for b in range(nb) for j in range(cpi) for p in range(pairs)]
        rows = {(b, j): pl.ds(pl.multiple_of((it * cpi + j) * c, c), c) for b in range(nb) for j in range(cpi)}
        q_dec, k_inv, k_tail, dec, vv = {}, {}, {}, {}, {}
        for (b, j), rs in rows.items():
            q = z_ref[b, rs, 0:kw] * (GLA_DK ** -0.5)
            k = z_ref[b, rs, kw:2 * kw]
            log_a = z_ref[b, rs, 2 * kw + 2 * vw:3 * kw + 2 * vw]
            bcum = _chunk_cumsum(tri3, log_a)
            blast = bcum[c - 1:c, :]
            qd = q * jnp.exp(bcum)
            ki = k * jnp.exp(-bcum)
            kt = k * jnp.exp(blast - bcum)
            dc = jnp.exp(blast)
            for p in range(pairs):
                ks = slice(p * pk, (p + 1) * pk)
                q_dec[b, j, p] = qd[:, ks].astype(BF16)
                k_inv[b, j, p] = ki[:, ks]
                k_tail[b, j, p] = kt[:, ks].astype(BF16)
                dec[b, j, p] = dc[:, ks]
                vv[b, j, p] = z_ref[b, rs, 2 * kw + p * pv:2 * kw + (p + 1) * pv].astype(BF16)

        o_intra, d_state = {}, {}
        for u in units:
            scores = jnp.where(causal, lax.dot_general(q_dec[u], _bd(k_inv[u], lane_lo), nt_dims,
                                                       preferred_element_type=F32), 0.0)
            zero = jnp.zeros_like(vv[u])
            v_bd = jnp.concatenate([jnp.where(v_lo, vv[u], zero), jnp.where(v_lo, zero, vv[u])], axis=0)
            o_intra[u] = _mm1(scores, v_bd)
            d_state[u] = jnp.where(s_mask, lax.dot_general(vv[u], k_tail[u], tn_dims,
                                                           preferred_element_type=F32), 0.0)

        for b in range(nb):
            s_cur = [s_ref[b * pairs + p] for p in range(pairs)]
            for j in range(cpi):
                outs = []
                for p in range(pairs):
                    u = (b, j, p)
                    o = o_intra[u] + lax.dot_general(q_dec[u], s_cur[p].astype(BF16), nt_dims,
                                                     preferred_element_type=F32)
                    s_cur[p] = s_cur[p] * dec[u] + d_state[u]
                    for h in range(2):
                        oh = o[:, h * GLA_DV:(h + 1) * GLA_DV]
                        oh = oh * lax.rsqrt(jnp.mean(oh * oh, axis=-1, keepdims=True) + GLA_NORM_EPS) * norm_g
                        g0 = 2 * kw + vw + (2 * p + h) * GLA_DV
                        gh = z_ref[b, rows[b, j], g0:g0 + GLA_DV]
                        outs.append(oh * (gh * _sigmoid(gh)))
                o_ref[b, rows[b, j], :] = jnp.concatenate(outs, axis=-1).astype(o_ref.dtype)
            for p in range(pairs):
                s_ref[b * pairs + p] = s_cur[p]
        return carry

    lax.fori_loop(0, tt // (c * cpi), step, 0, unroll=min(GLA_UNROLL, tt // (c * cpi)))


def _gla(zg, norm_g, tt):
    b, t, n = zg.shape
    vw = GLA_HEADS * GLA_DV
    return pl.pallas_call(
        _gla_kernel,
        grid=(t // tt,),
        in_specs=[pl.BlockSpec((b, tt, n), lambda j: (0, j, 0)),
                  _const_spec((1, GLA_DV), 1)],
        out_specs=pl.BlockSpec((b, tt, vw), lambda j: (0, j, 0)),
        out_shape=jax.ShapeDtypeStruct((b, t, vw), BF16),
        scratch_shapes=[pltpu.VMEM((b * GLA_HEADS // 2, 2 * GLA_DV, 2 * GLA_DK), F32)],
        compiler_params=_params(1),
        name="gla",
    )(zg, norm_g)


def _seg_sum(x, seg_ones):
    xb = x.astype(BF16)
    blk = seg_ones.shape[0]
    parts = []
    for s in range(x.shape[1] // blk):
        sl = slice(s * blk, (s + 1) * blk)
        parts.append(jnp.dot(xb[:, sl], seg_ones, preferred_element_type=F32))
    return jnp.concatenate(parts, axis=-1)


def _block_diag(z, lane_lo):
    zero = jnp.zeros_like(z)
    return jnp.concatenate([jnp.where(lane_lo, z, zero), jnp.where(lane_lo, zero, z)], axis=0)


def _bd(z, lane_lo):
    return _block_diag(z.astype(BF16), lane_lo)


def _mm1(x, r):
    return jnp.dot(x.astype(BF16), r, preferred_element_type=F32)


def _rwkv_kernel(z_ref, w0_ref, w2_ref, a0_ref, a2_ref, g2_ref, kk_ref, ka_ref, rk_ref, lng_ref, lnb_ref,
                 o_ref, s_ref, st_ref, y_ref):
    nb, tt, _ = z_ref.shape
    w = w0_ref.shape[-1]
    n = RWKV_HEAD
    pairs = w // LANES
    c = CHUNK
    o_wl = 3 * w
    o_al = o_wl + RWKV_DECAY_RANK
    o_gl = o_al + RWKV_AAA_RANK

    @pl.when(pl.program_id(0) == 0)
    def _():
        s_ref[...] = jnp.zeros_like(s_ref)

    seg = 2 * LANES
    seg_ones = (_iota2((seg, seg), 0) // n == _iota2((seg, seg), 1) // n).astype(BF16)

    cols = lambda lo, hi: z_ref[:, :, lo:hi].reshape(nb * tt, hi - lo)
    r = cols(0, w)
    k = cols(w, 2 * w)
    v = cols(2 * w, 3 * w)
    u = w0_ref[...] + _dot(jnp.tanh(cols(o_wl, o_al)), w2_ref[...])
    lw = -math.exp(-0.5) * _sigmoid(u)
    a = _sigmoid(a0_ref[...] + _dot(cols(o_al, o_gl), a2_ref[...]))
    kk = k * kk_ref[...]
    kk = kk * lax.rsqrt(jnp.maximum(_seg_sum(kk * kk, seg_ones), 1e-24))
    k2 = k * (1.0 + (a - 1.0) * ka_ref[...])
    st_ref[0] = r
    st_ref[1] = lw
    st_ref[2] = k2
    st_ref[3] = v
    st_ref[4] = -kk
    st_ref[5] = kk * a

    tri3 = _cumsum_operator(c)
    row = _iota2((c, LANES), 0)
    col = _iota2((c, LANES), 1) % c
    lower = row >= col
    strict = row > col
    eye = (row == col).astype(F32)
    blk16 = (row // 16) == (col // 16)
    blk32 = (row // 32) == (col // 32)
    off16 = jnp.logical_and(blk32, jnp.logical_not(blk16))
    lane_lo = _iota2((c, LANES), 1) < n
    bd_mask = (_iota2((LANES, LANES), 0) // n) == (_iota2((LANES, LANES), 1) // n)
    units = [(b, p) for b in range(nb) for p in range(pairs)]
    prange = range(len(units))
    nt_dims = (((1,), (1,)), ((), ()))

    def chunk(ci, carry):
        rows = [pl.ds(pl.multiple_of(b * tt + ci * c, c), c) for b in range(nb)]
        a_t, r_t, b_t, k_t, b_h, k_h, vc, g_c = [], [], [], [], [], [], [], []
        for b in range(nb):
            rc = st_ref[0, rows[b], :]
            lwc = st_ref[1, rows[b], :]
            kc = st_ref[2, rows[b], :]
            ac = st_ref[4, rows[b], :]
            bc = st_ref[5, rows[b], :]
            cum = _chunk_cumsum(tri3, lwc)
            last = cum[c - 1:c, :]
            e_neg = jnp.exp(-cum)
            e_tail = jnp.exp(last - cum)
            r_tb = rc * jnp.exp(cum)
            a_tb = ac * jnp.exp(cum - lwc)
            vb = st_ref[3, rows[b], :]
            for p in range(pairs):
                sl = slice(p * LANES, (p + 1) * LANES)
                a_t.append(a_tb[:, sl])
                r_t.append(r_tb[:, sl])
                b_t.append(bc[:, sl] * e_neg[:, sl])
                k_t.append(kc[:, sl] * e_neg[:, sl])
                b_h.append(bc[:, sl] * e_tail[:, sl])
                k_h.append(kc[:, sl] * e_tail[:, sl])
                vc.append(vb[:, sl])
                g_c.append(jnp.exp(last[:, sl]))

        a_ab, a_ak, q_b, q_k = [], [], [], []
        for p in prange:
            lhs = jnp.concatenate([a_t[p], r_t[p]], axis=0).astype(BF16)
            rhs = jnp.concatenate([_bd(b_t[p], lane_lo), _bd(k_t[p], lane_lo)], axis=0)
            aa = lax.dot_general(lhs, rhs, nt_dims, preferred_element_type=F32)
            a_ab.append(jnp.where(strict, aa[:c, :LANES], 0.0))
            a_ak.append(jnp.where(strict, aa[:c, LANES:], 0.0))
            q_b.append(jnp.where(lower, aa[c:, :LANES], 0.0))
            q_k.append(jnp.where(lower, aa[c:, LANES:], 0.0))

        ad = [jnp.where(blk16, x, 0.0) for x in a_ab]
        pw = [_mm1(ad[p], _bd(ad[p], lane_lo)) for p in prange]
        t = [eye + ad[p] for p in prange]
        for _ in range(2):
            both = [_mm1(jnp.concatenate([pw[p], t[p]], axis=0), _bd(pw[p], lane_lo)) for p in prange]
            pw = [x[:c] for x in both]
            t = [t[p] + both[p][c:] for p in prange]
        t = [t[p] + _mm1(t[p], _bd(pw[p], lane_lo)) for p in prange]
        for msk in (off16, jnp.logical_not(blk32)):
            te = [_mm1(t[p], _bd(jnp.where(msk, a_ab[p], 0.0), lane_lo)) for p in prange]
            t = [t[p] + _mm1(te[p], _bd(t[p], lane_lo)) for p in prange]

        v_r = [_bd(vc[p], lane_lo) for p in prange]
        pv = [_mm1(a_ak[p], v_r[p]) for p in prange]
        wu = [_mm1(t[p], jnp.concatenate([_bd(a_t[p], lane_lo), _bd(pv[p], lane_lo)], axis=1))
              for p in prange]

        for p in prange:
            bi, pi = units[p]
            s0 = s_ref[p]
            ws = lax.dot_general(jnp.concatenate([wu[p][:, :LANES], r_t[p]], axis=0).astype(BF16),
                                 s0.astype(BF16), nt_dims, preferred_element_type=F32)
            u_mat = ws[:c] + wu[p][:, LANES:]
            y = ws[c:] + _mm1(jnp.concatenate([q_b[p], q_k[p]], axis=1),
                              jnp.concatenate([_bd(u_mat, lane_lo), v_r[p]], axis=0))
            y_ref[rows[bi], pi * LANES:(pi + 1) * LANES] = y
            ds = _mm1(jnp.concatenate([u_mat, vc[p]], axis=0).T,
                      jnp.concatenate([b_h[p], k_h[p]], axis=0).astype(BF16))
            s_ref[p] = s0 * g_c[p] + jnp.where(bd_mask, ds, 0.0)
        return carry

    lax.fori_loop(0, tt // c, chunk, 0, unroll=min(RWKV_UNROLL, tt // c))

    y = y_ref[...]
    mean = _seg_sum(y, seg_ones) * (1.0 / n)
    dlt = y - mean
    var = _seg_sum(dlt * dlt, seg_ones) * (1.0 / n)
    yn = dlt * lax.rsqrt(var + RWKV_LN_EPS) * lng_ref[...] + lnb_ref[...]
    bonus = _seg_sum(st_ref[0] * st_ref[2] * rk_ref[...], seg_ones) * st_ref[3]
    gate = _dot(_sigmoid(cols(o_gl, o_gl + RWKV_GATE_RANK)), g2_ref[...])
    o_ref[...] = ((yn + bonus) * gate).reshape(nb, tt, w).astype(o_ref.dtype)


def _rwkv(zr, w0, w2, a0, a2, g2, k_k, k_a, r_k, ln_g, ln_b, tt):
    b, t, nz = zr.shape
    w = w0.shape[-1]
    const = lambda *shape: _const_spec(shape, 1)
    return pl.pallas_call(
        _rwkv_kernel,
        grid=(t // tt,),
        in_specs=[pl.BlockSpec((b, tt, nz), lambda j: (0, j, 0)),
                  const(1, w), const(*w2.shape), const(1, w), const(*a2.shape), const(*g2.shape),
                  const(1, w), const(1, w), const(1, w), const(1, w), const(1, w)],
        out_specs=pl.BlockSpec((b, tt, w), lambda j: (0, j, 0)),
        out_shape=jax.ShapeDtypeStruct((b, t, w), BF16),
        scratch_shapes=[pltpu.VMEM((b * w // LANES, LANES, LANES), F32),
                        pltpu.VMEM((6, b * tt, w), F32),
                        pltpu.VMEM((b * tt, w), F32)],
        compiler_params=_params(1),
        name="rwkv",
    )(zr, w0, w2, a0, a2, g2, k_k, k_a, r_k, ln_g, ln_b)


def _post_attn_kernel(x_ref, og_ref, or_ref, wog_ref, wor_ref, gx_ref, wq_ref, k_ref, v_ref, wo_ref,
                      wgu_ref, wd_ref, out_ref, wgu_out_ref, wd_out_ref):
    wgu_out_ref[...] = wgu_ref[...].astype(BF16)
    wd_out_ref[...] = wd_ref[...].astype(BF16)
    tm, d = x_ref.shape
    hd = d // MEM_HEADS
    sub = min(POST_ATTN_SUBTILE_ROWS, tm)
    rows = [slice(s0, s0 + sub) for s0 in range(0, tm, sub)]
    nt_dims = (((1,), (1,)), ((), ()))
    x1 = [x_ref[r, :] + jnp.dot(og_ref[r, :], wog_ref[...], preferred_element_type=F32)
          + jnp.dot(or_ref[r, :], wor_ref[...], preferred_element_type=F32) for r in rows]
    q = [_dot(_rms_norm(x, gx_ref[...], NORM_EPS), wq_ref[...]).astype(BF16) for x in x1]
    heads = [[] for _ in rows]
    for h in range(MEM_HEADS):
        hs = slice(h * hd, (h + 1) * hd)
        s = [lax.dot_general(qi[:, hs], k_ref[:, hs], nt_dims, preferred_element_type=F32) * (hd ** -0.5)
             for qi in q]
        e = [jnp.exp(si - jnp.max(si, axis=-1, keepdims=True)) for si in s]
        p = [ei / jnp.sum(ei, axis=-1, keepdims=True) for ei in e]
        for i, pi in enumerate(p):
            heads[i].append(_dot(pi, v_ref[:, hs]))
    for i, r in enumerate(rows):
        out_ref[r, :] = x1[i] + _dot(jnp.concatenate(heads[i], axis=-1), wo_ref[...])


def _post_attn(x, o_gla, o_rwkv, wo_g, wo_r, gx, wq, kmem, vmem, wo, wgu, wd, tm):
    b, t, d = x.shape
    m = kmem.shape[1]
    nj = t // tm
    steps = b * nj
    bf16_rows = 16
    assert wgu.shape[0] % (steps * bf16_rows) == 0 and wd.shape[0] % (steps * bf16_rows) == 0
    const = lambda *shape: _const_spec(shape, 2)
    tile = lambda width: pl.BlockSpec((None, tm, width), lambda i, j: (i, j, 0))
    slab = lambda w: pl.BlockSpec((w.shape[0] // steps, w.shape[1]), lambda i, j: (i * nj + j, 0))
    return pl.pallas_call(
        _post_attn_kernel,
        grid=(b, nj),
        in_specs=[tile(d), tile(o_gla.shape[-1]), tile(o_rwkv.shape[-1]),
                  const(*wo_g.shape), const(*wo_r.shape), const(1, d), const(*wq.shape),
                  pl.BlockSpec((None, m, d), lambda i, j: (i, 0, 0)),
                  pl.BlockSpec((None, m, d), lambda i, j: (i, 0, 0)),
                  const(*wo.shape), slab(wgu), slab(wd)],
        out_specs=(tile(d), slab(wgu), slab(wd)),
        out_shape=(jax.ShapeDtypeStruct((b, t, d), F32), jax.ShapeDtypeStruct(wgu.shape, BF16),
                   jax.ShapeDtypeStruct(wd.shape, BF16)),
        compiler_params=_params(2),
        name="post_attn",
    )(x, o_gla, o_rwkv, wo_g, wo_r, gx, wq, kmem, vmem, wo, wgu, wd)


def _ffn_kernel(x_ref, g_ref, wgu_ref, wd_ref, gf_ref, out_ref, *, ff_chunk):
    x = x_ref[...]
    d_ff = wd_ref.shape[0]
    h = _rms_norm(x, g_ref[...], NORM_EPS).astype(BF16)
    acc = x
    for c0 in range(0, d_ff, ff_chunk):
        gate = jnp.dot(h, wgu_ref[:, c0:c0 + ff_chunk], preferred_element_type=F32)
        up = jnp.dot(h, wgu_ref[:, d_ff + c0:d_ff + c0 + ff_chunk], preferred_element_type=F32)
        act = (gate * _sigmoid(gate) * up).astype(BF16)
        acc = acc + jnp.dot(act, wd_ref[c0:c0 + ff_chunk, :], preferred_element_type=F32)
    out_ref[...] = _rms_norm(acc, gf_ref[...], NORM_EPS)


def _ffn(x, g, wgu, wd, gf, tm, ff_chunk):
    b, t, d = x.shape
    const = lambda *shape: _const_spec(shape, 2)
    tile = pl.BlockSpec((None, tm, d), lambda i, j: (i, j, 0))
    return pl.pallas_call(
        functools.partial(_ffn_kernel, ff_chunk=ff_chunk),
        grid=(b, t // tm),
        in_specs=[tile, const(1, d), const(*wgu.shape), const(*wd.shape), const(1, d)],
        out_specs=tile,
        out_shape=jax.ShapeDtypeStruct((b, t, d), F32),
        compiler_params=_params(2),
        name="ffn",
    )(x, g, wgu, wd, gf)


def _layer(x, mem, norm_mix_g, w_in, gla_wa2, gla_ba, gla_norm_g, rwkv_mu, rwkv_w0, rwkv_w2, rwkv_a0,
           rwkv_a2, rwkv_g2, rwkv_k_k, rwkv_k_a, rwkv_r_k, rwkv_ln_g, rwkv_ln_b, w_out, norm_mem_x_g,
           norm_mem_g, wq_mem, wkv_mem, wo_mem, norm_ffn_g, w_gate_up, w_down, final_g):
    b, t, d = x.shape
    m = mem.shape[1]
    row = lambda p: p.reshape(1, -1).astype(F32)
    vw = GLA_HEADS * GLA_DV
    wa2p = jnp.pad(gla_wa2, ((0, LANES - GLA_GATE_RANK), (0, 0))).astype(BF16)

    kmem, vmem = _mem_kv(mem.reshape(b * m, d), row(norm_mem_g), wkv_mem)
    zg, zr = _in_proj(x, row(norm_mix_g), w_in.T, wa2p, row(gla_ba), row(rwkv_mu), tm=min(ROWS_IN_PROJ, t))
    o_gla = _gla(zg, row(gla_norm_g), tt=min(ROWS_GLA, t))
    o_rwkv = _rwkv(zr, row(rwkv_w0), rwkv_w2.astype(BF16), row(rwkv_a0), rwkv_a2.astype(BF16),
                   rwkv_g2.astype(BF16), row(rwkv_k_k), row(rwkv_k_a), row(rwkv_r_k), row(rwkv_ln_g),
                   row(rwkv_ln_b), tt=min(ROWS_RWKV, t))
    w_out_b = w_out.astype(BF16)
    x2, wgu_b, wd_b = _post_attn(x, o_gla, o_rwkv, w_out_b[:vw], w_out_b[vw:], row(norm_mem_x_g),
                                 wq_mem.astype(BF16), kmem.reshape(b, m, d), vmem.reshape(b, m, d),
                                 wo_mem.astype(BF16), w_gate_up, w_down, tm=min(ROWS_POST_ATTN, t))
    return _ffn(x2, row(norm_ffn_g), wgu_b, wd_b, row(final_g), tm=min(ROWS_FFN, t), ff_chunk=FF_CHUNK)


def kernel(x, mem, norm_mix_g, w_in, gla_wa2, gla_ba, gla_norm_g, rwkv_mu, rwkv_w0, rwkv_w2, rwkv_a0, rwkv_a2, rwkv_g2, rwkv_k_k, rwkv_k_a, rwkv_r_k, rwkv_ln_g, rwkv_ln_b, w_out, norm_mem_x_g, norm_mem_g, wq_mem, wkv_mem, wo_mem, norm_ffn_g, w_gate_up, w_down, norm_final_g):
    assert norm_mix_g.shape[0] == 1, "single-layer block"
    return _layer(x, mem, norm_mix_g[0], w_in[0], gla_wa2[0], gla_ba[0], gla_norm_g[0], rwkv_mu[0],
                  rwkv_w0[0], rwkv_w2[0], rwkv_a0[0], rwkv_a2[0], rwkv_g2[0], rwkv_k_k[0], rwkv_k_a[0],
                  rwkv_r_k[0], rwkv_ln_g[0], rwkv_ln_b[0], w_out[0], norm_mem_x_g[0], norm_mem_g[0],
                  wq_mem[0], wkv_mem[0], wo_mem[0], norm_ffn_g[0], w_gate_up[0], w_down[0], norm_final_g)
```

```python
import functools
import math

import jax
import jax.numpy as jnp
from jax import lax
from jax.experimental import pallas as pl
from jax.experimental.pallas import tpu as pltpu

F32 = jnp.float32
BF16 = jnp.bfloat16

MEM_HEADS = 4
GLA_HEADS = 4
GLA_DK = 64
GLA_DV = 128
GLA_GATE_RANK = 16
GLA_GATE_NORMALIZER = 16.0
GLA_NORM_EPS = 1e-5
RWKV_HEAD = 64
RWKV_DECAY_RANK = 64
RWKV_AAA_RANK = 64
RWKV_GATE_RANK = 128
RWKV_LN_EPS = 64e-5
NORM_EPS = 1e-6

CHUNK = 64
GLA_CHUNKS_PER_ITER = 2
GLA_UNROLL = 4
RWKV_UNROLL = 8
SUBTILE_ROWS = 256
POST_ATTN_SUBTILE_ROWS = 512
ROWS_IN_PROJ = 512
ROWS_GLA = 1024
ROWS_RWKV = 512
ROWS_POST_ATTN = 1024
ROWS_FFN = 1024
FF_CHUNK = 256
LANES = 128
VMEM_LIMIT = 56 * 1024 * 1024


def _params(n_grid_dims):
    return pltpu.CompilerParams(dimension_semantics=("arbitrary",) * n_grid_dims,
                                vmem_limit_bytes=VMEM_LIMIT)


def _const_spec(shape, grid_rank):
    zeros = (0,) * len(shape)
    index_map = (lambda j: zeros) if grid_rank == 1 else (lambda i, j: zeros)
    return pl.BlockSpec(shape, index_map, pipeline_mode=pl.Buffered(1))


def _dot(a, b):
    return jnp.dot(a.astype(BF16), b.astype(BF16), preferred_element_type=F32)


def _rms_norm(x, g, eps):
    return x * lax.rsqrt(jnp.mean(x * x, axis=-1, keepdims=True) + eps) * g


def _sigmoid(x):
    return 1.0 / (1.0 + jnp.exp(-x))


def _log_sigmoid(x):
    return jnp.minimum(x, 0.0) - jnp.log1p(jnp.exp(-jnp.abs(x)))


def _iota2(shape, dim):
    return lax.broadcasted_iota(jnp.int32, shape, dim)


def _cumsum_operator(c):
    return (_iota2((c, 3 * c), 0) >= _iota2((c, 3 * c), 1) % c).astype(BF16)


def _chunk_cumsum(tri3, x):
    p1 = x.astype(BF16)
    rem = x - p1.astype(F32)
    p2 = rem.astype(BF16)
    p3 = (rem - p2.astype(F32)).astype(BF16)
    return jnp.dot(tri3, jnp.concatenate([p1, p2, p3], axis=0), preferred_element_type=F32)


def _mem_kv_kernel(mem_ref, g_ref, w_ref, k_ref, v_ref):
    d = k_ref.shape[-1]
    m = _rms_norm(mem_ref[...], g_ref[...], NORM_EPS)
    kv = _dot(m, w_ref[...])
    k_ref[...] = kv[:, :d].astype(BF16)
    v_ref[...] = kv[:, d:].astype(BF16)


def _mem_kv(mem2d, g, wkv):
    n, d = mem2d.shape
    return pl.pallas_call(
        _mem_kv_kernel,
        out_shape=(jax.ShapeDtypeStruct((n, d), BF16), jax.ShapeDtypeStruct((n, d), BF16)),
        compiler_params=pltpu.CompilerParams(vmem_limit_bytes=VMEM_LIMIT),
        name="mem_kv",
    )(mem2d, g, wkv)


def _in_proj_kernel(x_ref, g_ref, w_ref, wa2_ref, ba_ref, mu_ref, zg_ref, zr_ref, prev_ref, wg_ref, wr_ref):
    tm = x_ref.shape[0]
    n_main = zg_ref.shape[-1] - wa2_ref.shape[-1]
    n_gla = n_main + GLA_GATE_RANK

    @pl.when(jnp.logical_and(pl.program_id(0) == 0, pl.program_id(1) == 0))
    def _():
        for dst, c_first in ((wg_ref, 0), (wr_ref, n_gla)):
            for c0 in range(0, dst.shape[1], LANES):
                slab = w_ref[c_first + c0:c_first + c0 + LANES, :]
                dst[:, c0:c0 + LANES] = slab.T.astype(BF16)

    @pl.when(pl.program_id(1) == 0)
    def _():
        prev_ref[...] = jnp.zeros_like(prev_ref)

    sub = min(SUBTILE_ROWS, tm)
    prev = prev_ref[0:1, :]
    for s in range(tm // sub):
        rows = slice(s * sub, (s + 1) * sub)
        h = _rms_norm(x_ref[rows, :], g_ref[...], NORM_EPS).astype(BF16)

        zr = jnp.dot(h, wr_ref[...], preferred_element_type=F32)
        rolled = pltpu.roll(zr, shift=1, axis=0)
        first = _iota2(zr.shape, 0) == 0
        shifted = jnp.where(first, jnp.broadcast_to(prev, zr.shape), rolled)
        zr_ref[rows, :] = zr + (shifted - zr) * mu_ref[...]
        prev = zr[sub - 1:sub, :]

        a_lo = jnp.dot(h, wg_ref[:, n_main:], preferred_element_type=F32)
        pre = _dot(a_lo, wa2_ref[...]) + ba_ref[...]
        zg_ref[rows, n_main:] = _log_sigmoid(pre) * (1.0 / GLA_GATE_NORMALIZER)
        zg_ref[rows, :n_main] = jnp.dot(h, wg_ref[:, :n_main], preferred_element_type=F32)
    prev_ref[0:1, :] = prev


def _in_proj(x, g, w_in_t, wa2p, ba, mu, tm):
    b, t, d = x.shape
    nr = mu.shape[1]
    n_main = w_in_t.shape[0] - nr - GLA_GATE_RANK
    ng = n_main + wa2p.shape[1]
    const = lambda *shape: _const_spec(shape, 2)
    return pl.pallas_call(
        _in_proj_kernel,
        grid=(b, t // tm),
        in_specs=[
            pl.BlockSpec((None, tm, d), lambda i, j: (i, j, 0)),
            const(1, d), const(*w_in_t.shape), const(*wa2p.shape),
            const(1, wa2p.shape[1]), const(1, nr),
        ],
        out_specs=(pl.BlockSpec((None, tm, ng), lambda i, j: (i, j, 0)),
                   pl.BlockSpec((None, tm, nr), lambda i, j: (i, j, 0))),
        out_shape=(jax.ShapeDtypeStruct((b, t, ng), F32), jax.ShapeDtypeStruct((b, t, nr), F32)),
        scratch_shapes=[pltpu.VMEM((8, nr), F32),
                        pltpu.VMEM((d, n_main + LANES), BF16),
                        pltpu.VMEM((d, nr), BF16)],
        compiler_params=_params(2),
        name="in_proj",
    )(x, g, w_in_t, wa2p, ba, mu)


def _gla_kernel(z_ref, ng_ref, o_ref, s_ref):
    nb, tt, _ = z_ref.shape
    kw = GLA_HEADS * GLA_DK
    vw = GLA_HEADS * GLA_DV
    c = CHUNK
    pairs = GLA_HEADS // 2
    pk = 2 * GLA_DK
    pv = 2 * GLA_DV
    cpi = GLA_CHUNKS_PER_ITER

    @pl.when(pl.program_id(0) == 0)
    def _():
        s_ref[...] = jnp.zeros_like(s_ref)

    tri3 = _cumsum_operator(c)
    causal = _iota2((c, pk), 0) >= _iota2((c, pk), 1) % c
    lane_lo = _iota2((c, pk), 1) < GLA_DK
    v_lo = _iota2((c, pv), 1) < GLA_DV
    s_mask = (_iota2((pv, pk), 0) // GLA_DV) == (_iota2((pv, pk), 1) // GLA_DK)
    norm_g = ng_ref[...]
    nt_dims = (((1,), (1,)), ((), ()))
    tn_dims = (((0,), (0,)), ((), ()))

    def step(it, carry):
        units = [(b, j, p) for b in range(nb) for j in range(cpi) for p in range(pairs)]
        rows = {(b, j): pl.ds(pl.multiple_of((it * cpi + j) * c, c), c) for b in range(nb) for j in range(cpi)}
        q_dec, k_inv, k_tail, dec, vv = {}, {}, {}, {}, {}
        for (b, j), rs in rows.items():
            q = z_ref[b, rs, 0:kw] * (GLA_DK ** -0.5)
            k = z_ref[b, rs, kw:2 * kw]
            log_a = z_ref[b, rs, 2 * kw + 2 * vw:3 * kw + 2 * vw]
            bcum = _chunk_cumsum(tri3, log_a)
            blast = bcum[c - 1:c, :]
            qd = q * jnp.exp(bcum)
            ki = k * jnp.exp(-bcum)
            kt = k * jnp.exp(blast - bcum)
            dc = jnp.exp(blast)
            for p in range(pairs):
                ks = slice(p * pk, (p + 1) * pk)
                q_dec[b, j, p] = qd[:, ks].astype(BF16)
                k_inv[b, j, p] = ki[:, ks]
                k_tail[b, j, p] = kt[:, ks].astype(BF16)
                dec[b, j, p] = dc[:, ks]
                vv[b, j, p] = z_ref[b, rs, 2 * kw + p * pv:2 * kw + (p + 1) * pv].astype(BF16)

        o_intra, d_state = {}, {}
        for u in units:
            scores = jnp.where(causal, lax.dot_general(q_dec[u], _bd(k_inv[u], lane_lo), nt_dims,
                                                       preferred_element_type=F32), 0.0)
            zero = jnp.zeros_like(vv[u])
            v_bd = jnp.concatenate([jnp.where(v_lo, vv[u], zero), jnp.where(v_lo, zero, vv[u])], axis=0)
            o_intra[u] = _mm1(scores, v_bd)
            d_state[u] = jnp.where(s_mask, lax.dot_general(vv[u], k_tail[u], tn_dims,
                                                           preferred_element_type=F32), 0.0)

        for b in range(nb):
            s_cur = [s_ref[b * pairs + p] for p in range(pairs)]
            for j in range(cpi):
                outs = []
                for p in range(pairs):
                    u = (b, j, p)
                    o = o_intra[u] + lax.dot_general(q_dec[u], s_cur[p].astype(BF16), nt_dims,
                                                     preferred_element_type=F32)
                    s_cur[p] = s_cur[p] * dec[u] + d_state[u]
                    for h in range(2):
                        oh = o[:, h * GLA_DV:(h + 1) * GLA_DV]
                        oh = oh * lax.rsqrt(jnp.mean(oh * oh, axis=-1, keepdims=True) + GLA_NORM_EPS) * norm_g
                        g0 = 2 * kw + vw + (2 * p + h) * GLA_DV
                        gh = z_ref[b, rows[b, j], g0:g0 + GLA_DV]
                        outs.append(oh * (gh * _sigmoid(gh)))
                o_ref[b, rows[b, j], :] = jnp.concatenate(outs, axis=-1).astype(o_ref.dtype)
            for p in range(pairs):
                s_ref[b * pairs + p] = s_cur[p]
        return carry

    lax.fori_loop(0, tt // (c * cpi), step, 0, unroll=min(GLA_UNROLL, tt // (c * cpi)))


def _gla(zg, norm_g, tt):
    b, t, n = zg.shape
    vw = GLA_HEADS * GLA_DV
    return pl.pallas_call(
        _gla_kernel,
        grid=(t // tt,),
        in_specs=[pl.BlockSpec((b, tt, n), lambda j: (0, j, 0)),
                  _const_spec((1, GLA_DV), 1)],
        out_specs=pl.BlockSpec((b, tt, vw), lambda j: (0, j, 0)),
        out_shape=jax.ShapeDtypeStruct((b, t, vw), BF16),
        scratch_shapes=[pltpu.VMEM((b * GLA_HEADS // 2, 2 * GLA_DV, 2 * GLA_DK), F32)],
        compiler_params=_params(1),
        name="gla",
    )(zg, norm_g)


def _seg_sum(x, seg_ones):
    xb = x.astype(BF16)
    blk = seg_ones.shape[0]
    parts = []
    for s in range(x.shape[1] // blk):
        sl = slice(s * blk, (s + 1) * blk)
        parts.append(jnp.dot(xb[:, sl], seg_ones, preferred_element_type=F32))
    return jnp.concatenate(parts, axis=-1)


def _block_diag(z, lane_lo):
    m = lane_lo.astype(z.dtype)
    return jnp.concatenate([z * m, z * (1 - m)], axis=0)


def _bd(z, lane_lo):
    return _block_diag(z.astype(BF16), lane_lo)


def _mm1(x, r):
    return jnp.dot(x.astype(BF16), r, preferred_element_type=F32)


def _rwkv_kernel(z_ref, w0_ref, w2_ref, a0_ref, a2_ref, g2_ref, kk_ref, ka_ref, rk_ref, lng_ref, lnb_ref,
                 o_ref, s_ref, st_ref, y_ref):
    nb, tt, _ = z_ref.shape
    w = w0_ref.shape[-1]
    n = RWKV_HEAD
    pairs = w // LANES
    c = CHUNK
    o_wl = 3 * w
    o_al = o_wl + RWKV_DECAY_RANK
    o_gl = o_al + RWKV_AAA_RANK

    @pl.when(pl.program_id(0) == 0)
    def _():
        s_ref[...] = jnp.zeros_like(s_ref)

    seg = 2 * LANES
    seg_ones = (_iota2((seg, seg), 0) // n == _iota2((seg, seg), 1) // n).astype(BF16)

    cols = lambda lo, hi: z_ref[:, :, lo:hi].reshape(nb * tt, hi - lo)
    r = cols(0, w)
    k = cols(w, 2 * w)
    v = cols(2 * w, 3 * w)
    u = w0_ref[...] + _dot(jnp.tanh(cols(o_wl, o_al)), w2_ref[...])
    lw = -math.exp(-0.5) * _sigmoid(u)
    a = _sigmoid(a0_ref[...] + _dot(cols(o_al, o_gl), a2_ref[...]))
    kk = k * kk_ref[...]
    kk = kk * lax.rsqrt(jnp.maximum(_seg_sum(kk * kk, seg_ones), 1e-24))
    k2 = k * (1.0 + (a - 1.0) * ka_ref[...])
    st_ref[0] = r
    st_ref[1] = lw
    st_ref[2] = k2
    st_ref[3] = v
    st_ref[4] = -kk
    st_ref[5] = kk * a

    tri3 = _cumsum_operator(c)
    row = _iota2((c, LANES), 0)
    col = _iota2((c, LANES), 1) % c
    lower = row >= col
    strict = row > col
    eye = (row == col).astype(F32)
    blk16 = (row // 16) == (col // 16)
    blk32 = (row // 32) == (col // 32)
    off16 = jnp.logical_and(blk32, jnp.logical_not(blk16))
    lane_lo = _iota2((c, LANES), 1) < n
    bd_mask = (_iota2((LANES, LANES), 0) // n) == (_iota2((LANES, LANES), 1) // n)
    units = [(b, p) for b in range(nb) for p in range(pairs)]
    prange = range(len(units))
    nt_dims = (((1,), (1,)), ((), ()))

    def chunk(ci, carry):
        rows = [pl.ds(pl.multiple_of(b * tt + ci * c, c), c) for b in range(nb)]
        a_t, r_t, b_t, k_t, b_h, k_h, vc, g_c = [], [], [], [], [], [], [], []
        for b in range(nb):
            rc = st_ref[0, rows[b], :]
            lwc = st_ref[1, rows[b], :]
            kc = st_ref[2, rows[b], :]
            ac = st_ref[4, rows[b], :]
            bc = st_ref[5, rows[b], :]
            cum = _chunk_cumsum(tri3, lwc)
            last = cum[c - 1:c, :]
            e_neg = jnp.exp(-cum)
            e_tail = jnp.exp(last - cum)
            r_tb = rc * jnp.exp(cum)
            a_tb = ac * jnp.exp(cum - lwc)
            vb = st_ref[3, rows[b], :]
            for p in range(pairs):
                sl = slice(p * LANES, (p + 1) * LANES)
                a_t.append(a_tb[:, sl])
                r_t.append(r_tb[:, sl])
                b_t.append(bc[:, sl] * e_neg[:, sl])
                k_t.append(kc[:, sl] * e_neg[:, sl])
                b_h.append(bc[:, sl] * e_tail[:, sl])
                k_h.append(kc[:, sl] * e_tail[:, sl])
                vc.append(vb[:, sl])
                g_c.append(jnp.exp(last[:, sl]))

        a_ab, a_ak, q_b, q_k = [], [], [], []
        for p in prange:
            lhs = jnp.concatenate([a_t[p], r_t[p]], axis=0).astype(BF16)
            rhs = jnp.concatenate([_bd(b_t[p], lane_lo), _bd(k_t[p], lane_lo)], axis=0)
            aa = lax.dot_general(lhs, rhs, nt_dims, preferred_element_type=F32)
            a_ab.append(jnp.where(strict, aa[:c, :LANES], 0.0))
            a_ak.append(jnp.where(strict, aa[:c, LANES:], 0.0))
            q_b.append(jnp.where(lower, aa[c:, :LANES], 0.0))
            q_k.append(jnp.where(lower, aa[c:, LANES:], 0.0))

        ad = [jnp.where(blk16, x, 0.0) for x in a_ab]
        pw = [_mm1(ad[p], _bd(ad[p], lane_lo)) for p in prange]
        t = [eye + ad[p] for p in prange]
        for _ in range(2):
            both = [_mm1(jnp.concatenate([pw[p], t[p]], axis=0), _bd(pw[p], lane_lo)) for p in prange]
            pw = [x[:c] for x in both]
            t = [t[p] + both[p][c:] for p in prange]
        t = [t[p] + _mm1(t[p], _bd(pw[p], lane_lo)) for p in prange]
        for msk in (off16, jnp.logical_not(blk32)):
            te = [_mm1(t[p], _bd(jnp.where(msk, a_ab[p], 0.0), lane_lo)) for p in prange]
            t = [t[p] + _mm1(te[p], _bd(t[p], lane_lo)) for p in prange]

        v_r = [_bd(vc[p], lane_lo) for p in prange]
        pv = [_mm1(a_ak[p], v_r[p]) for p in prange]
        wu = [_mm1(t[p], jnp.concatenate([_bd(a_t[p], lane_lo), _bd(pv[p], lane_lo)], axis=1))
              for p in prange]

        for p in prange:
            bi, pi = units[p]
            s0 = s_ref[p]
            ws = lax.dot_general(jnp.concatenate([wu[p][:, :LANES], r_t[p]], axis=0).astype(BF16),
                                 s0.astype(BF16), nt_dims, preferred_element_type=F32)
            u_mat = ws[:c] + wu[p][:, LANES:]
            y = ws[c:] + _mm1(jnp.concatenate([q_b[p], q_k[p]], axis=1),
                              jnp.concatenate([_bd(u_mat, lane_lo), v_r[p]], axis=0))
            y_ref[rows[bi], pi * LANES:(pi + 1) * LANES] = y
            ds = _mm1(jnp.concatenate([u_mat, vc[p]], axis=0).T,
                      jnp.concatenate([b_h[p], k_h[p]], axis=0).astype(BF16))
            s_ref[p] = s0 * g_c[p] + jnp.where(bd_mask, ds, 0.0)
        return carry

    lax.fori_loop(0, tt // c, chunk, 0, unroll=min(RWKV_UNROLL, tt // c))

    y = y_ref[...]
    mean = _seg_sum(y, seg_ones) * (1.0 / n)
    dlt = y - mean
    var = _seg_sum(dlt * dlt, seg_ones) * (1.0 / n)
    yn = dlt * lax.rsqrt(var + RWKV_LN_EPS) * lng_ref[...] + lnb_ref[...]
    bonus = _seg_sum(st_ref[0] * st_ref[2] * rk_ref[...], seg_ones) * st_ref[3]
    gate = _dot(_sigmoid(cols(o_gl, o_gl + RWKV_GATE_RANK)), g2_ref[...])
    o_ref[...] = ((yn + bonus) * gate).reshape(nb, tt, w).astype(o_ref.dtype)


def _rwkv(zr, w0, w2, a0, a2, g2, k_k, k_a, r_k, ln_g, ln_b, tt):
    b, t, nz = zr.shape
    w = w0.shape[-1]
    const = lambda *shape: _const_spec(shape, 1)
    return pl.pallas_call(
        _rwkv_kernel,
        grid=(t // tt,),
        in_specs=[pl.BlockSpec((b, tt, nz), lambda j: (0, j, 0)),
                  const(1, w), const(*w2.shape), const(1, w), const(*a2.shape), const(*g2.shape),
                  const(1, w), const(1, w), const(1, w), const(1, w), const(1, w)],
        out_specs=pl.BlockSpec((b, tt, w), lambda j: (0, j, 0)),
        out_shape=jax.ShapeDtypeStruct((b, t, w), BF16),
        scratch_shapes=[pltpu.VMEM((b * w // LANES, LANES, LANES), F32),
                        pltpu.VMEM((6, b * tt, w), F32),
                        pltpu.VMEM((b * tt, w), F32)],
        compiler_params=_params(1),
        name="rwkv",
    )(zr, w0, w2, a0, a2, g2, k_k, k_a, r_k, ln_g, ln_b)


def _post_attn_kernel(x_ref, og_ref, or_ref, wog_ref, wor_ref, gx_ref, wq_ref, k_ref, v_ref, wo_ref,
                      wgu_ref, wd_ref, out_ref, wgu_out_ref, wd_out_ref):
    wgu_out_ref[...] = wgu_ref[...].astype(BF16)
    wd_out_ref[...] = wd_ref[...].astype(BF16)
    tm, d = x_ref.shape
    hd = d // MEM_HEADS
    sub = min(POST_ATTN_SUBTILE_ROWS, tm)
    rows = [slice(s0, s0 + sub) for s0 in range(0, tm, sub)]
    nt_dims = (((1,), (1,)), ((), ()))
    x1 = [x_ref[r, :] + jnp.dot(og_ref[r, :], wog_ref[...], preferred_element_type=F32)
          + jnp.dot(or_ref[r, :], wor_ref[...], preferred_element_type=F32) for r in rows]
    q = [_dot(_rms_norm(x, gx_ref[...], NORM_EPS), wq_ref[...]).astype(BF16) for x in x1]
    heads = [[] for _ in rows]
    for h in range(MEM_HEADS):
        hs = slice(h * hd, (h + 1) * hd)
        s = [lax.dot_general(qi[:, hs], k_ref[:, hs], nt_dims, preferred_element_type=F32) * (hd ** -0.5)
             for qi in q]
        e = [jnp.exp(si - jnp.max(si, axis=-1, keepdims=True)) for si in s]
        p = [ei / jnp.sum(ei, axis=-1, keepdims=True) for ei in e]
        for i, pi in enumerate(p):
            heads[i].append(_dot(pi, v_ref[:, hs]))
    for i, r in enumerate(rows):
        out_ref[r, :] = x1[i] + _dot(jnp.concatenate(heads[i], axis=-1), wo_ref[...])


def _post_attn(x, o_gla, o_rwkv, wo_g, wo_r, gx, wq, kmem, vmem, wo, wgu, wd, tm):
    b, t, d = x.shape
    m = kmem.shape[1]
    nj = t // tm
    steps = b * nj
    bf16_rows = 16
    assert wgu.shape[0] % (steps * bf16_rows) == 0 and wd.shape[0] % (steps * bf16_rows) == 0
    const = lambda *shape: _const_spec(shape, 2)
    tile = lambda width: pl.BlockSpec((None, tm, width), lambda i, j: (i, j, 0))
    slab = lambda w: pl.BlockSpec((w.shape[0] // steps, w.shape[1]), lambda i, j: (i * nj + j, 0))
    return pl.pallas_call(
        _post_attn_kernel,
        grid=(b, nj),
        in_specs=[tile(d), tile(o_gla.shape[-1]), tile(o_rwkv.shape[-1]),
                  const(*wo_g.shape), const(*wo_r.shape), const(1, d), const(*wq.shape),
                  pl.BlockSpec((None, m, d), lambda i, j: (i, 0, 0)),
                  pl.BlockSpec((None, m, d), lambda i, j: (i, 0, 0)),
                  const(*wo.shape), slab(wgu), slab(wd)],
        out_specs=(tile(d), slab(wgu), slab(wd)),
        out_shape=(jax.ShapeDtypeStruct((b, t, d), F32), jax.ShapeDtypeStruct(wgu.shape, BF16),
                   jax.ShapeDtypeStruct(wd.shape, BF16)),
        compiler_params=_params(2),
        name="post_attn",
    )(x, o_gla, o_rwkv, wo_g, wo_r, gx, wq, kmem, vmem, wo, wgu, wd)


def _ffn_kernel(x_ref, g_ref, wgu_ref, wd_ref, gf_ref, out_ref, *, ff_chunk):
    x = x_ref[...]
    d_ff = wd_ref.shape[0]
    h = _rms_norm(x, g_ref[...], NORM_EPS).astype(BF16)
    acc = x
    for c0 in range(0, d_ff, ff_chunk):
        gate = jnp.dot(h, wgu_ref[:, c0:c0 + ff_chunk], preferred_element_type=F32)
        up = jnp.dot(h, wgu_ref[:, d_ff + c0:d_ff + c0 + ff_chunk], preferred_element_type=F32)
        act = (gate * _sigmoid(gate) * up).astype(BF16)
        acc = acc + jnp.dot(act, wd_ref[c0:c0 + ff_chunk, :], preferred_element_type=F32)
    out_ref[...] = _rms_norm(acc, gf_ref[...], NORM_EPS)


def _ffn(x, g, wgu, wd, gf, tm, ff_chunk):
    b, t, d = x.shape
    const = lambda *shape: _const_spec(shape, 2)
    tile = pl.BlockSpec((None, tm, d), lambda i, j: (i, j, 0))
    return pl.pallas_call(
        functools.partial(_ffn_kernel, ff_chunk=ff_chunk),
        grid=(b, t // tm),
        in_specs=[tile, const(1, d), const(*wgu.shape), const(*wd.shape), const(1, d)],
        out_specs=tile,
        out_shape=jax.ShapeDtypeStruct((b, t, d), F32),
        compiler_params=_params(2),
        name="ffn",
    )(x, g, wgu, wd, gf)


def _layer(x, mem, norm_mix_g, w_in, gla_wa2, gla_ba, gla_norm_g, rwkv_mu, rwkv_w0, rwkv_w2, rwkv_a0,
           rwkv_a2, rwkv_g2, rwkv_k_k, rwkv_k_a, rwkv_r_k, rwkv_ln_g, rwkv_ln_b, w_out, norm_mem_x_g,
           norm_mem_g, wq_mem, wkv_mem, wo_mem, norm_ffn_g, w_gate_up, w_down, final_g):
    b, t, d = x.shape
    m = mem.shape[1]
    row = lambda p: p.reshape(1, -1).astype(F32)
    vw = GLA_HEADS * GLA_DV
    wa2p = jnp.pad(gla_wa2, ((0, LANES - GLA_GATE_RANK), (0, 0))).astype(BF16)

    kmem, vmem = _mem_kv(mem.reshape(b * m, d), row(norm_mem_g), wkv_mem)
    zg, zr = _in_proj(x, row(norm_mix_g), w_in.T, wa2p, row(gla_ba), row(rwkv_mu), tm=min(ROWS_IN_PROJ, t))
    o_gla = _gla(zg, row(gla_norm_g), tt=min(ROWS_GLA, t))
    o_rwkv = _rwkv(zr, row(rwkv_w0), rwkv_w2.astype(BF16), row(rwkv_a0), rwkv_a2.astype(BF16),
                   rwkv_g2.astype(BF16), row(rwkv_k_k), row(rwkv_k_a), row(rwkv_r_k), row(rwkv_ln_g),
                   row(rwkv_ln_b), tt=min(ROWS_RWKV, t))
    w_out_b = w_out.astype(BF16)
    x2, wgu_b, wd_b = _post_attn(x, o_gla, o_rwkv, w_out_b[:vw], w_out_b[vw:], row(norm_mem_x_g),
                                 wq_mem.astype(BF16), kmem.reshape(b, m, d), vmem.reshape(b, m, d),
                                 wo_mem.astype(BF16), w_gate_up, w_down, tm=min(ROWS_POST_ATTN, t))
    return _ffn(x2, row(norm_ffn_g), wgu_b, wd_b, row(final_g), tm=min(ROWS_FFN, t), ff_chunk=FF_CHUNK)


def kernel(x, mem, norm_mix_g, w_in, gla_wa2, gla_ba, gla_norm_g, rwkv_mu, rwkv_w0, rwkv_w2, rwkv_a0, rwkv_a2, rwkv_g2, rwkv_k_k, rwkv_k_a, rwkv_r_k, rwkv_ln_g, rwkv_ln_b, w_out, norm_mem_x_g, norm_mem_g, wq_mem, wkv_mem, wo_mem, norm_ffn_g, w_gate_up, w_down, norm_final_g):
    assert norm_mix_g.shape[0] == 1, "single-layer block"
    return _layer(x, mem, norm_mix_g[0], w_in[0], gla_wa2[0], gla_ba[0], gla_norm_g[0], rwkv_mu[0],
                  rwkv_w0[0], rwkv_w2[0], rwkv_a0[0], rwkv_a2[0], rwkv_g2[0], rwkv_k_k[0], rwkv_k_a[0],
                  rwkv_r_k[0], rwkv_ln_g[0], rwkv_ln_b[0], w_out[0], norm_mem_x_g[0], norm_mem_g[0],
                  wq_mem[0], wkv_mem[0], wo_mem[0], norm_ffn_g[0], w_gate_up[0], w_down[0], norm_final_g)
```

```python
import functools
import math

import jax
import jax.numpy as jnp
from jax import lax
from jax.experimental import pallas as pl
from jax.experimental.pallas import tpu as pltpu

F32 = jnp.float32
BF16 = jnp.bfloat16

MEM_HEADS = 4
GLA_HEADS = 4
GLA_DK = 64
GLA_DV = 128
GLA_GATE_RANK = 16
GLA_GATE_NORMALIZER = 16.0
GLA_NORM_EPS = 1e-5
RWKV_HEAD = 64
RWKV_DECAY_RANK = 64
RWKV_AAA_RANK = 64
RWKV_GATE_RANK = 128
RWKV_LN_EPS = 64e-5
NORM_EPS = 1e-6

CHUNK = 64
GLA_CHUNKS_PER_ITER = 2
GLA_UNROLL = 4
RWKV_UNROLL = 8
SUBTILE_ROWS = 256
POST_ATTN_SUBTILE_ROWS = 512
ROWS_IN_PROJ = 512
ROWS_GLA = 1024
ROWS_RWKV = 512
ROWS_POST_ATTN = 1024
ROWS_FFN = 1024
FF_CHUNK = 256
LANES = 128
VMEM_LIMIT = 56 * 1024 * 1024


def _params(n_grid_dims):
    return pltpu.CompilerParams(dimension_semantics=("arbitrary",) * n_grid_dims,
                                vmem_limit_bytes=VMEM_LIMIT)


def _const_spec(shape, grid_rank):
    zeros = (0,) * len(shape)
    index_map = (lambda j: zeros) if grid_rank == 1 else (lambda i, j: zeros)
    return pl.BlockSpec(shape, index_map, pipeline_mode=pl.Buffered(1))


def _dot(a, b):
    return jnp.dot(a.astype(BF16), b.astype(BF16), preferred_element_type=F32)


def _rms_norm(x, g, eps):
    return x * lax.rsqrt(jnp.mean(x * x, axis=-1, keepdims=True) + eps) * g


def _sigmoid(x):
    return 1.0 / (1.0 + jnp.exp(-x))


def _log_sigmoid(x):
    return jnp.minimum(x, 0.0) - jnp.log1p(jnp.exp(-jnp.abs(x)))


def _iota2(shape, dim):
    return lax.broadcasted_iota(jnp.int32, shape, dim)


def _cumsum_operator(c):
    return (_iota2((c, 3 * c), 0) >= _iota2((c, 3 * c), 1) % c).astype(BF16)


def _chunk_cumsum(tri3, x):
    p1 = x.astype(BF16)
    rem = x - p1.astype(F32)
    p2 = rem.astype(BF16)
    p3 = (rem - p2.astype(F32)).astype(BF16)
    return jnp.dot(tri3, jnp.concatenate([p1, p2, p3], axis=0), preferred_element_type=F32)


def _mem_kv_kernel(mem_ref, g_ref, w_ref, k_ref, v_ref):
    d = k_ref.shape[-1]
    m = _rms_norm(mem_ref[...], g_ref[...], NORM_EPS)
    kv = _dot(m, w_ref[...])
    k_ref[...] = kv[:, :d].astype(BF16)
    v_ref[...] = kv[:, d:].astype(BF16)


def _mem_kv(mem2d, g, wkv):
    n, d = mem2d.shape
    return pl.pallas_call(
        _mem_kv_kernel,
        out_shape=(jax.ShapeDtypeStruct((n, d), BF16), jax.ShapeDtypeStruct((n, d), BF16)),
        compiler_params=pltpu.CompilerParams(vmem_limit_bytes=VMEM_LIMIT),
        name="mem_kv",
    )(mem2d, g, wkv)


def _in_proj_kernel(x_ref, g_ref, w_ref, wa2_ref, ba_ref, mu_ref, zg_ref, zr_ref, prev_ref, wg_ref, wr_ref):
    tm = x_ref.shape[0]
    n_main = zg_ref.shape[-1] - wa2_ref.shape[-1]
    n_gla = n_main + GLA_GATE_RANK

    @pl.when(jnp.logical_and(pl.program_id(0) == 0, pl.program_id(1) == 0))
    def _():
        for dst, c_first in ((wg_ref, 0), (wr_ref, n_gla)):
            for c0 in range(0, dst.shape[1], LANES):
                slab = w_ref[c_first + c0:c_first + c0 + LANES, :]
                dst[:, c0:c0 + LANES] = slab.T.astype(BF16)

    @pl.when(pl.program_id(1) == 0)
    def _():
        prev_ref[...] = jnp.zeros_like(prev_ref)

    sub = min(SUBTILE_ROWS, tm)
    prev = prev_ref[0:1, :]
    for s in range(tm // sub):
        rows = slice(s * sub, (s + 1) * sub)
        h = _rms_norm(x_ref[rows, :], g_ref[...], NORM_EPS).astype(BF16)

        zr = jnp.dot(h, wr_ref[...], preferred_element_type=F32)
        rolled = pltpu.roll(zr, shift=1, axis=0)
        first = _iota2(zr.shape, 0) == 0
        shifted = jnp.where(first, jnp.broadcast_to(prev, zr.shape), rolled)
        zr_ref[rows, :] = zr + (shifted - zr) * mu_ref[...]
        prev = zr[sub - 1:sub, :]

        a_lo = jnp.dot(h, wg_ref[:, n_main:], preferred_element_type=F32)
        pre = _dot(a_lo, wa2_ref[...]) + ba_ref[...]
        zg_ref[rows, n_main:] = _log_sigmoid(pre) * (1.0 / GLA_GATE_NORMALIZER)
        zg_ref[rows, :n_main] = jnp.dot(h, wg_ref[:, :n_main], preferred_element_type=F32)
    prev_ref[0:1, :] = prev


def _in_proj(x, g, w_in_t, wa2p, ba, mu, tm):
    b, t, d = x.shape
    nr = mu.shape[1]
    n_main = w_in_t.shape[0] - nr - GLA_GATE_RANK
    ng = n_main + wa2p.shape[1]
    const = lambda *shape: _const_spec(shape, 2)
    return pl.pallas_call(
        _in_proj_kernel,
        grid=(b, t // tm),
        in_specs=[
            pl.BlockSpec((None, tm, d), lambda i, j: (i, j, 0)),
            const(1, d), const(*w_in_t.shape), const(*wa2p.shape),
            const(1, wa2p.shape[1]), const(1, nr),
        ],
        out_specs=(pl.BlockSpec((None, tm, ng), lambda i, j: (i, j, 0)),
                   pl.BlockSpec((None, tm, nr), lambda i, j: (i, j, 0))),
        out_shape=(jax.ShapeDtypeStruct((b, t, ng), F32), jax.ShapeDtypeStruct((b, t, nr), F32)),
        scratch_shapes=[pltpu.VMEM((8, nr), F32),
                        pltpu.VMEM((d, n_main + LANES), BF16),
                        pltpu.VMEM((d, nr), BF16)],
        compiler_params=_params(2),
        name="in_proj",
    )(x, g, w_in_t, wa2p, ba, mu)


def _gla_kernel(z_ref, ng_ref, o_ref, s_ref):
    nb, tt, _ = z_ref.shape
    kw = GLA_HEADS * GLA_DK
    vw = GLA_HEADS * GLA_DV
    c = CHUNK
    pairs = GLA_HEADS // 2
    pk = 2 * GLA_DK
    pv = 2 * GLA_DV
    cpi = GLA_CHUNKS_PER_ITER

    @pl.when(pl.program_id(0) == 0)
    def _():
        s_ref[...] = jnp.zeros_like(s_ref)

    tri3 = _cumsum_operator(c)
    causal = _iota2((c, pk), 0) >= _iota2((c, pk), 1) % c
    lane_lo = _iota2((c, pk), 1) < GLA_DK
    v_lo = _iota2((c, pv), 1) < GLA_DV
    s_mask = (_iota2((pv, pk), 0) // GLA_DV) == (_iota2((pv, pk), 1) // GLA_DK)
    norm_g = ng_ref[...]
    nt_dims = (((1,), (1,)), ((), ()))
    tn_dims = (((0,), (0,)), ((), ()))

    def step(it, carry):
        units = [(b, j, p) for b in range(nb) for j in range(cpi) for p in range(pairs)]
        rows = {(b, j): pl.ds(pl.multiple_of((it * cpi + j) * c, c), c) for b in range(nb) for j in range(cpi)}
        q_dec, k_inv, k_tail, dec, vv = {}, {}, {}, {}, {}
        for (b, j), rs in rows.items():
            q = z_ref[b, rs, 0:kw] * (GLA_DK ** -0.5)
            k = z_ref[b, rs, kw:2 * kw]
            log_a = z_ref[b, rs, 2 * kw + 2 * vw:3 * kw + 2 * vw]
            bcum = _chunk_cumsum(tri3, log_a)
            blast = bcum[c - 1:c, :]
            qd = q * jnp.exp(bcum)
            ki = k * jnp.exp(-bcum)
            kt = k * jnp.exp(blast - bcum)
            dc = jnp.exp(blast)
            for p in range(pairs):
                ks = slice(p * pk, (p + 1) * pk)
                q_dec[b, j, p] = qd[:, ks].astype(BF16)
                k_inv[b, j, p] = ki[:, ks]
                k_tail[b, j, p] = kt[:, ks].astype(BF16)
                dec[b, j, p] = dc[:, ks]
                vv[b, j, p] = z_ref[b, rs, 2 * kw + p * pv:2 * kw + (p + 1) * pv].astype(BF16)

        o_intra, d_state = {}, {}
        for u in units:
            scores = jnp.where(causal, lax.dot_general(q_dec[u], _bd(k_inv[u], lane_lo), nt_dims,
                                                       preferred_element_type=F32), 0.0)
            zero = jnp.zeros_like(vv[u])
            v_bd = jnp.concatenate([jnp.where(v_lo, vv[u], zero), jnp.where(v_lo, zero, vv[u])], axis=0)
            o_intra[u] = _mm1(scores, v_bd)
            d_state[u] = jnp.where(s_mask, lax.dot_general(vv[u], k_tail[u], tn_dims,
                                                           preferred_element_type=F32), 0.0)

        for b in range(nb):
            s_cur = [s_ref[b * pairs + p] for p in range(pairs)]
            for j in range(cpi):
                outs = []
                for p in range(pairs):
                    u = (b, j, p)
                    o = o_intra[u] + lax.dot_general(q_dec[u], s_cur[p].astype(BF16), nt_dims,
                                                     preferred_element_type=F32)
                    s_cur[p] = s_cur[p] * dec[u] + d_state[u]
                    for h in range(2):
                        oh = o[:, h * GLA_DV:(h + 1) * GLA_DV]
                        oh = oh * lax.rsqrt(jnp.mean(oh * oh, axis=-1, keepdims=True) + GLA_NORM_EPS) * norm_g
                        g0 = 2 * kw + vw + (2 * p + h) * GLA_DV
                        gh = z_ref[b, rows[b, j], g0:g0 + GLA_DV]
                        outs.append(oh * (gh * _sigmoid(gh)))
                o_ref[b, rows[b, j], :] = jnp.concatenate(outs, axis=-1).astype(o_ref.dtype)
            for p in range(pairs):
                s_ref[b * pairs + p] = s_cur[p]
        return carry

    lax.fori_loop(0, tt // (c * cpi), step, 0, unroll=min(GLA_UNROLL, tt // (c * cpi)))


def _gla(zg, norm_g, tt):
    b, t, n = zg.shape
    vw = GLA_HEADS * GLA_DV
    return pl.pallas_call(
        _gla_kernel,
        grid=(t // tt,),
        in_specs=[pl.BlockSpec((b, tt, n), lambda j: (0, j, 0)),
                  _const_spec((1, GLA_DV), 1)],
        out_specs=pl.BlockSpec((b, tt, vw), lambda j: (0, j, 0)),
        out_shape=jax.ShapeDtypeStruct((b, t, vw), BF16),
        scratch_shapes=[pltpu.VMEM((b * GLA_HEADS // 2, 2 * GLA_DV, 2 * GLA_DK), F32)],
        compiler_params=_params(1),
        name="gla",
    )(zg, norm_g)


def _seg_sum(x, seg_ones):
    xb = x.astype(BF16)
    blk = seg_ones.shape[0]
    parts = []
    for s in range(x.shape[1] // blk):
        sl = slice(s * blk, (s + 1) * blk)
        parts.append(jnp.dot(xb[:, sl], seg_ones, preferred_element_type=F32))
    return jnp.concatenate(parts, axis=-1)


def _block_diag(z, lane_lo):
    zero = jnp.zeros_like(z)
    return jnp.concatenate([jnp.where(lane_lo, z, zero), jnp.where(lane_lo, zero, z)], axis=0)


def _bd(z, lane_lo):
    return _block_diag(z.astype(BF16), lane_lo)


def _pair_mm(xs, z, lane_lo):
    n = RWKV_HEAD
    lhs = jnp.concatenate([piece for x in xs for piece in (x[:, :n], x[:, n:])], axis=0).astype(BF16)
    out = jnp.dot(lhs, z.astype(BF16), preferred_element_type=F32)
    c = xs[0].shape[0]
    return [jnp.where(lane_lo, out[2 * i * c:(2 * i + 1) * c], out[(2 * i + 1) * c:(2 * i + 2) * c])
            for i in range(len(xs))]


def _mm1(x, r):
    return jnp.dot(x.astype(BF16), r, preferred_element_type=F32)


def _rwkv_kernel(z_ref, w0_ref, w2_ref, a0_ref, a2_ref, g2_ref, kk_ref, ka_ref, rk_ref, lng_ref, lnb_ref,
                 o_ref, s_ref, st_ref, y_ref):
    nb, tt, _ = z_ref.shape
    w = w0_ref.shape[-1]
    n = RWKV_HEAD
    pairs = w // LANES
    c = CHUNK
    o_wl = 3 * w
    o_al = o_wl + RWKV_DECAY_RANK
    o_gl = o_al + RWKV_AAA_RANK

    @pl.when(pl.program_id(0) == 0)
    def _():
        s_ref[...] = jnp.zeros_like(s_ref)

    seg = 2 * LANES
    seg_ones = (_iota2((seg, seg), 0) // n == _iota2((seg, seg), 1) // n).astype(BF16)

    cols = lambda lo, hi: z_ref[:, :, lo:hi].reshape(nb * tt, hi - lo)
    r = cols(0, w)
    k = cols(w, 2 * w)
    v = cols(2 * w, 3 * w)
    u = w0_ref[...] + _dot(jnp.tanh(cols(o_wl, o_al)), w2_ref[...])
    lw = -math.exp(-0.5) * _sigmoid(u)
    a = _sigmoid(a0_ref[...] + _dot(cols(o_al, o_gl), a2_ref[...]))
    kk = k * kk_ref[...]
    kk = kk * lax.rsqrt(jnp.maximum(_seg_sum(kk * kk, seg_ones), 1e-24))
    k2 = k * (1.0 + (a - 1.0) * ka_ref[...])
    st_ref[0] = r
    st_ref[1] = lw
    st_ref[2] = k2
    st_ref[3] = v
    st_ref[4] = -kk
    st_ref[5] = kk * a

    tri3 = _cumsum_operator(c)
    row = _iota2((c, LANES), 0)
    col = _iota2((c, LANES), 1) % c
    lower = row >= col
    strict = row > col
    eye = (row == col).astype(F32)
    blk16 = (row // 16) == (col // 16)
    blk32 = (row // 32) == (col // 32)
    off16 = jnp.logical_and(blk32, jnp.logical_not(blk16))
    lane_lo = _iota2((c, LANES), 1) < n
    bd_mask = (_iota2((LANES, LANES), 0) // n) == (_iota2((LANES, LANES), 1) // n)
    units = [(b, p) for b in range(nb) for p in range(pairs)]
    prange = range(len(units))
    nt_dims = (((1,), (1,)), ((), ()))

    def chunk(ci, carry):
        rows = [pl.ds(pl.multiple_of(b * tt + ci * c, c), c) for b in range(nb)]
        a_t, r_t, b_t, k_t, b_h, k_h, vc, g_c = [], [], [], [], [], [], [], []
        for b in range(nb):
            rc = st_ref[0, rows[b], :]
            lwc = st_ref[1, rows[b], :]
            kc = st_ref[2, rows[b], :]
            ac = st_ref[4, rows[b], :]
            bc = st_ref[5, rows[b], :]
            cum = _chunk_cumsum(tri3, lwc)
            last = cum[c - 1:c, :]
            e_neg = jnp.exp(-cum)
            e_tail = jnp.exp(last - cum)
            r_tb = rc * jnp.exp(cum)
            a_tb = ac * jnp.exp(cum - lwc)
            vb = st_ref[3, rows[b], :]
            for p in range(pairs):
                sl = slice(p * LANES, (p + 1) * LANES)
                a_t.append(a_tb[:, sl])
                r_t.append(r_tb[:, sl])
                b_t.append(bc[:, sl] * e_neg[:, sl])
                k_t.append(kc[:, sl] * e_neg[:, sl])
                b_h.append(bc[:, sl] * e_tail[:, sl])
                k_h.append(kc[:, sl] * e_tail[:, sl])
                vc.append(vb[:, sl])
                g_c.append(jnp.exp(last[:, sl]))

        a_ab, a_ak, q_b, q_k = [], [], [], []
        for p in prange:
            lhs = jnp.concatenate([a_t[p], r_t[p]], axis=0).astype(BF16)
            rhs = jnp.concatenate([_bd(b_t[p], lane_lo), _bd(k_t[p], lane_lo)], axis=0)
            aa = lax.dot_general(lhs, rhs, nt_dims, preferred_element_type=F32)
            a_ab.append(jnp.where(strict, aa[:c, :LANES], 0.0))
            a_ak.append(jnp.where(strict, aa[:c, LANES:], 0.0))
            q_b.append(jnp.where(lower, aa[c:, :LANES], 0.0))
            q_k.append(jnp.where(lower, aa[c:, LANES:], 0.0))

        ad = [jnp.where(blk16, x, 0.0) for x in a_ab]
        pw = [_pair_mm([ad[p]], ad[p], lane_lo)[0] for p in prange]
        t = [eye + ad[p] for p in prange]
        for _ in range(2):
            both = [_pair_mm([pw[p], t[p]], pw[p], lane_lo) for p in prange]
            pw = [x[0] for x in both]
            t = [t[p] + both[p][1] for p in prange]
        t = [t[p] + _pair_mm([t[p]], pw[p], lane_lo)[0] for p in prange]
        for msk in (off16, jnp.logical_not(blk32)):
            te = [_pair_mm([t[p]], jnp.where(msk, a_ab[p], 0.0), lane_lo)[0] for p in prange]
            t = [t[p] + _pair_mm([te[p]], t[p], lane_lo)[0] for p in prange]

        v_r = [_bd(vc[p], lane_lo) for p in prange]
        pv = [_mm1(a_ak[p], v_r[p]) for p in prange]
        wu = [_mm1(t[p], jnp.concatenate([_bd(a_t[p], lane_lo), _bd(pv[p], lane_lo)], axis=1))
              for p in prange]

        for p in prange:
            bi, pi = units[p]
            s0 = s_ref[p]
            ws = lax.dot_general(jnp.concatenate([wu[p][:, :LANES], r_t[p]], axis=0).astype(BF16),
                                 s0.astype(BF16), nt_dims, preferred_element_type=F32)
            u_mat = ws[:c] + wu[p][:, LANES:]
            y = ws[c:] + _mm1(jnp.concatenate([q_b[p], q_k[p]], axis=1),
                              jnp.concatenate([_bd(u_mat, lane_lo), v_r[p]], axis=0))
            y_ref[rows[bi], pi * LANES:(pi + 1) * LANES] = y
            ds = _mm1(jnp.concatenate([u_mat, vc[p]], axis=0).T,
                      jnp.concatenate([b_h[p], k_h[p]], axis=0).astype(BF16))
            s_ref[p] = s0 * g_c[p] + jnp.where(bd_mask, ds, 0.0)
        return carry

    lax.fori_loop(0, tt // c, chunk, 0, unroll=min(RWKV_UNROLL, tt // c))

    y = y_ref[...]
    mean = _seg_sum(y, seg_ones) * (1.0 / n)
    dlt = y - mean
    var = _seg_sum(dlt * dlt, seg_ones) * (1.0 / n)
    yn = dlt * lax.rsqrt(var + RWKV_LN_EPS) * lng_ref[...] + lnb_ref[...]
    bonus = _seg_sum(st_ref[0] * st_ref[2] * rk_ref[...], seg_ones) * st_ref[3]
    gate = _dot(_sigmoid(cols(o_gl, o_gl + RWKV_GATE_RANK)), g2_ref[...])
    o_ref[...] = ((yn + bonus) * gate).reshape(nb, tt, w).astype(o_ref.dtype)


def _rwkv(zr, w0, w2, a0, a2, g2, k_k, k_a, r_k, ln_g, ln_b, tt):
    b, t, nz = zr.shape
    w = w0.shape[-1]
    const = lambda *shape: _const_spec(shape, 1)
    return pl.pallas_call(
        _rwkv_kernel,
        grid=(t // tt,),
        in_specs=[pl.BlockSpec((b, tt, nz), lambda j: (0, j, 0)),
                  const(1, w), const(*w2.shape), const(1, w), const(*a2.shape), const(*g2.shape),
                  const(1, w), const(1, w), const(1, w), const(1, w), const(1, w)],
        out_specs=pl.BlockSpec((b, tt, w), lambda j: (0, j, 0)),
        out_shape=jax.ShapeDtypeStruct((b, t, w), BF16),
        scratch_shapes=[pltpu.VMEM((b * w // LANES, LANES, LANES), F32),
                        pltpu.VMEM((6, b * tt, w), F32),
                        pltpu.VMEM((b * tt, w), F32)],
        compiler_params=_params(1),
        name="rwkv",
    )(zr, w0, w2, a0, a2, g2, k_k, k_a, r_k, ln_g, ln_b)


def _post_attn_kernel(x_ref, og_ref, or_ref, wog_ref, wor_ref, gx_ref, wq_ref, k_ref, v_ref, wo_ref,
                      wgu_ref, wd_ref, out_ref, wgu_out_ref, wd_out_ref):
    wgu_out_ref[...] = wgu_ref[...].astype(BF16)
    wd_out_ref[...] = wd_ref[...].astype(BF16)
    tm, d = x_ref.shape
    hd = d // MEM_HEADS
    sub = min(POST_ATTN_SUBTILE_ROWS, tm)
    rows = [slice(s0, s0 + sub) for s0 in range(0, tm, sub)]
    nt_dims = (((1,), (1,)), ((), ()))
    x1 = [x_ref[r, :] + jnp.dot(og_ref[r, :], wog_ref[...], preferred_element_type=F32)
          + jnp.dot(or_ref[r, :], wor_ref[...], preferred_element_type=F32) for r in rows]
    q = [_dot(_rms_norm(x, gx_ref[...], NORM_EPS), wq_ref[...]).astype(BF16) for x in x1]
    heads = [[] for _ in rows]
    for h in range(MEM_HEADS):
        hs = slice(h * hd, (h + 1) * hd)
        s = [lax.dot_general(qi[:, hs], k_ref[:, hs], nt_dims, preferred_element_type=F32) * (hd ** -0.5)
             for qi in q]
        e = [jnp.exp(si - jnp.max(si, axis=-1, keepdims=True)) for si in s]
        p = [ei / jnp.sum(ei, axis=-1, keepdims=True) for ei in e]
        for i, pi in enumerate(p):
            heads[i].append(_dot(pi, v_ref[:, hs]))
    for i, r in enumerate(rows):
        out_ref[r, :] = x1[i] + _dot(jnp.concatenate(heads[i], axis=-1), wo_ref[...])


def _post_attn(x, o_gla, o_rwkv, wo_g, wo_r, gx, wq, kmem, vmem, wo, wgu, wd, tm):
    b, t, d = x.shape
    m = kmem.shape[1]
    nj = t // tm
    steps = b * nj
    bf16_rows = 16
    assert wgu.shape[0] % (steps * bf16_rows) == 0 and wd.shape[0] % (steps * bf16_rows) == 0
    const = lambda *shape: _const_spec(shape, 2)
    tile = lambda width: pl.BlockSpec((None, tm, width), lambda i, j: (i, j, 0))
    slab = lambda w: pl.BlockSpec((w.shape[0] // steps, w.shape[1]), lambda i, j: (i * nj + j, 0))
    return pl.pallas_call(
        _post_attn_kernel,
        grid=(b, nj),
        in_specs=[tile(d), tile(o_gla.shape[-1]), tile(o_rwkv.shape[-1]),
                  const(*wo_g.shape), const(*wo_r.shape), const(1, d), const(*wq.shape),
                  pl.BlockSpec((None, m, d), lambda i, j: (i, 0, 0)),
                  pl.BlockSpec((None, m, d), lambda i, j: (i, 0, 0)),
                  const(*wo.shape), slab(wgu), slab(wd)],
        out_specs=(tile(d), slab(wgu), slab(wd)),
        out_shape=(jax.ShapeDtypeStruct((b, t, d), F32), jax.ShapeDtypeStruct(wgu.shape, BF16),
                   jax.ShapeDtypeStruct(wd.shape, BF16)),
        compiler_params=_params(2),
        name="post_attn",
    )(x, o_gla, o_rwkv, wo_g, wo_r, gx, wq, kmem, vmem, wo, wgu, wd)


def _ffn_kernel(x_ref, g_ref, wgu_ref, wd_ref, gf_ref, out_ref, *, ff_chunk):
    x = x_ref[...]
    d_ff = wd_ref.shape[0]
    h = _rms_norm(x, g_ref[...], NORM_EPS).astype(BF16)
    acc = x
    for c0 in range(0, d_ff, ff_chunk):
        gate = jnp.dot(h, wgu_ref[:, c0:c0 + ff_chunk], preferred_element_type=F32)
        up = jnp.dot(h, wgu_ref[:, d_ff + c0:d_ff + c0 + ff_chunk], preferred_element_type=F32)
        act = (gate * _sigmoid(gate) * up).astype(BF16)
        acc = acc + jnp.dot(act, wd_ref[c0:c0 + ff_chunk, :], preferred_element_type=F32)
    out_ref[...] = _rms_norm(acc, gf_ref[...], NORM_EPS)


def _ffn(x, g, wgu, wd, gf, tm, ff_chunk):
    b, t, d = x.shape
    const = lambda *shape: _const_spec(shape, 2)
    tile = pl.BlockSpec((None, tm, d), lambda i, j: (i, j, 0))
    return pl.pallas_call(
        functools.partial(_ffn_kernel, ff_chunk=ff_chunk),
        grid=(b, t // tm),
        in_specs=[tile, const(1, d), const(*wgu.shape), const(*wd.shape), const(1, d)],
        out_specs=tile,
        out_shape=jax.ShapeDtypeStruct((b, t, d), F32),
        compiler_params=_params(2),
        name="ffn",
    )(x, g, wgu, wd, gf)


def _layer(x, mem, norm_mix_g, w_in, gla_wa2, gla_ba, gla_norm_g, rwkv_mu, rwkv_w0, rwkv_w2, rwkv_a0,
           rwkv_a2, rwkv_g2, rwkv_k_k, rwkv_k_a, rwkv_r_k, rwkv_ln_g, rwkv_ln_b, w_out, norm_mem_x_g,
           norm_mem_g, wq_mem, wkv_mem, wo_mem, norm_ffn_g, w_gate_up, w_down, final_g):
    b, t, d = x.shape
    m = mem.shape[1]
    row = lambda p: p.reshape(1, -1).astype(F32)
    vw = GLA_HEADS * GLA_DV
    wa2p = jnp.pad(gla_wa2, ((0, LANES - GLA_GATE_RANK), (0, 0))).astype(BF16)

    kmem, vmem = _mem_kv(mem.reshape(b * m, d), row(norm_mem_g), wkv_mem)
    zg, zr = _in_proj(x, row(norm_mix_g), w_in.T, wa2p, row(gla_ba), row(rwkv_mu), tm=min(ROWS_IN_PROJ, t))
    o_gla = _gla(zg, row(gla_norm_g), tt=min(ROWS_GLA, t))
    o_rwkv = _rwkv(zr, row(rwkv_w0), rwkv_w2.astype(BF16), row(rwkv_a0), rwkv_a2.astype(BF16),
                   rwkv_g2.astype(BF16), row(rwkv_k_k), row(rwkv_k_a), row(rwkv_r_k), row(rwkv_ln_g),
                   row(rwkv_ln_b), tt=min(ROWS_RWKV, t))
    w_out_b = w_out.astype(BF16)
    x2, wgu_b, wd_b = _post_attn(x, o_gla, o_rwkv, w_out_b[:vw], w_out_b[vw:], row(norm_mem_x_g),
                                 wq_mem.astype(BF16), kmem.reshape(b, m, d), vmem.reshape(b, m, d),
                                 wo_mem.astype(BF16), w_gate_up, w_down, tm=min(ROWS_POST_ATTN, t))
    return _ffn(x2, row(norm_ffn_g), wgu_b, wd_b, row(final_g), tm=min(ROWS_FFN, t), ff_chunk=FF_CHUNK)


def kernel(x, mem, norm_mix_g, w_in, gla_wa2, gla_ba, gla_norm_g, rwkv_mu, rwkv_w0, rwkv_w2, rwkv_a0, rwkv_a2, rwkv_g2, rwkv_k_k, rwkv_k_a, rwkv_r_k, rwkv_ln_g, rwkv_ln_b, w_out, norm_mem_x_g, norm_mem_g, wq_mem, wkv_mem, wo_mem, norm_ffn_g, w_gate_up, w_down, norm_final_g):
    assert norm_mix_g.shape[0] == 1, "single-layer block"
    return _layer(x, mem, norm_mix_g[0], w_in[0], gla_wa2[0], gla_ba[0], gla_norm_g[0], rwkv_mu[0],
                  rwkv_w0[0], rwkv_w2[0], rwkv_a0[0], rwkv_a2[0], rwkv_g2[0], rwkv_k_k[0], rwkv_k_a[0],
                  rwkv_r_k[0], rwkv_ln_g[0], rwkv_ln_b[0], w_out[0], norm_mem_x_g[0], norm_mem_g[0],
                  wq_mem[0], wkv_mem[0], wo_mem[0], norm_ffn_g[0], w_gate_up[0], w_down[0], norm_final_g)
```

```python
import functools
import math

import jax
import jax.numpy as jnp
from jax import lax
from jax.experimental import pallas as pl
from jax.experimental.pallas import tpu as pltpu

F32 = jnp.float32
BF16 = jnp.bfloat16

MEM_HEADS = 4
GLA_HEADS = 4
GLA_DK = 64
GLA_DV = 128
GLA_GATE_RANK = 16
GLA_GATE_NORMALIZER = 16.0
GLA_NORM_EPS = 1e-5
RWKV_HEAD = 64
RWKV_DECAY_RANK = 64
RWKV_AAA_RANK = 64
RWKV_GATE_RANK = 128
RWKV_LN_EPS = 64e-5
NORM_EPS = 1e-6

CHUNK = 64
GLA_CHUNKS_PER_ITER = 2
GLA_UNROLL = 4
RWKV_UNROLL = 8
SUBTILE_ROWS = 256
POST_ATTN_SUBTILE_ROWS = 512
ROWS_IN_PROJ = 512
ROWS_GLA = 1024
ROWS_RWKV = 512
ROWS_POST_ATTN = 1024
ROWS_FFN = 1024
FF_CHUNK = 256
LANES = 128
VMEM_LIMIT = 56 * 1024 * 1024


def _params(n_grid_dims):
    return pltpu.CompilerParams(dimension_semantics=("arbitrary",) * n_grid_dims,
                                vmem_limit_bytes=VMEM_LIMIT)


def _const_spec(shape, grid_rank):
    zeros = (0,) * len(shape)
    index_map = (lambda j: zeros) if grid_rank == 1 else (lambda i, j: zeros)
    return pl.BlockSpec(shape, index_map, pipeline_mode=pl.Buffered(1))


def _dot(a, b):
    return jnp.dot(a.astype(BF16), b.astype(BF16), preferred_element_type=F32)


def _rms_norm(x, g, eps):
    return x * lax.rsqrt(jnp.mean(x * x, axis=-1, keepdims=True) + eps) * g


def _sigmoid(x):
    return 1.0 / (1.0 + jnp.exp(-x))


def _log_sigmoid(x):
    return jnp.minimum(x, 0.0) - jnp.log1p(jnp.exp(-jnp.abs(x)))


def _iota2(shape, dim):
    return lax.broadcasted_iota(jnp.int32, shape, dim)


def _cumsum_operator(c):
    return (_iota2((c, 3 * c), 0) >= _iota2((c, 3 * c), 1) % c).astype(BF16)


def _chunk_cumsum(tri3, x):
    p1 = x.astype(BF16)
    rem = x - p1.astype(F32)
    p2 = rem.astype(BF16)
    p3 = (rem - p2.astype(F32)).astype(BF16)
    return jnp.dot(tri3, jnp.concatenate([p1, p2, p3], axis=0), preferred_element_type=F32)


def _mem_kv_kernel(mem_ref, g_ref, w_ref, k_ref, v_ref):
    d = k_ref.shape[-1]
    m = _rms_norm(mem_ref[...], g_ref[...], NORM_EPS)
    kv = _dot(m, w_ref[...])
    k_ref[...] = kv[:, :d].astype(BF16)
    v_ref[...] = kv[:, d:].astype(BF16)


def _mem_kv(mem2d, g, wkv):
    n, d = mem2d.shape
    return pl.pallas_call(
        _mem_kv_kernel,
        out_shape=(jax.ShapeDtypeStruct((n, d), BF16), jax.ShapeDtypeStruct((n, d), BF16)),
        compiler_params=pltpu.CompilerParams(vmem_limit_bytes=VMEM_LIMIT),
        name="mem_kv",
    )(mem2d, g, wkv)


def _in_proj_kernel(x_ref, g_ref, w_ref, wa2_ref, ba_ref, mu_ref, zg_ref, zr_ref, prev_ref, wg_ref, wr_ref):
    tm = x_ref.shape[0]
    n_main = zg_ref.shape[-1] - wa2_ref.shape[-1]
    n_gla = n_main + GLA_GATE_RANK

    @pl.when(jnp.logical_and(pl.program_id(0) == 0, pl.program_id(1) == 0))
    def _():
        for dst, c_first in ((wg_ref, 0), (wr_ref, n_gla)):
            for c0 in range(0, dst.shape[1], LANES):
                slab = w_ref[c_first + c0:c_first + c0 + LANES, :]
                dst[:, c0:c0 + LANES] = slab.T.astype(BF16)

    @pl.when(pl.program_id(1) == 0)
    def _():
        prev_ref[...] = jnp.zeros_like(prev_ref)

    sub = min(SUBTILE_ROWS, tm)
    prev = prev_ref[0:1, :]
    for s in range(tm // sub):
        rows = slice(s * sub, (s + 1) * sub)
        h = _rms_norm(x_ref[rows, :], g_ref[...], NORM_EPS).astype(BF16)

        zr = jnp.dot(h, wr_ref[...], preferred_element_type=F32)
        rolled = pltpu.roll(zr, shift=1, axis=0)
        first = _iota2(zr.shape, 0) == 0
        shifted = jnp.where(first, jnp.broadcast_to(prev, zr.shape), rolled)
        zr_ref[rows, :] = zr + (shifted - zr) * mu_ref[...]
        prev = zr[sub - 1:sub, :]

        a_lo = jnp.dot(h, wg_ref[:, n_main:], preferred_element_type=F32)
        pre = _dot(a_lo, wa2_ref[...]) + ba_ref[...]
        zg_ref[rows, n_main:] = _log_sigmoid(pre) * (1.0 / GLA_GATE_NORMALIZER)
        zg_ref[rows, :n_main] = jnp.dot(h, wg_ref[:, :n_main], preferred_element_type=F32)
    prev_ref[0:1, :] = prev


def _in_proj(x, g, w_in_t, wa2p, ba, mu, tm):
    b, t, d = x.shape
    nr = mu.shape[1]
    n_main = w_in_t.shape[0] - nr - GLA_GATE_RANK
    ng = n_main + wa2p.shape[1]
    const = lambda *shape: _const_spec(shape, 2)
    return pl.pallas_call(
        _in_proj_kernel,
        grid=(b, t // tm),
        in_specs=[
            pl.BlockSpec((None, tm, d), lambda i, j: (i, j, 0)),
            const(1, d), const(*w_in_t.shape), const(*wa2p.shape),
            const(1, wa2p.shape[1]), const(1, nr),
        ],
        out_specs=(pl.BlockSpec((None, tm, ng), lambda i, j: (i, j, 0)),
                   pl.BlockSpec((None, tm, nr), lambda i, j: (i, j, 0))),
        out_shape=(jax.ShapeDtypeStruct((b, t, ng), F32), jax.ShapeDtypeStruct((b, t, nr), F32)),
        scratch_shapes=[pltpu.VMEM((8, nr), F32),
                        pltpu.VMEM((d, n_main + LANES), BF16),
                        pltpu.VMEM((d, nr), BF16)],
        compiler_params=_params(2),
        name="in_proj",
    )(x, g, w_in_t, wa2p, ba, mu)


def _gla_kernel(z_ref, ng_ref, o_ref, s_ref):
    nb, tt, _ = z_ref.shape
    kw = GLA_HEADS * GLA_DK
    vw = GLA_HEADS * GLA_DV
    c = CHUNK
    pairs = GLA_HEADS // 2
    pk = 2 * GLA_DK
    pv = 2 * GLA_DV
    cpi = GLA_CHUNKS_PER_ITER

    @pl.when(pl.program_id(0) == 0)
    def _():
        s_ref[...] = jnp.zeros_like(s_ref)

    tri3 = _cumsum_operator(c)
    causal = _iota2((c, pk), 0) >= _iota2((c, pk), 1) % c
    lane_lo = _iota2((c, pk), 1) < GLA_DK
    v_lo = _iota2((c, pv), 1) < GLA_DV
    s_mask = (_iota2((pv, pk), 0) // GLA_DV) == (_iota2((pv, pk), 1) // GLA_DK)
    norm_g = ng_ref[...]
    nt_dims = (((1,), (1,)), ((), ()))
    tn_dims = (((0,), (0,)), ((), ()))

    def step(it, carry):
        units = [(b, j, p) for b in range(nb) for j in range(cpi) for p in range(pairs)]
        rows = {(b, j): pl.ds(pl.multiple_of((it * cpi + j) * c, c), c) for b in range(nb) for j in range(cpi)}
        q_dec, k_inv, k_tail, dec, vv = {}, {}, {}, {}, {}
        for (b, j), rs in rows.items():
            q = z_ref[b, rs, 0:kw] * (GLA_DK ** -0.5)
            k = z_ref[b, rs, kw:2 * kw]
            log_a = z_ref[b, rs, 2 * kw + 2 * vw:3 * kw + 2 * vw]
            bcum = _chunk_cumsum(tri3, log_a)
            blast = bcum[c - 1:c, :]
            qd = q * jnp.exp(bcum)
            ki = k * jnp.exp(-bcum)
            kt = k * jnp.exp(blast - bcum)
            dc = jnp.exp(blast)
            for p in range(pairs):
                ks = slice(p * pk, (p + 1) * pk)
                q_dec[b, j, p] = qd[:, ks].astype(BF16)
                k_inv[b, j, p] = ki[:, ks]
                k_tail[b, j, p] = kt[:, ks].astype(BF16)
                dec[b, j, p] = dc[:, ks]
                vv[b, j, p] = z_ref[b, rs, 2 * kw + p * pv:2 * kw + (p + 1) * pv].astype(BF16)

        o_intra, d_state = {}, {}
        for u in units:
            scores = jnp.where(causal, lax.dot_general(q_dec[u], _bd(k_inv[u], lane_lo), nt_dims,
                                                       preferred_element_type=F32), 0.0)
            zero = jnp.zeros_like(vv[u])
            v_bd = jnp.concatenate([jnp.where(v_lo, vv[u], zero), jnp.where(v_lo, zero, vv[u])], axis=0)
            o_intra[u] = _mm1(scores, v_bd)
            d_state[u] = jnp.where(s_mask, lax.dot_general(vv[u], k_tail[u], tn_dims,
                                                           preferred_element_type=F32), 0.0)

        for b in range(nb):
            s_cur = [s_ref[b * pairs + p] for p in range(pairs)]
            for j in range(cpi):
                outs = []
                for p in range(pairs):
                    u = (b, j, p)
                    o = o_intra[u] + lax.dot_general(q_dec[u], s_cur[p].astype(BF16), nt_dims,
                                                     preferred_element_type=F32)
                    s_cur[p] = s_cur[p] * dec[u] + d_state[u]
                    for h in range(2):
                        oh = o[:, h * GLA_DV:(h + 1) * GLA_DV]
                        oh = oh * lax.rsqrt(jnp.mean(oh * oh, axis=-1, keepdims=True) + GLA_NORM_EPS) * norm_g
                        g0 = 2 * kw + vw + (2 * p + h) * GLA_DV
                        gh = z_ref[b, rows[b, j], g0:g0 + GLA_DV]
                        outs.append(oh * (gh * _sigmoid(gh)))
                o_ref[b, rows[b, j], :] = jnp.concatenate(outs, axis=-1).astype(o_ref.dtype)
            for p in range(pairs):
                s_ref[b * pairs + p] = s_cur[p]
        return carry

    lax.fori_loop(0, tt // (c * cpi), step, 0, unroll=min(GLA_UNROLL, tt // (c * cpi)))


def _gla(zg, norm_g, tt):
    b, t, n = zg.shape
    vw = GLA_HEADS * GLA_DV
    return pl.pallas_call(
        _gla_kernel,
        grid=(t // tt,),
        in_specs=[pl.BlockSpec((b, tt, n), lambda j: (0, j, 0)),
                  _const_spec((1, GLA_DV), 1)],
        out_specs=pl.BlockSpec((b, tt, vw), lambda j: (0, j, 0)),
        out_shape=jax.ShapeDtypeStruct((b, t, vw), BF16),
        scratch_shapes=[pltpu.VMEM((b * GLA_HEADS // 2, 2 * GLA_DV, 2 * GLA_DK), F32)],
        compiler_params=_params(1),
        name="gla",
    )(zg, norm_g)


def _seg_sum(x, seg_ones):
    xb = x.astype(BF16)
    blk = seg_ones.shape[0]
    parts = []
    for s in range(x.shape[1] // blk):
        sl = slice(s * blk, (s + 1) * blk)
        parts.append(jnp.dot(xb[:, sl], seg_ones, preferred_element_type=F32))
    return jnp.concatenate(parts, axis=-1)


def _block_diag(z, lane_lo):
    zero = jnp.zeros_like(z)
    return jnp.concatenate([jnp.where(lane_lo, z, zero), jnp.where(lane_lo, zero, z)], axis=0)


def _bd(z, lane_lo):
    return _block_diag(z.astype(BF16), lane_lo)


def _mm1(x, r):
    return jnp.dot(x.astype(BF16), r, preferred_element_type=F32)


def _rwkv_kernel(z_ref, w0_ref, w2_ref, a0_ref, a2_ref, g2_ref, kk_ref, ka_ref, rk_ref, lng_ref, lnb_ref,
                 o_ref, s_ref, st_ref, y_ref):
    nb, tt, _ = z_ref.shape
    w = w0_ref.shape[-1]
    n = RWKV_HEAD
    pairs = w // LANES
    c = CHUNK
    o_wl = 3 * w
    o_al = o_wl + RWKV_DECAY_RANK
    o_gl = o_al + RWKV_AAA_RANK

    @pl.when(pl.program_id(0) == 0)
    def _():
        s_ref[...] = jnp.zeros_like(s_ref)

    seg = 2 * LANES
    seg_ones = (_iota2((seg, seg), 0) // n == _iota2((seg, seg), 1) // n).astype(BF16)

    cols = lambda lo, hi: z_ref[:, :, lo:hi].reshape(nb * tt, hi - lo)
    r = cols(0, w)
    k = cols(w, 2 * w)
    v = cols(2 * w, 3 * w)
    u = w0_ref[...] + _dot(jnp.tanh(cols(o_wl, o_al)), w2_ref[...])
    lw = -math.exp(-0.5) * _sigmoid(u)
    a = _sigmoid(a0_ref[...] + _dot(cols(o_al, o_gl), a2_ref[...]))
    kk = k * kk_ref[...]
    kk = kk * lax.rsqrt(jnp.maximum(_seg_sum(kk * kk, seg_ones), 1e-24))
    k2 = k * (1.0 + (a - 1.0) * ka_ref[...])
    st_ref[0] = r
    st_ref[1] = lw
    st_ref[2] = k2
    st_ref[3] = v
    st_ref[4] = -kk
    st_ref[5] = kk * a

    tri3 = _cumsum_operator(c)
    row = _iota2((c, LANES), 0)
    col = _iota2((c, LANES), 1) % c
    lower = row >= col
    strict = row > col
    eye = (row == col).astype(F32)
    blk16 = (row // 16) == (col // 16)
    blk32 = (row // 32) == (col // 32)
    off16 = jnp.logical_and(blk32, jnp.logical_not(blk16))
    lane_lo = _iota2((c, LANES), 1) < n
    bd_mask = (_iota2((LANES, LANES), 0) // n) == (_iota2((LANES, LANES), 1) // n)
    units = [(b, p) for b in range(nb) for p in range(pairs)]
    prange = range(len(units))
    nt_dims = (((1,), (1,)), ((), ()))

    def chunk(ci, carry):
        rows = [pl.ds(pl.multiple_of(b * tt + ci * c, c), c) for b in range(nb)]
        a_t, r_t, b_t, k_t, b_h, k_h, vc, g_c = [], [], [], [], [], [], [], []
        for b in range(nb):
            rc = st_ref[0, rows[b], :]
            lwc = st_ref[1, rows[b], :]
            kc = st_ref[2, rows[b], :]
            ac = st_ref[4, rows[b], :]
            bc = st_ref[5, rows[b], :]
            cum = _chunk_cumsum(tri3, lwc)
            last = cum[c - 1:c, :]
            e_neg = jnp.exp(-cum)
            e_tail = jnp.exp(last - cum)
            r_tb = rc * jnp.exp(cum)
            a_tb = ac * jnp.exp(cum - lwc)
            vb = st_ref[3, rows[b], :]
            for p in range(pairs):
                sl = slice(p * LANES, (p + 1) * LANES)
                a_t.append(a_tb[:, sl])
                r_t.append(r_tb[:, sl])
                b_t.append(bc[:, sl] * e_neg[:, sl])
                k_t.append(kc[:, sl] * e_neg[:, sl])
                b_h.append(bc[:, sl] * e_tail[:, sl])
                k_h.append(kc[:, sl] * e_tail[:, sl])
                vc.append(vb[:, sl])
                g_c.append(jnp.exp(last[:, sl]))

        a_ab, a_ak, q_b, q_k = [], [], [], []
        for p in prange:
            lhs = jnp.concatenate([a_t[p], r_t[p]], axis=0).astype(BF16)
            rhs = jnp.concatenate([_bd(b_t[p], lane_lo), _bd(k_t[p], lane_lo)], axis=0)
            aa = lax.dot_general(lhs, rhs, nt_dims, preferred_element_type=F32)
            a_ab.append(jnp.where(strict, aa[:c, :LANES], 0.0))
            a_ak.append(jnp.where(strict, aa[:c, LANES:], 0.0))
            q_b.append(jnp.where(lower, aa[c:, :LANES], 0.0))
            q_k.append(jnp.where(lower, aa[c:, LANES:], 0.0))

        ad = [jnp.where(blk16, x, 0.0) for x in a_ab]
        pw = [_mm1(ad[p], _bd(ad[p], lane_lo)) for p in prange]
        t = [eye + ad[p] for p in prange]
        for _ in range(2):
            both = [_mm1(jnp.concatenate([pw[p], t[p]], axis=0), _bd(pw[p], lane_lo)) for p in prange]
            pw = [x[:c] for x in both]
            t = [t[p] + both[p][c:] for p in prange]
        t = [t[p] + _mm1(t[p], _bd(pw[p], lane_lo)) for p in prange]
        for msk in (off16, jnp.logical_not(blk32)):
            te = [_mm1(t[p], _bd(jnp.where(msk, a_ab[p], 0.0), lane_lo)) for p in prange]
            t = [t[p] + _mm1(te[p], _bd(t[p], lane_lo)) for p in prange]

        v_r = [_bd(vc[p], lane_lo) for p in prange]
        z_mat = [_mm1(t[p], jnp.concatenate([_bd(a_t[p], lane_lo), _bd(a_ak[p], lane_lo)], axis=1))
                 for p in prange]
        g_col = [jnp.broadcast_to(g_c[p], (LANES, LANES)).T for p in prange]
        bk_t = [jnp.concatenate([b_h[p], k_h[p]], axis=0).T.astype(BF16) for p in prange]

        for p in prange:
            bi, pi = units[p]
            m0 = s_ref[p]
            lhs = jnp.concatenate([z_mat[p], jnp.concatenate([r_t[p], jnp.zeros_like(r_t[p])], axis=1)],
                                  axis=0).astype(BF16)
            ws = jnp.dot(lhs, jnp.concatenate([m0.astype(BF16), v_r[p]], axis=0),
                         preferred_element_type=F32)
            u_mat = ws[:c]
            y = ws[c:] + _mm1(jnp.concatenate([q_b[p], q_k[p]], axis=1),
                              jnp.concatenate([_bd(u_mat, lane_lo), v_r[p]], axis=0))
            y_ref[rows[bi], pi * LANES:(pi + 1) * LANES] = y
            ds = jnp.dot(bk_t[p], jnp.concatenate([u_mat, vc[p]], axis=0).astype(BF16),
                         preferred_element_type=F32)
            s_ref[p] = m0 * g_col[p] + jnp.where(bd_mask, ds, 0.0)
        return carry

    lax.fori_loop(0, tt // c, chunk, 0, unroll=min(RWKV_UNROLL, tt // c))

    y = y_ref[...]
    mean = _seg_sum(y, seg_ones) * (1.0 / n)
    dlt = y - mean
    var = _seg_sum(dlt * dlt, seg_ones) * (1.0 / n)
    yn = dlt * lax.rsqrt(var + RWKV_LN_EPS) * lng_ref[...] + lnb_ref[...]
    bonus = _seg_sum(st_ref[0] * st_ref[2] * rk_ref[...], seg_ones) * st_ref[3]
    gate = _dot(_sigmoid(cols(o_gl, o_gl + RWKV_GATE_RANK)), g2_ref[...])
    o_ref[...] = ((yn + bonus) * gate).reshape(nb, tt, w).astype(o_ref.dtype)


def _rwkv(zr, w0, w2, a0, a2, g2, k_k, k_a, r_k, ln_g, ln_b, tt):
    b, t, nz = zr.shape
    w = w0.shape[-1]
    const = lambda *shape: _const_spec(shape, 1)
    return pl.pallas_call(
        _rwkv_kernel,
        grid=(t // tt,),
        in_specs=[pl.BlockSpec((b, tt, nz), lambda j: (0, j, 0)),
                  const(1, w), const(*w2.shape), const(1, w), const(*a2.shape), const(*g2.shape),
                  const(1, w), const(1, w), const(1, w), const(1, w), const(1, w)],
        out_specs=pl.BlockSpec((b, tt, w), lambda j: (0, j, 0)),
        out_shape=jax.ShapeDtypeStruct((b, t, w), BF16),
        scratch_shapes=[pltpu.VMEM((b * w // LANES, LANES, LANES), F32),
                        pltpu.VMEM((6, b * tt, w), F32),
                        pltpu.VMEM((b * tt, w), F32)],
        compiler_params=_params(1),
        name="rwkv",
    )(zr, w0, w2, a0, a2, g2, k_k, k_a, r_k, ln_g, ln_b)


def _post_attn_kernel(x_ref, og_ref, or_ref, wog_ref, wor_ref, gx_ref, wq_ref, k_ref, v_ref, wo_ref,
                      wgu_ref, wd_ref, out_ref, wgu_out_ref, wd_out_ref):
    wgu_out_ref[...] = wgu_ref[...].astype(BF16)
    wd_out_ref[...] = wd_ref[...].astype(BF16)
    tm, d = x_ref.shape
    hd = d // MEM_HEADS
    sub = min(POST_ATTN_SUBTILE_ROWS, tm)
    rows = [slice(s0, s0 + sub) for s0 in range(0, tm, sub)]
    nt_dims = (((1,), (1,)), ((), ()))
    x1 = [x_ref[r, :] + jnp.dot(og_ref[r, :], wog_ref[...], preferred_element_type=F32)
          + jnp.dot(or_ref[r, :], wor_ref[...], preferred_element_type=F32) for r in rows]
    q = [_dot(_rms_norm(x, gx_ref[...], NORM_EPS), wq_ref[...]).astype(BF16) for x in x1]
    heads = [[] for _ in rows]
    for h in range(MEM_HEADS):
        hs = slice(h * hd, (h + 1) * hd)
        s = [lax.dot_general(qi[:, hs], k_ref[:, hs], nt_dims, preferred_element_type=F32) * (hd ** -0.5)
             for qi in q]
        e = [jnp.exp(si - jnp.max(si, axis=-1, keepdims=True)) for si in s]
        p = [ei / jnp.sum(ei, axis=-1, keepdims=True) for ei in e]
        for i, pi in enumerate(p):
            heads[i].append(_dot(pi, v_ref[:, hs]))
    for i, r in enumerate(rows):
        out_ref[r, :] = x1[i] + _dot(jnp.concatenate(heads[i], axis=-1), wo_ref[...])


def _post_attn(x, o_gla, o_rwkv, wo_g, wo_r, gx, wq, kmem, vmem, wo, wgu, wd, tm):
    b, t, d = x.shape
    m = kmem.shape[1]
    nj = t // tm
    steps = b * nj
    bf16_rows = 16
    assert wgu.shape[0] % (steps * bf16_rows) == 0 and wd.shape[0] % (steps * bf16_rows) == 0
    const = lambda *shape: _const_spec(shape, 2)
    tile = lambda width: pl.BlockSpec((None, tm, width), lambda i, j: (i, j, 0))
    slab = lambda w: pl.BlockSpec((w.shape[0] // steps, w.shape[1]), lambda i, j: (i * nj + j, 0))
    return pl.pallas_call(
        _post_attn_kernel,
        grid=(b, nj),
        in_specs=[tile(d), tile(o_gla.shape[-1]), tile(o_rwkv.shape[-1]),
                  const(*wo_g.shape), const(*wo_r.shape), const(1, d), const(*wq.shape),
                  pl.BlockSpec((None, m, d), lambda i, j: (i, 0, 0)),
                  pl.BlockSpec((None, m, d), lambda i, j: (i, 0, 0)),
                  const(*wo.shape), slab(wgu), slab(wd)],
        out_specs=(tile(d), slab(wgu), slab(wd)),
        out_shape=(jax.ShapeDtypeStruct((b, t, d), F32), jax.ShapeDtypeStruct(wgu.shape, BF16),
                   jax.ShapeDtypeStruct(wd.shape, BF16)),
        compiler_params=_params(2),
        name="post_attn",
    )(x, o_gla, o_rwkv, wo_g, wo_r, gx, wq, kmem, vmem, wo, wgu, wd)


def _ffn_kernel(x_ref, g_ref, wgu_ref, wd_ref, gf_ref, out_ref, *, ff_chunk):
    x = x_ref[...]
    d_ff = wd_ref.shape[0]
    h = _rms_norm(x, g_ref[...], NORM_EPS).astype(BF16)
    acc = x
    for c0 in range(0, d_ff, ff_chunk):
        gate = jnp.dot(h, wgu_ref[:, c0:c0 + ff_chunk], preferred_element_type=F32)
        up = jnp.dot(h, wgu_ref[:, d_ff + c0:d_ff + c0 + ff_chunk], preferred_element_type=F32)
        act = (gate * _sigmoid(gate) * up).astype(BF16)
        acc = acc + jnp.dot(act, wd_ref[c0:c0 + ff_chunk, :], preferred_element_type=F32)
    out_ref[...] = _rms_norm(acc, gf_ref[...], NORM_EPS)


def _ffn(x, g, wgu, wd, gf, tm, ff_chunk):
    b, t, d = x.shape
    const = lambda *shape: _const_spec(shape, 2)
    tile = pl.BlockSpec((None, tm, d), lambda i, j: (i, j, 0))
    return pl.pallas_call(
        functools.partial(_ffn_kernel, ff_chunk=ff_chunk),
        grid=(b, t // tm),
        in_specs=[tile, const(1, d), const(*wgu.shape), const(*wd.shape), const(1, d)],
        out_specs=tile,
        out_shape=jax.ShapeDtypeStruct((b, t, d), F32),
        compiler_params=_params(2),
        name="ffn",
    )(x, g, wgu, wd, gf)


def _layer(x, mem, norm_mix_g, w_in, gla_wa2, gla_ba, gla_norm_g, rwkv_mu, rwkv_w0, rwkv_w2, rwkv_a0,
           rwkv_a2, rwkv_g2, rwkv_k_k, rwkv_k_a, rwkv_r_k, rwkv_ln_g, rwkv_ln_b, w_out, norm_mem_x_g,
           norm_mem_g, wq_mem, wkv_mem, wo_mem, norm_ffn_g, w_gate_up, w_down, final_g):
    b, t, d = x.shape
    m = mem.shape[1]
    row = lambda p: p.reshape(1, -1).astype(F32)
    vw = GLA_HEADS * GLA_DV
    wa2p = jnp.pad(gla_wa2, ((0, LANES - GLA_GATE_RANK), (0, 0))).astype(BF16)

    kmem, vmem = _mem_kv(mem.reshape(b * m, d), row(norm_mem_g), wkv_mem)
    zg, zr = _in_proj(x, row(norm_mix_g), w_in.T, wa2p, row(gla_ba), row(rwkv_mu), tm=min(ROWS_IN_PROJ, t))
    o_gla = _gla(zg, row(gla_norm_g), tt=min(ROWS_GLA, t))
    o_rwkv = _rwkv(zr, row(rwkv_w0), rwkv_w2.astype(BF16), row(rwkv_a0), rwkv_a2.astype(BF16),
                   rwkv_g2.astype(BF16), row(rwkv_k_k), row(rwkv_k_a), row(rwkv_r_k), row(rwkv_ln_g),
                   row(rwkv_ln_b), tt=min(ROWS_RWKV, t))
    w_out_b = w_out.astype(BF16)
    x2, wgu_b, wd_b = _post_attn(x, o_gla, o_rwkv, w_out_b[:vw], w_out_b[vw:], row(norm_mem_x_g),
                                 wq_mem.astype(BF16), kmem.reshape(b, m, d), vmem.reshape(b, m, d),
                                 wo_mem.astype(BF16), w_gate_up, w_down, tm=min(ROWS_POST_ATTN, t))
    return _ffn(x2, row(norm_ffn_g), wgu_b, wd_b, row(final_g), tm=min(ROWS_FFN, t), ff_chunk=FF_CHUNK)


def kernel(x, mem, norm_mix_g, w_in, gla_wa2, gla_ba, gla_norm_g, rwkv_mu, rwkv_w0, rwkv_w2, rwkv_a0, rwkv_a2, rwkv_g2, rwkv_k_k, rwkv_k_a, rwkv_r_k, rwkv_ln_g, rwkv_ln_b, w_out, norm_mem_x_g, norm_mem_g, wq_mem, wkv_mem, wo_mem, norm_ffn_g, w_gate_up, w_down, norm_final_g):
    assert norm_mix_g.shape[0] == 1, "single-layer block"
    return _layer(x, mem, norm_mix_g[0], w_in[0], gla_wa2[0], gla_ba[0], gla_norm_g[0], rwkv_mu[0],
                  rwkv_w0[0], rwkv_w2[0], rwkv_a0[0], rwkv_a2[0], rwkv_g2[0], rwkv_k_k[0], rwkv_k_a[0],
                  rwkv_r_k[0], rwkv_ln_g[0], rwkv_ln_b[0], w_out[0], norm_mem_x_g[0], norm_mem_g[0],
                  wq_mem[0], wkv_mem[0], wo_mem[0], norm_ffn_g[0], w_gate_up[0], w_down[0], norm_final_g)
```

```python
import functools
import math

import jax
import jax.numpy as jnp
from jax import lax
from jax.experimental import pallas as pl
from jax.experimental.pallas import tpu as pltpu

F32 = jnp.float32
BF16 = jnp.bfloat16

MEM_HEADS = 4
GLA_HEADS = 4
GLA_DK = 64
GLA_DV = 128
GLA_GATE_RANK = 16
GLA_GATE_NORMALIZER = 16.0
GLA_NORM_EPS = 1e-5
RWKV_HEAD = 64
RWKV_DECAY_RANK = 64
RWKV_AAA_RANK = 64
RWKV_GATE_RANK = 128
RWKV_LN_EPS = 64e-5
NORM_EPS = 1e-6

CHUNK = 64
GLA_CHUNKS_PER_ITER = 4
GLA_UNROLL = 2
RWKV_UNROLL = 8
SUBTILE_ROWS = 256
POST_ATTN_SUBTILE_ROWS = 512
ROWS_IN_PROJ = 512
ROWS_GLA = 1024
ROWS_RWKV = 512
ROWS_POST_ATTN = 1024
ROWS_FFN = 1024
FF_CHUNK = 256
LANES = 128
VMEM_LIMIT = 56 * 1024 * 1024


def _params(n_grid_dims):
    return pltpu.CompilerParams(dimension_semantics=("arbitrary",) * n_grid_dims,
                                vmem_limit_bytes=VMEM_LIMIT)


def _const_spec(shape, grid_rank):
    zeros = (0,) * len(shape)
    index_map = (lambda j: zeros) if grid_rank == 1 else (lambda i, j: zeros)
    return pl.BlockSpec(shape, index_map, pipeline_mode=pl.Buffered(1))


def _dot(a, b):
    return jnp.dot(a.astype(BF16), b.astype(BF16), preferred_element_type=F32)


def _rms_norm(x, g, eps):
    return x * lax.rsqrt(jnp.mean(x * x, axis=-1, keepdims=True) + eps) * g


def _sigmoid(x):
    return 1.0 / (1.0 + jnp.exp(-x))


def _log_sigmoid(x):
    return jnp.minimum(x, 0.0) - jnp.log1p(jnp.exp(-jnp.abs(x)))


def _iota2(shape, dim):
    return lax.broadcasted_iota(jnp.int32, shape, dim)


def _cumsum_operator(c):
    return (_iota2((c, 3 * c), 0) >= _iota2((c, 3 * c), 1) % c).astype(BF16)


def _chunk_cumsum(tri3, x):
    p1 = x.astype(BF16)
    rem = x - p1.astype(F32)
    p2 = rem.astype(BF16)
    p3 = (rem - p2.astype(F32)).astype(BF16)
    return jnp.dot(tri3, jnp.concatenate([p1, p2, p3], axis=0), preferred_element_type=F32)


def _mem_kv_kernel(mem_ref, g_ref, w_ref, k_ref, v_ref):
    d = k_ref.shape[-1]
    m = _rms_norm(mem_ref[...], g_ref[...], NORM_EPS)
    kv = _dot(m, w_ref[...])
    k_ref[...] = kv[:, :d].astype(BF16)
    v_ref[...] = kv[:, d:].astype(BF16)


def _mem_kv(mem2d, g, wkv):
    n, d = mem2d.shape
    return pl.pallas_call(
        _mem_kv_kernel,
        out_shape=(jax.ShapeDtypeStruct((n, d), BF16), jax.ShapeDtypeStruct((n, d), BF16)),
        compiler_params=pltpu.CompilerParams(vmem_limit_bytes=VMEM_LIMIT),
        name="mem_kv",
    )(mem2d, g, wkv)


def _in_proj_kernel(x_ref, g_ref, w_ref, wa2_ref, ba_ref, mu_ref, zg_ref, zr_ref, prev_ref, wg_ref, wr_ref):
    tm = x_ref.shape[0]
    n_main = zg_ref.shape[-1] - wa2_ref.shape[-1]
    n_gla = n_main + GLA_GATE_RANK

    @pl.when(jnp.logical_and(pl.program_id(0) == 0, pl.program_id(1) == 0))
    def _():
        for dst, c_first in ((wg_ref, 0), (wr_ref, n_gla)):
            for c0 in range(0, dst.shape[1], LANES):
                slab = w_ref[c_first + c0:c_first + c0 + LANES, :]
                dst[:, c0:c0 + LANES] = slab.T.astype(BF16)

    @pl.when(pl.program_id(1) == 0)
    def _():
        prev_ref[...] = jnp.zeros_like(prev_ref)

    sub = min(SUBTILE_ROWS, tm)
    prev = prev_ref[0:1, :]
    for s in range(tm // sub):
        rows = slice(s * sub, (s + 1) * sub)
        h = _rms_norm(x_ref[rows, :], g_ref[...], NORM_EPS).astype(BF16)

        zr = jnp.dot(h, wr_ref[...], preferred_element_type=F32)
        rolled = pltpu.roll(zr, shift=1, axis=0)
        first = _iota2(zr.shape, 0) == 0
        shifted = jnp.where(first, jnp.broadcast_to(prev, zr.shape), rolled)
        zr_ref[rows, :] = zr + (shifted - zr) * mu_ref[...]
        prev = zr[sub - 1:sub, :]

        a_lo = jnp.dot(h, wg_ref[:, n_main:], preferred_element_type=F32)
        pre = _dot(a_lo, wa2_ref[...]) + ba_ref[...]
        zg_ref[rows, n_main:] = _log_sigmoid(pre) * (1.0 / GLA_GATE_NORMALIZER)
        zg_ref[rows, :n_main] = jnp.dot(h, wg_ref[:, :n_main], preferred_element_type=F32)
    prev_ref[0:1, :] = prev


def _in_proj(x, g, w_in_t, wa2p, ba, mu, tm):
    b, t, d = x.shape
    nr = mu.shape[1]
    n_main = w_in_t.shape[0] - nr - GLA_GATE_RANK
    ng = n_main + wa2p.shape[1]
    const = lambda *shape: _const_spec(shape, 2)
    return pl.pallas_call(
        _in_proj_kernel,
        grid=(b, t // tm),
        in_specs=[
            pl.BlockSpec((None, tm, d), lambda i, j: (i, j, 0)),
            const(1, d), const(*w_in_t.shape), const(*wa2p.shape),
            const(1, wa2p.shape[1]), const(1, nr),
        ],
        out_specs=(pl.BlockSpec((None, tm, ng), lambda i, j: (i, j, 0)),
                   pl.BlockSpec((None, tm, nr), lambda i, j: (i, j, 0))),
        out_shape=(jax.ShapeDtypeStruct((b, t, ng), F32), jax.ShapeDtypeStruct((b, t, nr), F32)),
        scratch_shapes=[pltpu.VMEM((8, nr), F32),
                        pltpu.VMEM((d, n_main + LANES), BF16),
                        pltpu.VMEM((d, nr), BF16)],
        compiler_params=_params(2),
        name="in_proj",
    )(x, g, w_in_t, wa2p, ba, mu)


def _gla_kernel(z_ref, ng_ref, o_ref, s_ref):
    nb, tt, _ = z_ref.shape
    kw = GLA_HEADS * GLA_DK
    vw = GLA_HEADS * GLA_DV
    c = CHUNK
    pairs = GLA_HEADS // 2
    pk = 2 * GLA_DK
    pv = 2 * GLA_DV
    cpi = GLA_CHUNKS_PER_ITER

    @pl.when(pl.program_id(0) == 0)
    def _():
        s_ref[...] = jnp.zeros_like(s_ref)

    tri3 = _cumsum_operator(c)
    causal = _iota2((c, pk), 0) >= _iota2((c, pk), 1) % c
    lane_lo = _iota2((c, pk), 1) < GLA_DK
    v_lo = _iota2((c, pv), 1) < GLA_DV
    s_mask = (_iota2((pv, pk), 0) // GLA_DV) == (_iota2((pv, pk), 1) // GLA_DK)
    norm_g = ng_ref[...]
    nt_dims = (((1,), (1,)), ((), ()))
    tn_dims = (((0,), (0,)), ((), ()))

    def step(it, carry):
        units = [(b, j, p) for b in range(nb) for j in range(cpi) for p in range(pairs)]
        rows = {(b, j): pl.ds(pl.multiple_of((it * cpi + j) * c, c), c) for b in range(nb) for j in range(cpi)}
        q_dec, k_inv, k_tail, dec, vv = {}, {}, {}, {}, {}
        for (b, j), rs in rows.items():
            q = z_ref[b, rs, 0:kw] * (GLA_DK ** -0.5)
            k = z_ref[b, rs, kw:2 * kw]
            log_a = z_ref[b, rs, 2 * kw + 2 * vw:3 * kw + 2 * vw]
            bcum = _chunk_cumsum(tri3, log_a)
            blast = bcum[c - 1:c, :]
            qd = q * jnp.exp(bcum)
            ki = k * jnp.exp(-bcum)
            kt = k * jnp.exp(blast - bcum)
            dc = jnp.exp(blast)
            for p in range(pairs):
                ks = slice(p * pk, (p + 1) * pk)
                q_dec[b, j, p] = qd[:, ks].astype(BF16)
                k_inv[b, j, p] = ki[:, ks]
                k_tail[b, j, p] = kt[:, ks].astype(BF16)
                dec[b, j, p] = dc[:, ks]
                vv[b, j, p] = z_ref[b, rs, 2 * kw + p * pv:2 * kw + (p + 1) * pv].astype(BF16)

        o_intra, d_state = {}, {}
        for u in units:
            scores = jnp.where(causal, lax.dot_general(q_dec[u], _bd(k_inv[u], lane_lo), nt_dims,
                                                       preferred_element_type=F32), 0.0)
            zero = jnp.zeros_like(vv[u])
            v_bd = jnp.concatenate([jnp.where(v_lo, vv[u], zero), jnp.where(v_lo, zero, vv[u])], axis=0)
            o_intra[u] = _mm1(scores, v_bd)
            d_state[u] = jnp.where(s_mask, lax.dot_general(vv[u], k_tail[u], tn_dims,
                                                           preferred_element_type=F32), 0.0)

        for b in range(nb):
            s_cur = [s_ref[b * pairs + p] for p in range(pairs)]
            for j in range(cpi):
                outs = []
                for p in range(pairs):
                    u = (b, j, p)
                    o = o_intra[u] + lax.dot_general(q_dec[u], s_cur[p].astype(BF16), nt_dims,
                                                     preferred_element_type=F32)
                    s_cur[p] = s_cur[p] * dec[u] + d_state[u]
                    for h in range(2):
                        oh = o[:, h * GLA_DV:(h + 1) * GLA_DV]
                        oh = oh * lax.rsqrt(jnp.mean(oh * oh, axis=-1, keepdims=True) + GLA_NORM_EPS) * norm_g
                        g0 = 2 * kw + vw + (2 * p + h) * GLA_DV
                        gh = z_ref[b, rows[b, j], g0:g0 + GLA_DV]
                        outs.append(oh * (gh * _sigmoid(gh)))
                o_ref[b, rows[b, j], :] = jnp.concatenate(outs, axis=-1).astype(o_ref.dtype)
            for p in range(pairs):
                s_ref[b * pairs + p] = s_cur[p]
        return carry

    lax.fori_loop(0, tt // (c * cpi), step, 0, unroll=min(GLA_UNROLL, tt // (c * cpi)))


def _gla(zg, norm_g, tt):
    b, t, n = zg.shape
    vw = GLA_HEADS * GLA_DV
    return pl.pallas_call(
        _gla_kernel,
        grid=(t // tt,),
        in_specs=[pl.BlockSpec((b, tt, n), lambda j: (0, j, 0)),
                  _const_spec((1, GLA_DV), 1)],
        out_specs=pl.BlockSpec((b, tt, vw), lambda j: (0, j, 0)),
        out_shape=jax.ShapeDtypeStruct((b, t, vw), BF16),
        scratch_shapes=[pltpu.VMEM((b * GLA_HEADS // 2, 2 * GLA_DV, 2 * GLA_DK), F32)],
        compiler_params=_params(1),
        name="gla",
    )(zg, norm_g)


def _seg_sum(x, seg_ones):
    xb = x.astype(BF16)
    blk = seg_ones.shape[0]
    parts = []
    for s in range(x.shape[1] // blk):
        sl = slice(s * blk, (s + 1) * blk)
        parts.append(jnp.dot(xb[:, sl], seg_ones, preferred_element_type=F32))
    return jnp.concatenate(parts, axis=-1)


def _block_diag(z, lane_lo):
    zero = jnp.zeros_like(z)
    return jnp.concatenate([jnp.where(lane_lo, z, zero), jnp.where(lane_lo, zero, z)], axis=0)


def _bd(z, lane_lo):
    return _block_diag(z.astype(BF16), lane_lo)


def _mm1(x, r):
    return jnp.dot(x.astype(BF16), r, preferred_element_type=F32)


def _rwkv_kernel(z_ref, w0_ref, w2_ref, a0_ref, a2_ref, g2_ref, kk_ref, ka_ref, rk_ref, lng_ref, lnb_ref,
                 o_ref, s_ref, st_ref, y_ref):
    nb, tt, _ = z_ref.shape
    w = w0_ref.shape[-1]
    n = RWKV_HEAD
    pairs = w // LANES
    c = CHUNK
    o_wl = 3 * w
    o_al = o_wl + RWKV_DECAY_RANK
    o_gl = o_al + RWKV_AAA_RANK

    @pl.when(pl.program_id(0) == 0)
    def _():
        s_ref[...] = jnp.zeros_like(s_ref)

    seg = 2 * LANES
    seg_ones = (_iota2((seg, seg), 0) // n == _iota2((seg, seg), 1) // n).astype(BF16)

    cols = lambda lo, hi: z_ref[:, :, lo:hi].reshape(nb * tt, hi - lo)
    r = cols(0, w)
    k = cols(w, 2 * w)
    v = cols(2 * w, 3 * w)
    u = w0_ref[...] + _dot(jnp.tanh(cols(o_wl, o_al)), w2_ref[...])
    lw = -math.exp(-0.5) * _sigmoid(u)
    a = _sigmoid(a0_ref[...] + _dot(cols(o_al, o_gl), a2_ref[...]))
    kk = k * kk_ref[...]
    kk = kk * lax.rsqrt(jnp.maximum(_seg_sum(kk * kk, seg_ones), 1e-24))
    k2 = k * (1.0 + (a - 1.0) * ka_ref[...])
    st_ref[0] = r
    st_ref[1] = lw
    st_ref[2] = k2
    st_ref[3] = v
    st_ref[4] = -kk
    st_ref[5] = kk * a

    tri3 = _cumsum_operator(c)
    row = _iota2((c, LANES), 0)
    col = _iota2((c, LANES), 1) % c
    lower = row >= col
    strict = row > col
    eye = (row == col).astype(F32)
    blk16 = (row // 16) == (col // 16)
    blk32 = (row // 32) == (col // 32)
    off16 = jnp.logical_and(blk32, jnp.logical_not(blk16))
    lane_lo = _iota2((c, LANES), 1) < n
    bd_mask = (_iota2((LANES, LANES), 0) // n) == (_iota2((LANES, LANES), 1) // n)
    units = [(b, p) for b in range(nb) for p in range(pairs)]
    prange = range(len(units))
    nt_dims = (((1,), (1,)), ((), ()))

    def chunk(ci, carry):
        rows = [pl.ds(pl.multiple_of(b * tt + ci * c, c), c) for b in range(nb)]
        a_t, r_t, b_t, k_t, b_h, k_h, vc, g_c = [], [], [], [], [], [], [], []
        for b in range(nb):
            rc = st_ref[0, rows[b], :]
            lwc = st_ref[1, rows[b], :]
            kc = st_ref[2, rows[b], :]
            ac = st_ref[4, rows[b], :]
            bc = st_ref[5, rows[b], :]
            cum = _chunk_cumsum(tri3, lwc)
            last = cum[c - 1:c, :]
            e_neg = jnp.exp(-cum)
            e_tail = jnp.exp(last - cum)
            r_tb = rc * jnp.exp(cum)
            a_tb = ac * jnp.exp(cum - lwc)
            vb = st_ref[3, rows[b], :]
            for p in range(pairs):
                sl = slice(p * LANES, (p + 1) * LANES)
                a_t.append(a_tb[:, sl])
                r_t.append(r_tb[:, sl])
                b_t.append(bc[:, sl] * e_neg[:, sl])
                k_t.append(kc[:, sl] * e_neg[:, sl])
                b_h.append(bc[:, sl] * e_tail[:, sl])
                k_h.append(kc[:, sl] * e_tail[:, sl])
                vc.append(vb[:, sl])
                g_c.append(jnp.exp(last[:, sl]))

        a_ab, a_ak, q_b, q_k = [], [], [], []
        for p in prange:
            lhs = jnp.concatenate([a_t[p], r_t[p]], axis=0).astype(BF16)
            rhs = jnp.concatenate([_bd(b_t[p], lane_lo), _bd(k_t[p], lane_lo)], axis=0)
            aa = lax.dot_general(lhs, rhs, nt_dims, preferred_element_type=F32)
            a_ab.append(jnp.where(strict, aa[:c, :LANES], 0.0))
            a_ak.append(jnp.where(strict, aa[:c, LANES:], 0.0))
            q_b.append(jnp.where(lower, aa[c:, :LANES], 0.0))
            q_k.append(jnp.where(lower, aa[c:, LANES:], 0.0))

        ad = [jnp.where(blk16, x, 0.0) for x in a_ab]
        pw = [_mm1(ad[p], _bd(ad[p], lane_lo)) for p in prange]
        t = [eye + ad[p] for p in prange]
        for _ in range(2):
            both = [_mm1(jnp.concatenate([pw[p], t[p]], axis=0), _bd(pw[p], lane_lo)) for p in prange]
            pw = [x[:c] for x in both]
            t = [t[p] + both[p][c:] for p in prange]
        t = [t[p] + _mm1(t[p], _bd(pw[p], lane_lo)) for p in prange]
        for msk in (off16, jnp.logical_not(blk32)):
            te = [_mm1(t[p], _bd(jnp.where(msk, a_ab[p], 0.0), lane_lo)) for p in prange]
            t = [t[p] + _mm1(te[p], _bd(t[p], lane_lo)) for p in prange]

        v_r = [_bd(vc[p], lane_lo) for p in prange]
        z_mat = [_mm1(t[p], jnp.concatenate([_bd(a_t[p], lane_lo), _bd(a_ak[p], lane_lo)], axis=1))
                 for p in prange]
        g_col = [jnp.broadcast_to(g_c[p], (LANES, LANES)).T for p in prange]
        bk_t = [jnp.concatenate([b_h[p], k_h[p]], axis=0).T.astype(BF16) for p in prange]

        for p in prange:
            bi, pi = units[p]
            m0 = s_ref[p]
            lhs = jnp.concatenate([z_mat[p], jnp.concatenate([r_t[p], jnp.zeros_like(r_t[p])], axis=1)],
                                  axis=0).astype(BF16)
            ws = jnp.dot(lhs, jnp.concatenate([m0.astype(BF16), v_r[p]], axis=0),
                         preferred_element_type=F32)
            u_mat = ws[:c]
            y = ws[c:] + _mm1(jnp.concatenate([q_b[p], q_k[p]], axis=1),
                              jnp.concatenate([_bd(u_mat, lane_lo), v_r[p]], axis=0))
            y_ref[rows[bi], pi * LANES:(pi + 1) * LANES] = y
            ds = jnp.dot(bk_t[p], jnp.concatenate([u_mat, vc[p]], axis=0).astype(BF16),
                         preferred_element_type=F32)
            s_ref[p] = m0 * g_col[p] + jnp.where(bd_mask, ds, 0.0)
        return carry

    lax.fori_loop(0, tt // c, chunk, 0, unroll=min(RWKV_UNROLL, tt // c))

    y = y_ref[...]
    mean = _seg_sum(y, seg_ones) * (1.0 / n)
    dlt = y - mean
    var = _seg_sum(dlt * dlt, seg_ones) * (1.0 / n)
    yn = dlt * lax.rsqrt(var + RWKV_LN_EPS) * lng_ref[...] + lnb_ref[...]
    bonus = _seg_sum(st_ref[0] * st_ref[2] * rk_ref[...], seg_ones) * st_ref[3]
    gate = _dot(_sigmoid(cols(o_gl, o_gl + RWKV_GATE_RANK)), g2_ref[...])
    o_ref[...] = ((yn + bonus) * gate).reshape(nb, tt, w).astype(o_ref.dtype)


def _rwkv(zr, w0, w2, a0, a2, g2, k_k, k_a, r_k, ln_g, ln_b, tt):
    b, t, nz = zr.shape
    w = w0.shape[-1]
    const = lambda *shape: _const_spec(shape, 1)
    return pl.pallas_call(
        _rwkv_kernel,
        grid=(t // tt,),
        in_specs=[pl.BlockSpec((b, tt, nz), lambda j: (0, j, 0)),
                  const(1, w), const(*w2.shape), const(1, w), const(*a2.shape), const(*g2.shape),
                  const(1, w), const(1, w), const(1, w), const(1, w), const(1, w)],
        out_specs=pl.BlockSpec((b, tt, w), lambda j: (0, j, 0)),
        out_shape=jax.ShapeDtypeStruct((b, t, w), BF16),
        scratch_shapes=[pltpu.VMEM((b * w // LANES, LANES, LANES), F32),
                        pltpu.VMEM((6, b * tt, w), F32),
                        pltpu.VMEM((b * tt, w), F32)],
        compiler_params=_params(1),
        name="rwkv",
    )(zr, w0, w2, a0, a2, g2, k_k, k_a, r_k, ln_g, ln_b)


def _post_attn_kernel(x_ref, og_ref, or_ref, wog_ref, wor_ref, gx_ref, wq_ref, k_ref, v_ref, wo_ref,
                      wgu_ref, wd_ref, out_ref, wgu_out_ref, wd_out_ref):
    wgu_out_ref[...] = wgu_ref[...].astype(BF16)
    wd_out_ref[...] = wd_ref[...].astype(BF16)
    tm, d = x_ref.shape
    hd = d // MEM_HEADS
    sub = min(POST_ATTN_SUBTILE_ROWS, tm)
    rows = [slice(s0, s0 + sub) for s0 in range(0, tm, sub)]
    nt_dims = (((1,), (1,)), ((), ()))
    x1 = [x_ref[r, :] + jnp.dot(og_ref[r, :], wog_ref[...], preferred_element_type=F32)
          + jnp.dot(or_ref[r, :], wor_ref[...], preferred_element_type=F32) for r in rows]
    q = [_dot(_rms_norm(x, gx_ref[...], NORM_EPS), wq_ref[...]).astype(BF16) for x in x1]
    heads = [[] for _ in rows]
    for h in range(MEM_HEADS):
        hs = slice(h * hd, (h + 1) * hd)
        s = [lax.dot_general(qi[:, hs], k_ref[:, hs], nt_dims, preferred_element_type=F32) * (hd ** -0.5)
             for qi in q]
        e = [jnp.exp(si - jnp.max(si, axis=-1, keepdims=True)) for si in s]
        p = [ei / jnp.sum(ei, axis=-1, keepdims=True) for ei in e]
        for i, pi in enumerate(p):
            heads[i].append(_dot(pi, v_ref[:, hs]))
    for i, r in enumerate(rows):
        out_ref[r, :] = x1[i] + _dot(jnp.concatenate(heads[i], axis=-1), wo_ref[...])


def _post_attn(x, o_gla, o_rwkv, wo_g, wo_r, gx, wq, kmem, vmem, wo, wgu, wd, tm):
    b, t, d = x.shape
    m = kmem.shape[1]
    nj = t // tm
    steps = b * nj
    bf16_rows = 16
    assert wgu.shape[0] % (steps * bf16_rows) == 0 and wd.shape[0] % (steps * bf16_rows) == 0
    const = lambda *shape: _const_spec(shape, 2)
    tile = lambda width: pl.BlockSpec((None, tm, width), lambda i, j: (i, j, 0))
    slab = lambda w: pl.BlockSpec((w.shape[0] // steps, w.shape[1]), lambda i, j: (i * nj + j, 0))
    return pl.pallas_call(
        _post_attn_kernel,
        grid=(b, nj),
        in_specs=[tile(d), tile(o_gla.shape[-1]), tile(o_rwkv.shape[-1]),
                  const(*wo_g.shape), const(*wo_r.shape), const(1, d), const(*wq.shape),
                  pl.BlockSpec((None, m, d), lambda i, j: (i, 0, 0)),
                  pl.BlockSpec((None, m, d), lambda i, j: (i, 0, 0)),
                  const(*wo.shape), slab(wgu), slab(wd)],
        out_specs=(tile(d), slab(wgu), slab(wd)),
        out_shape=(jax.ShapeDtypeStruct((b, t, d), F32), jax.ShapeDtypeStruct(wgu.shape, BF16),
                   jax.ShapeDtypeStruct(wd.shape, BF16)),
        compiler_params=_params(2),
        name="post_attn",
    )(x, o_gla, o_rwkv, wo_g, wo_r, gx, wq, kmem, vmem, wo, wgu, wd)


def _ffn_kernel(x_ref, g_ref, wgu_ref, wd_ref, gf_ref, out_ref, *, ff_chunk):
    x = x_ref[...]
    d_ff = wd_ref.shape[0]
    h = _rms_norm(x, g_ref[...], NORM_EPS).astype(BF16)
    acc = x
    for c0 in range(0, d_ff, ff_chunk):
        gate = jnp.dot(h, wgu_ref[:, c0:c0 + ff_chunk], preferred_element_type=F32)
        up = jnp.dot(h, wgu_ref[:, d_ff + c0:d_ff + c0 + ff_chunk], preferred_element_type=F32)
        act = (gate * _sigmoid(gate) * up).astype(BF16)
        acc = acc + jnp.dot(act, wd_ref[c0:c0 + ff_chunk, :], preferred_element_type=F32)
    out_ref[...] = _rms_norm(acc, gf_ref[...], NORM_EPS)


def _ffn(x, g, wgu, wd, gf, tm, ff_chunk):
    b, t, d = x.shape
    const = lambda *shape: _const_spec(shape, 2)
    tile = pl.BlockSpec((None, tm, d), lambda i, j: (i, j, 0))
    return pl.pallas_call(
        functools.partial(_ffn_kernel, ff_chunk=ff_chunk),
        grid=(b, t // tm),
        in_specs=[tile, const(1, d), const(*wgu.shape), const(*wd.shape), const(1, d)],
        out_specs=tile,
        out_shape=jax.ShapeDtypeStruct((b, t, d), F32),
        compiler_params=_params(2),
        name="ffn",
    )(x, g, wgu, wd, gf)


def _layer(x, mem, norm_mix_g, w_in, gla_wa2, gla_ba, gla_norm_g, rwkv_mu, rwkv_w0, rwkv_w2, rwkv_a0,
           rwkv_a2, rwkv_g2, rwkv_k_k, rwkv_k_a, rwkv_r_k, rwkv_ln_g, rwkv_ln_b, w_out, norm_mem_x_g,
           norm_mem_g, wq_mem, wkv_mem, wo_mem, norm_ffn_g, w_gate_up, w_down, final_g):
    b, t, d = x.shape
    m = mem.shape[1]
    row = lambda p: p.reshape(1, -1).astype(F32)
    vw = GLA_HEADS * GLA_DV
    wa2p = jnp.pad(gla_wa2, ((0, LANES - GLA_GATE_RANK), (0, 0))).astype(BF16)

    kmem, vmem = _mem_kv(mem.reshape(b * m, d), row(norm_mem_g), wkv_mem)
    zg, zr = _in_proj(x, row(norm_mix_g), w_in.T, wa2p, row(gla_ba), row(rwkv_mu), tm=min(ROWS_IN_PROJ, t))
    o_gla = _gla(zg, row(gla_norm_g), tt=min(ROWS_GLA, t))
    o_rwkv = _rwkv(zr, row(rwkv_w0), rwkv_w2.astype(BF16), row(rwkv_a0), rwkv_a2.astype(BF16),
                   rwkv_g2.astype(BF16), row(rwkv_k_k), row(rwkv_k_a), row(rwkv_r_k), row(rwkv_ln_g),
                   row(rwkv_ln_b), tt=min(ROWS_RWKV, t))
    w_out_b = w_out.astype(BF16)
    x2, wgu_b, wd_b = _post_attn(x, o_gla, o_rwkv, w_out_b[:vw], w_out_b[vw:], row(norm_mem_x_g),
                                 wq_mem.astype(BF16), kmem.reshape(b, m, d), vmem.reshape(b, m, d),
                                 wo_mem.astype(BF16), w_gate_up, w_down, tm=min(ROWS_POST_ATTN, t))
    return _ffn(x2, row(norm_ffn_g), wgu_b, wd_b, row(final_g), tm=min(ROWS_FFN, t), ff_chunk=FF_CHUNK)


def kernel(x, mem, norm_mix_g, w_in, gla_wa2, gla_ba, gla_norm_g, rwkv_mu, rwkv_w0, rwkv_w2, rwkv_a0, rwkv_a2, rwkv_g2, rwkv_k_k, rwkv_k_a, rwkv_r_k, rwkv_ln_g, rwkv_ln_b, w_out, norm_mem_x_g, norm_mem_g, wq_mem, wkv_mem, wo_mem, norm_ffn_g, w_gate_up, w_down, norm_final_g):
    assert norm_mix_g.shape[0] == 1, "single-layer block"
    return _layer(x, mem, norm_mix_g[0], w_in[0], gla_wa2[0], gla_ba[0], gla_norm_g[0], rwkv_mu[0],
                  rwkv_w0[0], rwkv_w2[0], rwkv_a0[0], rwkv_a2[0], rwkv_g2[0], rwkv_k_k[0], rwkv_k_a[0],
                  rwkv_r_k[0], rwkv_ln_g[0], rwkv_ln_b[0], w_out[0], norm_mem_x_g[0], norm_mem_g[0],
                  wq_mem[0], wkv_mem[0], wo_mem[0], norm_ffn_g[0], w_gate_up[0], w_down[0], norm_final_g)
```

```python
import functools
import math

import jax
import jax.numpy as jnp
from jax import lax
from jax.experimental import pallas as pl
from jax.experimental.pallas import tpu as pltpu

F32 = jnp.float32
BF16 = jnp.bfloat16

MEM_HEADS = 4
GLA_HEADS = 4
GLA_DK = 64
GLA_DV = 128
GLA_GATE_RANK = 16
GLA_GATE_NORMALIZER = 16.0
GLA_NORM_EPS = 1e-5
RWKV_HEAD = 64
RWKV_DECAY_RANK = 64
RWKV_AAA_RANK = 64
RWKV_GATE_RANK = 128
RWKV_LN_EPS = 64e-5
NORM_EPS = 1e-6

CHUNK = 64
GLA_CHUNKS_PER_ITER = 4
GLA_UNROLL = 2
SUBTILE_ROWS = 256
POST_ATTN_SUBTILE_ROWS = 512
ROWS_IN_PROJ = 512
ROWS_GLA = 1024
ROWS_RWKV = 512
ROWS_POST_ATTN = 1024
ROWS_FFN = 1024
FF_CHUNK = 256
PREP_STEPS = 4
LANES = 128
VMEM_LIMIT = 56 * 1024 * 1024


def _params(n_grid_dims):
    return pltpu.CompilerParams(dimension_semantics=("arbitrary",) * n_grid_dims,
                                vmem_limit_bytes=VMEM_LIMIT)


def _const_spec(shape, grid_rank):
    zeros = (0,) * len(shape)
    index_map = (lambda j: zeros) if grid_rank == 1 else (lambda i, j: zeros)
    return pl.BlockSpec(shape, index_map, pipeline_mode=pl.Buffered(1))


def _dot(a, b):
    return jnp.dot(a.astype(BF16), b.astype(BF16), preferred_element_type=F32)


def _rms_norm(x, g, eps):
    return x * lax.rsqrt(jnp.mean(x * x, axis=-1, keepdims=True) + eps) * g


def _sigmoid(x):
    return 1.0 / (1.0 + jnp.exp(-x))


def _log_sigmoid(x):
    return jnp.minimum(x, 0.0) - jnp.log1p(jnp.exp(-jnp.abs(x)))


def _iota2(shape, dim):
    return lax.broadcasted_iota(jnp.int32, shape, dim)


def _cumsum_operator(c):
    return (_iota2((c, 3 * c), 0) >= _iota2((c, 3 * c), 1) % c).astype(BF16)


def _chunk_cumsum(tri3, x):
    p1 = x.astype(BF16)
    rem = x - p1.astype(F32)
    p2 = rem.astype(BF16)
    p3 = (rem - p2.astype(F32)).astype(BF16)
    return jnp.dot(tri3, jnp.concatenate([p1, p2, p3], axis=0), preferred_element_type=F32)


def _prep_kernel(mem_ref, g_ref, wkv_ref, wout_ref, wq_ref, wo_ref,
                 k_ref, v_ref, wout_b_ref, wq_b_ref, wo_b_ref, m_ref, acc_ref):
    j = pl.program_id(0)
    slab = wkv_ref.shape[0]
    d = k_ref.shape[-1]

    @pl.when(j == 0)
    def _():
        m_ref[...] = _rms_norm(mem_ref[...], g_ref[...], NORM_EPS).astype(BF16)
        acc_ref[...] = jnp.zeros_like(acc_ref)

    cols = pl.ds(pl.multiple_of(j * slab, slab), slab)
    acc_ref[...] += jnp.dot(m_ref[:, cols], wkv_ref[...].astype(BF16), preferred_element_type=F32)
    wout_b_ref[...] = wout_ref[...].astype(BF16)
    wq_b_ref[...] = wq_ref[...].astype(BF16)
    wo_b_ref[...] = wo_ref[...].astype(BF16)

    @pl.when(j == pl.num_programs(0) - 1)
    def _():
        k_ref[...] = acc_ref[:, :d].astype(BF16)
        v_ref[...] = acc_ref[:, d:].astype(BF16)


def _prep(mem2d, g, wkv, w_out, wq, wo):
    n, d = mem2d.shape
    steps = PREP_STEPS
    slab = lambda w: pl.BlockSpec((w.shape[0] // steps, w.shape[1]), lambda j: (j, 0))
    whole = lambda *shape: pl.BlockSpec(shape, lambda j: (0,) * len(shape))
    bf = lambda w: jax.ShapeDtypeStruct(w.shape, BF16)
    return pl.pallas_call(
        _prep_kernel,
        grid=(steps,),
        in_specs=[_const_spec((n, d), 1), _const_spec((1, d), 1), slab(wkv), slab(w_out), slab(wq), slab(wo)],
        out_specs=(whole(n, d), whole(n, d), slab(w_out), slab(wq), slab(wo)),
        out_shape=(jax.ShapeDtypeStruct((n, d), BF16), jax.ShapeDtypeStruct((n, d), BF16),
                   bf(w_out), bf(wq), bf(wo)),
        scratch_shapes=[pltpu.VMEM((n, d), BF16), pltpu.VMEM((n, wkv.shape[1]), F32)],
        compiler_params=_params(1),
        name="prep",
    )(mem2d, g, wkv, w_out, wq, wo)


def _in_proj_kernel(x_ref, g_ref, w_ref, wa2_ref, ba_ref, mu_ref, zg_ref, zr_ref, prev_ref, wg_ref, wr_ref):
    tm = x_ref.shape[0]
    n_main = zg_ref.shape[-1] - wa2_ref.shape[-1]
    n_gla = n_main + GLA_GATE_RANK

    @pl.when(jnp.logical_and(pl.program_id(0) == 0, pl.program_id(1) == 0))
    def _():
        for dst, c_first in ((wg_ref, 0), (wr_ref, n_gla)):
            for c0 in range(0, dst.shape[1], LANES):
                slab = w_ref[c_first + c0:c_first + c0 + LANES, :]
                dst[:, c0:c0 + LANES] = slab.T.astype(BF16)

    @pl.when(pl.program_id(1) == 0)
    def _():
        prev_ref[...] = jnp.zeros_like(prev_ref)

    sub = min(SUBTILE_ROWS, tm)
    prev = prev_ref[0:1, :]
    for s in range(tm // sub):
        rows = slice(s * sub, (s + 1) * sub)
        h = _rms_norm(x_ref[rows, :], g_ref[...], NORM_EPS).astype(BF16)

        zr = jnp.dot(h, wr_ref[...], preferred_element_type=F32)
        rolled = pltpu.roll(zr, shift=1, axis=0)
        first = _iota2(zr.shape, 0) == 0
        shifted = jnp.where(first, jnp.broadcast_to(prev, zr.shape), rolled)
        zr_ref[rows, :] = zr + (shifted - zr) * mu_ref[...]
        prev = zr[sub - 1:sub, :]

        a_lo = jnp.dot(h, wg_ref[:, n_main:], preferred_element_type=F32)
        pre = _dot(a_lo, wa2_ref[...]) + ba_ref[...]
        zg_ref[rows, n_main:] = _log_sigmoid(pre) * (1.0 / GLA_GATE_NORMALIZER)
        zg_ref[rows, :n_main] = jnp.dot(h, wg_ref[:, :n_main], preferred_element_type=F32)
    prev_ref[0:1, :] = prev


def _in_proj(x, g, w_in_t, wa2p, ba, mu, tm):
    b, t, d = x.shape
    nr = mu.shape[1]
    n_main = w_in_t.shape[0] - nr - GLA_GATE_RANK
    ng = n_main + wa2p.shape[1]
    const = lambda *shape: _const_spec(shape, 2)
    return pl.pallas_call(
        _in_proj_kernel,
        grid=(b, t // tm),
        in_specs=[
            pl.BlockSpec((None, tm, d), lambda i, j: (i, j, 0)),
            const(1, d), const(*w_in_t.shape), const(*wa2p.shape),
            const(1, wa2p.shape[1]), const(1, nr),
        ],
        out_specs=(pl.BlockSpec((None, tm, ng), lambda i, j: (i, j, 0)),
                   pl.BlockSpec((None, tm, nr), lambda i, j: (i, j, 0))),
        out_shape=(jax.ShapeDtypeStruct((b, t, ng), F32), jax.ShapeDtypeStruct((b, t, nr), F32)),
        scratch_shapes=[pltpu.VMEM((8, nr), F32),
                        pltpu.VMEM((d, n_main + LANES), BF16),
                        pltpu.VMEM((d, nr), BF16)],
        compiler_params=_params(2),
        name="in_proj",
    )(x, g, w_in_t, wa2p, ba, mu)


def _gla_kernel(z_ref, ng_ref, o_ref, s_ref):
    nb, tt, _ = z_ref.shape
    kw = GLA_HEADS * GLA_DK
    vw = GLA_HEADS * GLA_DV
    c = CHUNK
    pairs = GLA_HEADS // 2
    pk = 2 * GLA_DK
    pv = 2 * GLA_DV
    cpi = GLA_CHUNKS_PER_ITER

    @pl.when(pl.program_id(0) == 0)
    def _():
        s_ref[...] = jnp.zeros_like(s_ref)

    tri3 = _cumsum_operator(c)
    causal = _iota2((c, pk), 0) >= _iota2((c, pk), 1) % c
    lane_lo = _iota2((c, pk), 1) < GLA_DK
    v_lo = _iota2((c, pv), 1) < GLA_DV
    s_mask = (_iota2((pv, pk), 0) // GLA_DV) == (_iota2((pv, pk), 1) // GLA_DK)
    norm_g = ng_ref[...]
    nt_dims = (((1,), (1,)), ((), ()))
    tn_dims = (((0,), (0,)), ((), ()))

    def step(it, carry):
        units = [(b, j, p) for b in range(nb) for j in range(cpi) for p in range(pairs)]
        rows = {(b, j): pl.ds(pl.multiple_of((it * cpi + j) * c, c), c) for b in range(nb) for j in range(cpi)}
        q_dec, k_inv, k_tail, dec, vv = {}, {}, {}, {}, {}
        for (b, j), rs in rows.items():
            q = z_ref[b, rs, 0:kw] * (GLA_DK ** -0.5)
            k = z_ref[b, rs, kw:2 * kw]
            log_a = z_ref[b, rs, 2 * kw + 2 * vw:3 * kw + 2 * vw]
            bcum = _chunk_cumsum(tri3, log_a)
            blast = bcum[c - 1:c, :]
            qd = q * jnp.exp(bcum)
            ki = k * jnp.exp(-bcum)
            kt = k * jnp.exp(blast - bcum)
            dc = jnp.exp(blast)
            for p in range(pairs):
                ks = slice(p * pk, (p + 1) * pk)
                q_dec[b, j, p] = qd[:, ks].astype(BF16)
                k_inv[b, j, p] = ki[:, ks]
                k_tail[b, j, p] = kt[:, ks].astype(BF16)
                dec[b, j, p] = dc[:, ks]
                vv[b, j, p] = z_ref[b, rs, 2 * kw + p * pv:2 * kw + (p + 1) * pv].astype(BF16)

        o_intra, d_state = {}, {}
        for u in units:
            scores = jnp.where(causal, lax.dot_general(q_dec[u], _bd(k_inv[u], lane_lo), nt_dims,
                                                       preferred_element_type=F32), 0.0)
            zero = jnp.zeros_like(vv[u])
            v_bd = jnp.concatenate([jnp.where(v_lo, vv[u], zero), jnp.where(v_lo, zero, vv[u])], axis=0)
            o_intra[u] = _mm1(scores, v_bd)
            d_state[u] = jnp.where(s_mask, lax.dot_general(vv[u], k_tail[u], tn_dims,
                                                           preferred_element_type=F32), 0.0)

        for b in range(nb):
            s_cur = [s_ref[b * pairs + p] for p in range(pairs)]
            for j in range(cpi):
                outs = []
                for p in range(pairs):
                    u = (b, j, p)
                    o = o_intra[u] + lax.dot_general(q_dec[u], s_cur[p].astype(BF16), nt_dims,
                                                     preferred_element_type=F32)
                    s_cur[p] = s_cur[p] * dec[u] + d_state[u]
                    for h in range(2):
                        oh = o[:, h * GLA_DV:(h + 1) * GLA_DV]
                        oh = oh * lax.rsqrt(jnp.mean(oh * oh, axis=-1, keepdims=True) + GLA_NORM_EPS) * norm_g
                        g0 = 2 * kw + vw + (2 * p + h) * GLA_DV
                        gh = z_ref[b, rows[b, j], g0:g0 + GLA_DV]
                        outs.append(oh * (gh * _sigmoid(gh)))
                o_ref[b, rows[b, j], :] = jnp.concatenate(outs, axis=-1).astype(o_ref.dtype)
            for p in range(pairs):
                s_ref[b * pairs + p] = s_cur[p]
        return carry

    lax.fori_loop(0, tt // (c * cpi), step, 0, unroll=min(GLA_UNROLL, tt // (c * cpi)))


def _gla(zg, norm_g, tt):
    b, t, n = zg.shape
    vw = GLA_HEADS * GLA_DV
    return pl.pallas_call(
        _gla_kernel,
        grid=(t // tt,),
        in_specs=[pl.BlockSpec((b, tt, n), lambda j: (0, j, 0)),
                  _const_spec((1, GLA_DV), 1)],
        out_specs=pl.BlockSpec((b, tt, vw), lambda j: (0, j, 0)),
        out_shape=jax.ShapeDtypeStruct((b, t, vw), BF16),
        scratch_shapes=[pltpu.VMEM((b * GLA_HEADS // 2, 2 * GLA_DV, 2 * GLA_DK), F32)],
        compiler_params=_params(1),
        name="gla",
    )(zg, norm_g)


def _seg_sum(x, seg_ones):
    xb = x.astype(BF16)
    blk = seg_ones.shape[0]
    parts = []
    for s in range(x.shape[1] // blk):
        sl = slice(s * blk, (s + 1) * blk)
        parts.append(jnp.dot(xb[:, sl], seg_ones, preferred_element_type=F32))
    return jnp.concatenate(parts, axis=-1)


def _block_diag(z, lane_lo):
    zero = jnp.zeros_like(z)
    return jnp.concatenate([jnp.where(lane_lo, z, zero), jnp.where(lane_lo, zero, z)], axis=0)


def _bd(z, lane_lo):
    return _block_diag(z.astype(BF16), lane_lo)


def _mm1(x, r):
    return jnp.dot(x.astype(BF16), r, preferred_element_type=F32)


def _rwkv_kernel(z_ref, w0_ref, w2_ref, a0_ref, a2_ref, g2_ref, kk_ref, ka_ref, rk_ref, lng_ref, lnb_ref,
                 o_ref, s_ref, y_ref):
    nb, tt, _ = z_ref.shape
    w = w0_ref.shape[-1]
    n = RWKV_HEAD
    pairs = w // LANES
    c = CHUNK
    o_wl = 3 * w
    o_al = o_wl + RWKV_DECAY_RANK
    o_gl = o_al + RWKV_AAA_RANK

    @pl.when(pl.program_id(0) == 0)
    def _():
        s_ref[...] = jnp.zeros_like(s_ref)

    seg = 2 * LANES
    seg_ones = (_iota2((seg, seg), 0) // n == _iota2((seg, seg), 1) // n).astype(BF16)

    cols = lambda lo, hi: z_ref[:, :, lo:hi].reshape(nb * tt, hi - lo)
    r = cols(0, w)
    k = cols(w, 2 * w)
    v = cols(2 * w, 3 * w)
    u = w0_ref[...] + _dot(jnp.tanh(cols(o_wl, o_al)), w2_ref[...])
    lw = -math.exp(-0.5) * _sigmoid(u)
    a = _sigmoid(a0_ref[...] + _dot(cols(o_al, o_gl), a2_ref[...]))
    kk = k * kk_ref[...]
    kk = kk * lax.rsqrt(jnp.maximum(_seg_sum(kk * kk, seg_ones), 1e-24))
    k2 = k * (1.0 + (a - 1.0) * ka_ref[...])
    neg_kk = -kk
    kk_a = kk * a

    tri3 = _cumsum_operator(c)
    row = _iota2((c, LANES), 0)
    col = _iota2((c, LANES), 1) % c
    lower = row >= col
    strict = row > col
    eye = (row == col).astype(F32)
    blk16 = (row // 16) == (col // 16)
    blk32 = (row // 32) == (col // 32)
    off16 = jnp.logical_and(blk32, jnp.logical_not(blk16))
    lane_lo = _iota2((c, LANES), 1) < n
    bd_mask = (_iota2((LANES, LANES), 0) // n) == (_iota2((LANES, LANES), 1) // n)
    units = [(b, p) for b in range(nb) for p in range(pairs)]
    prange = range(len(units))
    nt_dims = (((1,), (1,)), ((), ()))

    def chunk(ci):
        rows = [slice(b * tt + ci * c, b * tt + (ci + 1) * c) for b in range(nb)]
        a_t, r_t, b_t, k_t, b_h, k_h, vc, g_c = [], [], [], [], [], [], [], []
        for b in range(nb):
            rc = r[rows[b], :]
            lwc = lw[rows[b], :]
            kc = k2[rows[b], :]
            ac = neg_kk[rows[b], :]
            bc = kk_a[rows[b], :]
            cum = _chunk_cumsum(tri3, lwc)
            last = cum[c - 1:c, :]
            e_neg = jnp.exp(-cum)
            e_tail = jnp.exp(last - cum)
            r_tb = rc * jnp.exp(cum)
            a_tb = ac * jnp.exp(cum - lwc)
            vb = v[rows[b], :]
            for p in range(pairs):
                sl = slice(p * LANES, (p + 1) * LANES)
                a_t.append(a_tb[:, sl])
                r_t.append(r_tb[:, sl])
                b_t.append(bc[:, sl] * e_neg[:, sl])
                k_t.append(kc[:, sl] * e_neg[:, sl])
                b_h.append(bc[:, sl] * e_tail[:, sl])
                k_h.append(kc[:, sl] * e_tail[:, sl])
                vc.append(vb[:, sl])
                g_c.append(jnp.exp(last[:, sl]))

        a_ab, a_ak, q_b, q_k = [], [], [], []
        for p in prange:
            lhs = jnp.concatenate([a_t[p], r_t[p]], axis=0).astype(BF16)
            rhs = jnp.concatenate([_bd(b_t[p], lane_lo), _bd(k_t[p], lane_lo)], axis=0)
            aa = lax.dot_general(lhs, rhs, nt_dims, preferred_element_type=F32)
            a_ab.append(jnp.where(strict, aa[:c, :LANES], 0.0))
            a_ak.append(jnp.where(strict, aa[:c, LANES:], 0.0))
            q_b.append(jnp.where(lower, aa[c:, :LANES], 0.0))
            q_k.append(jnp.where(lower, aa[c:, LANES:], 0.0))

        ad = [jnp.where(blk16, x, 0.0) for x in a_ab]
        pw = [_mm1(ad[p], _bd(ad[p], lane_lo)) for p in prange]
        t = [eye + ad[p] for p in prange]
        for _ in range(2):
            both = [_mm1(jnp.concatenate([pw[p], t[p]], axis=0), _bd(pw[p], lane_lo)) for p in prange]
            pw = [x[:c] for x in both]
            t = [t[p] + both[p][c:] for p in prange]
        t = [t[p] + _mm1(t[p], _bd(pw[p], lane_lo)) for p in prange]
        for msk in (off16, jnp.logical_not(blk32)):
            te = [_mm1(t[p], _bd(jnp.where(msk, a_ab[p], 0.0), lane_lo)) for p in prange]
            t = [t[p] + _mm1(te[p], _bd(t[p], lane_lo)) for p in prange]

        v_r = [_bd(vc[p], lane_lo) for p in prange]
        z_mat = [_mm1(t[p], jnp.concatenate([_bd(a_t[p], lane_lo), _bd(a_ak[p], lane_lo)], axis=1))
                 for p in prange]
        g_col = [jnp.broadcast_to(g_c[p], (LANES, LANES)).T for p in prange]
        bk_t = [jnp.concatenate([b_h[p], k_h[p]], axis=0).T.astype(BF16) for p in prange]

        for p in prange:
            bi, pi = units[p]
            m0 = s_ref[p]
            lhs = jnp.concatenate([z_mat[p], jnp.concatenate([r_t[p], jnp.zeros_like(r_t[p])], axis=1)],
                                  axis=0).astype(BF16)
            ws = jnp.dot(lhs, jnp.concatenate([m0.astype(BF16), v_r[p]], axis=0),
                         preferred_element_type=F32)
            u_mat = ws[:c]
            y = ws[c:] + _mm1(jnp.concatenate([q_b[p], q_k[p]], axis=1),
                              jnp.concatenate([_bd(u_mat, lane_lo), v_r[p]], axis=0))
            y_ref[rows[bi], pi * LANES:(pi + 1) * LANES] = y
            ds = jnp.dot(bk_t[p], jnp.concatenate([u_mat, vc[p]], axis=0).astype(BF16),
                         preferred_element_type=F32)
            s_ref[p] = m0 * g_col[p] + jnp.where(bd_mask, ds, 0.0)

    for ci in range(tt // c):
        chunk(ci)

    y = y_ref[...]
    mean = _seg_sum(y, seg_ones) * (1.0 / n)
    dlt = y - mean
    var = _seg_sum(dlt * dlt, seg_ones) * (1.0 / n)
    yn = dlt * lax.rsqrt(var + RWKV_LN_EPS) * lng_ref[...] + lnb_ref[...]
    bonus = _seg_sum(r * k2 * rk_ref[...], seg_ones) * v
    gate = _dot(_sigmoid(cols(o_gl, o_gl + RWKV_GATE_RANK)), g2_ref[...])
    o_ref[...] = ((yn + bonus) * gate).reshape(nb, tt, w).astype(o_ref.dtype)


def _rwkv(zr, w0, w2, a0, a2, g2, k_k, k_a, r_k, ln_g, ln_b, tt):
    b, t, nz = zr.shape
    w = w0.shape[-1]
    const = lambda *shape: _const_spec(shape, 1)
    return pl.pallas_call(
        _rwkv_kernel,
        grid=(t // tt,),
        in_specs=[pl.BlockSpec((b, tt, nz), lambda j: (0, j, 0)),
                  const(1, w), const(*w2.shape), const(1, w), const(*a2.shape), const(*g2.shape),
                  const(1, w), const(1, w), const(1, w), const(1, w), const(1, w)],
        out_specs=pl.BlockSpec((b, tt, w), lambda j: (0, j, 0)),
        out_shape=jax.ShapeDtypeStruct((b, t, w), BF16),
        scratch_shapes=[pltpu.VMEM((b * w // LANES, LANES, LANES), F32),
                        pltpu.VMEM((b * tt, w), F32)],
        compiler_params=_params(1),
        name="rwkv",
    )(zr, w0, w2, a0, a2, g2, k_k, k_a, r_k, ln_g, ln_b)


def _post_attn_kernel(x_ref, og_ref, or_ref, wout_ref, gx_ref, wq_ref, k_ref, v_ref, wo_ref,
                      wgu_ref, wd_ref, out_ref, wgu_out_ref, wd_out_ref):
    wgu_out_ref[...] = wgu_ref[...].astype(BF16)
    wd_out_ref[...] = wd_ref[...].astype(BF16)
    tm, d = x_ref.shape
    hd = d // MEM_HEADS
    sub = min(POST_ATTN_SUBTILE_ROWS, tm)
    rows = [slice(s0, s0 + sub) for s0 in range(0, tm, sub)]
    nt_dims = (((1,), (1,)), ((), ()))
    ng = og_ref.shape[-1]
    x1 = [x_ref[r, :] + jnp.dot(og_ref[r, :], wout_ref[:ng, :], preferred_element_type=F32)
          + jnp.dot(or_ref[r, :], wout_ref[ng:, :], preferred_element_type=F32) for r in rows]
    q = [_dot(_rms_norm(x, gx_ref[...], NORM_EPS), wq_ref[...]).astype(BF16) for x in x1]
    heads = [[] for _ in rows]
    for h in range(MEM_HEADS):
        hs = slice(h * hd, (h + 1) * hd)
        s = [lax.dot_general(qi[:, hs], k_ref[:, hs], nt_dims, preferred_element_type=F32) * (hd ** -0.5)
             for qi in q]
        e = [jnp.exp(si - jnp.max(si, axis=-1, keepdims=True)) for si in s]
        p = [ei / jnp.sum(ei, axis=-1, keepdims=True) for ei in e]
        for i, pi in enumerate(p):
            heads[i].append(_dot(pi, v_ref[:, hs]))
    for i, r in enumerate(rows):
        out_ref[r, :] = x1[i] + _dot(jnp.concatenate(heads[i], axis=-1), wo_ref[...])


def _post_attn(x, o_gla, o_rwkv, w_out, gx, wq, kmem, vmem, wo, wgu, wd, tm):
    b, t, d = x.shape
    m = kmem.shape[1]
    nj = t // tm
    steps = b * nj
    bf16_rows = 16
    assert wgu.shape[0] % (steps * bf16_rows) == 0 and wd.shape[0] % (steps * bf16_rows) == 0
    const = lambda *shape: _const_spec(shape, 2)
    tile = lambda width: pl.BlockSpec((None, tm, width), lambda i, j: (i, j, 0))
    slab = lambda w: pl.BlockSpec((w.shape[0] // steps, w.shape[1]), lambda i, j: (i * nj + j, 0))
    return pl.pallas_call(
        _post_attn_kernel,
        grid=(b, nj),
        in_specs=[tile(d), tile(o_gla.shape[-1]), tile(o_rwkv.shape[-1]),
                  const(*w_out.shape), const(1, d), const(*wq.shape),
                  pl.BlockSpec((None, m, d), lambda i, j: (i, 0, 0)),
                  pl.BlockSpec((None, m, d), lambda i, j: (i, 0, 0)),
                  const(*wo.shape), slab(wgu), slab(wd)],
        out_specs=(tile(d), slab(wgu), slab(wd)),
        out_shape=(jax.ShapeDtypeStruct((b, t, d), F32), jax.ShapeDtypeStruct(wgu.shape, BF16),
                   jax.ShapeDtypeStruct(wd.shape, BF16)),
        compiler_params=_params(2),
        name="post_attn",
    )(x, o_gla, o_rwkv, w_out, gx, wq, kmem, vmem, wo, wgu, wd)


def _ffn_kernel(x_ref, g_ref, wgu_ref, wd_ref, gf_ref, out_ref, *, ff_chunk):
    x = x_ref[...]
    d_ff = wd_ref.shape[0]
    h = _rms_norm(x, g_ref[...], NORM_EPS).astype(BF16)
    acc = x
    for c0 in range(0, d_ff, ff_chunk):
        gate = jnp.dot(h, wgu_ref[:, c0:c0 + ff_chunk], preferred_element_type=F32)
        up = jnp.dot(h, wgu_ref[:, d_ff + c0:d_ff + c0 + ff_chunk], preferred_element_type=F32)
        act = (gate * _sigmoid(gate) * up).astype(BF16)
        acc = acc + jnp.dot(act, wd_ref[c0:c0 + ff_chunk, :], preferred_element_type=F32)
    out_ref[...] = _rms_norm(acc, gf_ref[...], NORM_EPS)


def _ffn(x, g, wgu, wd, gf, tm, ff_chunk):
    b, t, d = x.shape
    const = lambda *shape: _const_spec(shape, 2)
    tile = pl.BlockSpec((None, tm, d), lambda i, j: (i, j, 0))
    return pl.pallas_call(
        functools.partial(_ffn_kernel, ff_chunk=ff_chunk),
        grid=(b, t // tm),
        in_specs=[tile, const(1, d), const(*wgu.shape), const(*wd.shape), const(1, d)],
        out_specs=tile,
        out_shape=jax.ShapeDtypeStruct((b, t, d), F32),
        compiler_params=_params(2),
        name="ffn",
    )(x, g, wgu, wd, gf)


def _layer(x, mem, norm_mix_g, w_in, gla_wa2, gla_ba, gla_norm_g, rwkv_mu, rwkv_w0, rwkv_w2, rwkv_a0,
           rwkv_a2, rwkv_g2, rwkv_k_k, rwkv_k_a, rwkv_r_k, rwkv_ln_g, rwkv_ln_b, w_out, norm_mem_x_g,
           norm_mem_g, wq_mem, wkv_mem, wo_mem, norm_ffn_g, w_gate_up, w_down, final_g):
    b, t, d = x.shape
    m = mem.shape[1]
    row = lambda p: p.reshape(1, -1).astype(F32)
    wa2p = jnp.pad(gla_wa2, ((0, LANES - GLA_GATE_RANK), (0, 0))).astype(BF16)

    kmem, vmem, w_out_b, wq_b, wo_b = _prep(mem.reshape(b * m, d), row(norm_mem_g), wkv_mem, w_out, wq_mem, wo_mem)
    zg, zr = _in_proj(x, row(norm_mix_g), w_in.T, wa2p, row(gla_ba), row(rwkv_mu), tm=min(ROWS_IN_PROJ, t))
    o_gla = _gla(zg, row(gla_norm_g), tt=min(ROWS_GLA, t))
    o_rwkv = _rwkv(zr, row(rwkv_w0), rwkv_w2.astype(BF16), row(rwkv_a0), rwkv_a2.astype(BF16),
                   rwkv_g2.astype(BF16), row(rwkv_k_k), row(rwkv_k_a), row(rwkv_r_k), row(rwkv_ln_g),
                   row(rwkv_ln_b), tt=min(ROWS_RWKV, t))
    x2, wgu_b, wd_b = _post_attn(x, o_gla, o_rwkv, w_out_b, row(norm_mem_x_g), wq_b, kmem.reshape(b, m, d),
                                 vmem.reshape(b, m, d), wo_b, w_gate_up, w_down, tm=min(ROWS_POST_ATTN, t))
    return _ffn(x2, row(norm_ffn_g), wgu_b, wd_b, row(final_g), tm=min(ROWS_FFN, t), ff_chunk=FF_CHUNK)


def kernel(x, mem, norm_mix_g, w_in, gla_wa2, gla_ba, gla_norm_g, rwkv_mu, rwkv_w0, rwkv_w2, rwkv_a0, rwkv_a2, rwkv_g2, rwkv_k_k, rwkv_k_a, rwkv_r_k, rwkv_ln_g, rwkv_ln_b, w_out, norm_mem_x_g, norm_mem_g, wq_mem, wkv_mem, wo_mem, norm_ffn_g, w_gate_up, w_down, norm_final_g):
    assert norm_mix_g.shape[0] == 1, "single-layer block"
    return _layer(x, mem, norm_mix_g[0], w_in[0], gla_wa2[0], gla_ba[0], gla_norm_g[0], rwkv_mu[0],
                  rwkv_w0[0], rwkv_w2[0], rwkv_a0[0], rwkv_a2[0], rwkv_g2[0], rwkv_k_k[0], rwkv_k_a[0],
                  rwkv_r_k[0], rwkv_ln_g[0], rwkv_ln_b[0], w_out[0], norm_mem_x_g[0], norm_mem_g[0],
                  wq_mem[0], wkv_mem[0], wo_mem[0], norm_ffn_g[0], w_gate_up[0], w_down[0], norm_final_g)
```
